```python
import math
import jax, jax.numpy as jnp
from jax import lax
import numpy as np

D_MODEL = 2048
BATCH = 1
SEQ = 8192
DEPTH = 2
DEC_BATCH = 16
DEC_SEQ = 2048
PAST_LEN = 128

D_FNET = D_MODEL // 2
FNET_GROUPS = 4
FNET_GROUP_DIM = D_FNET // FNET_GROUPS
D_HGRN = D_MODEL // 2
HGRN_HEAD_DIM = 128
HGRN_HEADS = D_HGRN // HGRN_HEAD_DIM
HGRN_CHUNK = 64
EVEN_WIDTHS = (D_FNET, D_FNET, D_HGRN, D_HGRN, D_HGRN, D_HGRN, D_HGRN)
D_EVEN_IN = sum(EVEN_WIDTHS)
D_EVEN_MIX = D_FNET + D_HGRN
DIFF_HEADS = 8
DIFF_HEAD_DIM = D_MODEL // (2 * DIFF_HEADS)
DIFF_V_DIM = 2 * DIFF_HEAD_DIM
D_ATTN = DIFF_HEADS * DIFF_V_DIM
D_ODD_IN = 4 * D_ATTN
Q_BLOCK = 128
N_EVEN = (DEPTH + 1) // 2
N_ODD = DEPTH // 2
RMS_EPS = 1e-6

kernel_name = "hybrid_fnet_hgrn2_diffattn_encoder"


def rmsnorm(x, g):
    x32 = x.astype(jnp.float32)
    inv = lax.rsqrt(jnp.mean(x32 * x32, axis=-1, keepdims=True) + RMS_EPS)
    return (x32 * inv * g.astype(jnp.float32)).astype(x.dtype)


def split_cols(t, widths):
    idx = [int(v) for v in np.cumsum(widths)[:-1]]
    return jnp.split(t, idx, axis=-1)


def fnet_mix(u):
    b, l, _ = u.shape
    ug = u.astype(jnp.float32).reshape(b, l, FNET_GROUPS, FNET_GROUP_DIM)
    y = jnp.fft.fft2(ug, axes=(1, 3), norm="ortho").real
    return y.reshape(b, l, D_FNET)


def hgrn2_scan(q, k, v, log_f):
    b, l, h, dk = q.shape
    dv = v.shape[-1]
    n = l // HGRN_CHUNK

    def to_chunks(t):
        return jnp.moveaxis(t.reshape(b, n, HGRN_CHUNK, h, t.shape[-1]), 1, 0)

    xs = tuple(to_chunks(t) for t in (q, k, v, log_f))
    lower = jnp.tril(jnp.ones((HGRN_CHUNK, HGRN_CHUNK), dtype=bool))

    def step(state, chunk):
        qn, kn, vn, gn = chunk
        cum = jnp.cumsum(gn, axis=1)
        last = cum[:, -1:]
        q_dec = qn * jnp.exp(cum)
        k_dec = kn * jnp.exp(-cum)
        scores = jnp.einsum("bthk,bshk->bhts", q_dec, k_dec)
        scores = jnp.where(lower, scores, 0.0)
        o = (jnp.einsum("bhts,bshv->bthv", scores, vn)
             + jnp.einsum("bthk,bhkv->bthv", q_dec, state))
        k_to_end = kn * jnp.exp(last - cum)
        state = (state * jnp.exp(last[:, 0])[..., None]
                 + jnp.einsum("bshk,bshv->bhkv", k_to_end, vn))
        return state, o

    s0 = jnp.zeros((b, h, dk, dv), jnp.float32)
    _, o = lax.scan(step, s0, xs)
    return jnp.moveaxis(o, 0, 1).reshape(b, l, h, dv)


def hgrn2_mix(q_raw, i_raw, f_fwd_raw, f_bwd_raw, lb_fwd, lb_bwd, g_norm):
    b, l, _ = q_raw.shape

    def heads(t):
        return t.astype(jnp.float32).reshape(b, l, HGRN_HEADS, HGRN_HEAD_DIM)

    q = jax.nn.silu(heads(q_raw))
    v = heads(i_raw)

    def gates(f_raw, lb):
        lb = lb.astype(jnp.float32).reshape(HGRN_HEADS, HGRN_HEAD_DIM)
        f = lb + (1.0 - lb) * jax.nn.sigmoid(heads(f_raw))
        return 1.0 - f, jnp.log(f)

    k_f, g_f = gates(f_fwd_raw, lb_fwd)
    k_b, g_b = gates(f_bwd_raw, lb_bwd)
    o_fwd = hgrn2_scan(q, k_f, v, g_f)
    rev = lambda t: jnp.flip(t, axis=1)
    o_bwd = rev(hgrn2_scan(rev(q), rev(k_b), rev(v), rev(g_b)))
    o = rmsnorm(o_fwd + o_bwd, g_norm)
    return o.reshape(b, l, D_HGRN)


def alibi_slopes(n_heads):
    return jnp.exp2(-8.0 * (jnp.arange(n_heads, dtype=jnp.float32) + 1.0) / n_heads)


def diff_attention(q, k, v, lam, subln_g, lambda_init):
    b, l = q.shape[:2]
    nb = l // Q_BLOCK
    key_pos = jnp.arange(l)
    slopes = alibi_slopes(DIFF_HEADS)
    scale = DIFF_HEAD_DIM ** -0.5
    q_blocks = jnp.moveaxis(q.reshape(b, nb, Q_BLOCK, DIFF_HEADS, 2, DIFF_HEAD_DIM), 1, 0)
    starts = jnp.arange(nb) * Q_BLOCK

    def block(args):
        q_blk, start = args
        q_pos = start + jnp.arange(Q_BLOCK)
        dist = jnp.abs(q_pos[:, None] - key_pos[None, :]).astype(jnp.float32)
        bias = -slopes[:, None, None] * dist
        s = jnp.einsum("bqhcd,bkhcd->bhcqk", q_blk, k) * scale + bias[None, :, None]
        p = jax.nn.softmax(s, axis=-1)
        attn = p[:, :, 0] - lam * p[:, :, 1]
        return jnp.einsum("bhqk,bkhv->bqhv", attn, v)

    o = lax.map(block, (q_blocks, starts))
    o = jnp.moveaxis(o, 0, 1).reshape(b, l, DIFF_HEADS, DIFF_V_DIM)
    o = rmsnorm(o, subln_g) * (1.0 - lambda_init)
    return o.reshape(b, l, D_ATTN)


def even_layer(x, w_in, w_out, g_pre, g_post, lb_fwd, lb_bwd, g_hgrn):
    h = rmsnorm(x, g_pre)
    proj = jnp.einsum("bld,de->ble", h, w_in)
    u_a, gate_a, q, i, f_fwd, f_bwd, gate_b = split_cols(proj, EVEN_WIDTHS)
    y_a = fnet_mix(u_a).astype(x.dtype) * jax.nn.silu(gate_a)
    y_b = hgrn2_mix(q, i, f_fwd, f_bwd, lb_fwd, lb_bwd, g_hgrn).astype(x.dtype) * jax.nn.silu(gate_b)
    y = jnp.einsum("ble,ed->bld", jnp.concatenate([y_a, y_b], axis=-1), w_out)
    return x + rmsnorm(y, g_post)


def odd_layer(x, w_in, w_out, g_pre, g_post, lq1, lk1, lq2, lk2, subln_g, lambda_init):
    b, l, _ = x.shape
    h = rmsnorm(x, g_pre)
    proj = jnp.einsum("bld,de->ble", h, w_in)
    q, k, v, gate = split_cols(proj, (D_ATTN, D_ATTN, D_ATTN, D_ATTN))
    q = q.astype(jnp.float32).reshape(b, l, DIFF_HEADS, 2, DIFF_HEAD_DIM)
    k = k.astype(jnp.float32).reshape(b, l, DIFF_HEADS, 2, DIFF_HEAD_DIM)
    v = v.astype(jnp.float32).reshape(b, l, DIFF_HEADS, DIFF_V_DIM)
    f32 = lambda t: t.astype(jnp.float32)
    lam = (jnp.exp(jnp.sum(f32(lq1) * f32(lk1))) - jnp.exp(jnp.sum(f32(lq2) * f32(lk2)))
           + lambda_init)
    o = diff_attention(q, k, v, lam, subln_g, lambda_init).astype(x.dtype) * jax.nn.silu(gate)
    y = jnp.einsum("ble,ed->bld", o, w_out)
    return x + rmsnorm(y, g_post)


def trunk(x, ev_w_in, ev_w_out, ev_norm_pre, ev_norm_post, hgrn_lb_logits, hgrn_norm,
          od_w_in, od_w_out, od_norm_pre, od_norm_post,
          lambda_q1, lambda_k1, lambda_q2, lambda_k2, subln):
    lb = jnp.cumsum(jax.nn.softmax(hgrn_lb_logits.astype(jnp.float32), axis=1), axis=1)
    for layer in range(DEPTH):
        if layer % 2 == 0:
            e = layer // 2
            x = even_layer(x, ev_w_in[e], ev_w_out[e], ev_norm_pre[e], ev_norm_post[e],
                           lb[0, layer], lb[1, layer], hgrn_norm[e])
        else:
            o = layer // 2
            lambda_init = 0.8 - 0.6 * math.exp(-0.3 * layer)
            x = odd_layer(x, od_w_in[o], od_w_out[o], od_norm_pre[o], od_norm_post[o],
                          lambda_q1[o], lambda_k1[o], lambda_q2[o], lambda_k2[o], subln[o],
                          lambda_init)
    return x


def setup_inputs(seed: int = 0) -> dict:
    key = jax.random.key(seed)
    ks = jax.random.split(key, 20)
    nrm = lambda k, shape, s: jax.random.normal(k, shape, jnp.float32) * s
    gain = lambda k, shape: 1.0 + 0.05 * jax.random.normal(k, shape, jnp.float32)
    return {
        "x_prompt": nrm(ks[0], (BATCH, SEQ, D_MODEL), 1.0),
        "x_sample": nrm(ks[1], (DEC_BATCH, DEC_SEQ, D_MODEL), 1.0),
        "ev_w_in": nrm(ks[2], (N_EVEN, D_MODEL, D_EVEN_IN), D_MODEL ** -0.5),
        "ev_w_out": nrm(ks[3], (N_EVEN, D_EVEN_MIX, D_MODEL), D_EVEN_MIX ** -0.5),
        "ev_norm_pre": gain(ks[4], (N_EVEN, D_MODEL)),
        "ev_norm_post": gain(ks[5], (N_EVEN, D_MODEL)),
        "hgrn_lb_logits": nrm(ks[6], (2, DEPTH + 1, D_HGRN), 0.1),
        "hgrn_norm": gain(ks[7], (N_EVEN, HGRN_HEAD_DIM)),
        "od_w_in": nrm(ks[8], (N_ODD, D_MODEL, D_ODD_IN), D_MODEL ** -0.5),
        "od_w_out": nrm(ks[9], (N_ODD, D_ATTN, D_MODEL), D_ATTN ** -0.5),
        "od_norm_pre": gain(ks[10], (N_ODD, D_MODEL)),
        "od_norm_post": gain(ks[11], (N_ODD, D_MODEL)),
        "lambda_q1": nrm(ks[12], (N_ODD, DIFF_HEAD_DIM), 0.1),
        "lambda_k1": nrm(ks[13], (N_ODD, DIFF_HEAD_DIM), 0.1),
        "lambda_q2": nrm(ks[14], (N_ODD, DIFF_HEAD_DIM), 0.1),
        "lambda_k2": nrm(ks[15], (N_ODD, DIFF_HEAD_DIM), 0.1),
        "subln": gain(ks[16], (N_ODD, DIFF_V_DIM)),
    }


def reference(x_prompt, x_sample, ev_w_in, ev_w_out, ev_norm_pre, ev_norm_post, hgrn_lb_logits,
              hgrn_norm, od_w_in, od_w_out, od_norm_pre, od_norm_post,
              lambda_q1, lambda_k1, lambda_q2, lambda_k2, subln):
    y_prompt = trunk(x_prompt, ev_w_in, ev_w_out, ev_norm_pre, ev_norm_post, hgrn_lb_logits,
                     hgrn_norm, od_w_in, od_w_out, od_norm_pre, od_norm_post,
                     lambda_q1, lambda_k1, lambda_q2, lambda_k2, subln)
    y_sample = trunk(x_sample, ev_w_in, ev_w_out, ev_norm_pre, ev_norm_post, hgrn_lb_logits,
                     hgrn_norm, od_w_in, od_w_out, od_norm_pre, od_norm_post,
                     lambda_q1, lambda_k1, lambda_q2, lambda_k2, subln)
    return (y_prompt, y_sample)
```

```python
import functools
import math

import jax
import jax.numpy as jnp
from jax import lax
from jax.experimental import pallas as pl
from jax.experimental.pallas import tpu as pltpu

F32 = jnp.float32
BF16 = jnp.bfloat16

D_MODEL = 2048
D_FNET = 1024
FNET_GROUP_DIM = 256
FNET_GROUPS = D_FNET // FNET_GROUP_DIM
D_HGRN = 1024
HGRN_HEAD_DIM = 128
HGRN_HEADS = D_HGRN // HGRN_HEAD_DIM
HGRN_CHUNK = 64
DIFF_HEADS = 8
DIFF_HEAD_DIM = 128
DIFF_V_DIM = 256
D_ATTN = DIFF_HEADS * DIFF_V_DIM
RMS_EPS = 1e-6
LOG2E = 1.4426950408889634

FFT_L2 = 128
FFT_KB = 8
MIB = 1024 * 1024


def _params(semantics, vmem_mib):
    return pltpu.CompilerParams(dimension_semantics=semantics, vmem_limit_bytes=vmem_mib * MIB)


def _silu(x):
    return x * jax.nn.sigmoid(x)


def _norm_kernel(x_ref, g_ref, o_ref):
    x = x_ref[...]
    inv = lax.rsqrt(jnp.mean(x * x, axis=-1, keepdims=True) + RMS_EPS)
    o_ref[...] = (x * inv * g_ref[...]).astype(o_ref.dtype)


def _norm(x, g, tm=512):
    t, d = x.shape
    return pl.pallas_call(
        _norm_kernel,
        grid=(t // tm,),
        in_specs=[pl.BlockSpec((tm, d), lambda i: (i, 0)),
                  pl.BlockSpec((1, d), lambda i: (0, 0))],
        out_specs=pl.BlockSpec((tm, d), lambda i: (i, 0)),
        out_shape=jax.ShapeDtypeStruct((t, d), BF16),
        compiler_params=_params(("parallel",), 32),
        name="rmsnorm_bf16",
    )(x, g.reshape(1, d).astype(F32))


def _mm_kernel(h_ref, w_ref, o_ref):
    o_ref[...] = jnp.dot(h_ref[...], w_ref[...], preferred_element_type=F32).astype(o_ref.dtype)


def _matmul_cols(h, w, col_off, n_cols, tm=1024, tn=1024):
    t, k = h.shape
    tm = min(tm, t)
    off = col_off // tn
    return pl.pallas_call(
        _mm_kernel,
        grid=(n_cols // tn, t // tm),
        in_specs=[pl.BlockSpec((tm, k), lambda n, m: (m, 0)),
                  pl.BlockSpec((k, tn), lambda n, m: (0, n + off))],
        out_specs=pl.BlockSpec((tm, tn), lambda n, m: (m, n)),
        out_shape=jax.ShapeDtypeStruct((t, n_cols), BF16),
        compiler_params=_params(("parallel", "parallel"), 40),
        name="in_proj",
    )(h, w)


def _fft_tables(l):
    l1 = l // FFT_L2
    two_pi = 2.0 * math.pi
    k1 = jnp.arange(l1, dtype=jnp.int32)
    a1 = ((k1[:, None] * k1[None, :]) % l1).astype(F32) * (two_pi / l1)
    f1 = jnp.concatenate([jnp.cos(a1), -jnp.sin(a1)], axis=0).astype(BF16)
    k2 = jnp.arange(FFT_L2, dtype=jnp.int32)
    kk = k1[:, None, None] + l1 * k2[None, :, None]
    a2 = ((kk * k2[None, None, :]) % l).astype(F32) * (two_pi / l)
    c2, s2 = jnp.cos(a2), jnp.sin(a2)
    g2 = jnp.concatenate([jnp.concatenate([c2, s2], axis=2),
                          jnp.concatenate([-s2, c2], axis=2)], axis=1).astype(BF16)
    c = jnp.arange(FNET_GROUP_DIM, dtype=jnp.int32)
    a3 = ((c[:, None] * c[None, :]) % FNET_GROUP_DIM).astype(F32) * (two_pi / FNET_GROUP_DIM)
    scale = 1.0 / math.sqrt(l * FNET_GROUP_DIM)
    cs = (jnp.concatenate([jnp.cos(a3), jnp.sin(a3)], axis=0) * scale).astype(BF16)
    return f1, g2, cs


def _fft1_kernel(f_ref, u_ref, t_ref):
    l1 = u_ref.shape[1]
    r = jnp.dot(f_ref[...], u_ref[0], preferred_element_type=F32)
    t_ref[0, 0] = r[:l1].astype(t_ref.dtype)
    t_ref[1, 0] = r[l1:].astype(t_ref.dtype)


def _fft2_kernel(t_ref, g_ref, cs_ref, gate_ref, o_ref, p_scr):
    kb = g_ref.shape[0]
    l2 = FFT_L2
    gd = FNET_GROUP_DIM
    for j in range(kb):
        gm = g_ref[j]
        for g in range(FNET_GROUPS):
            cols = slice(g * gd, (g + 1) * gd)
            rhs = jnp.concatenate([t_ref[0, 0, j, :, cols], t_ref[1, 0, j, :, cols]], axis=0)
            p = jnp.dot(gm, rhs, preferred_element_type=F32)
            r0 = (j * FNET_GROUPS + g) * l2
            p_scr[r0:r0 + l2, 0:gd] = p[:l2].astype(p_scr.dtype)
            p_scr[r0:r0 + l2, gd:2 * gd] = p[l2:].astype(p_scr.dtype)
    y = jnp.dot(p_scr[...], cs_ref[...], preferred_element_type=F32)
    for j in range(kb):
        for g in range(FNET_GROUPS):
            r0 = (j * FNET_GROUPS + g) * l2
            cols = slice(j * D_FNET + g * gd, j * D_FNET + (g + 1) * gd)
            gate = gate_ref[0, :, cols].astype(F32)
            o_ref[0, :, cols] = (y[r0:r0 + l2] * _silu(gate)).astype(o_ref.dtype)


def _fnet_mix_gated(u, gate, b, l):
    l1, l2 = l // FFT_L2, FFT_L2
    f1, g2, cs = _fft_tables(l)
    wcols = l2 * D_FNET
    w = min(wcols, (2 * MIB) // (2 * l1))
    t = pl.pallas_call(
        _fft1_kernel,
        grid=(b, wcols // w),
        in_specs=[pl.BlockSpec((2 * l1, l1), lambda i, j: (0, 0)),
                  pl.BlockSpec((1, l1, w), lambda i, j: (i, 0, j))],
        out_specs=pl.BlockSpec((2, 1, l1, w), lambda i, j: (0, i, 0, j)),
        out_shape=jax.ShapeDtypeStruct((2, b, l1, wcols), BF16),
        compiler_params=_params(("parallel", "parallel"), 32),
        name="fnet_stage1",
    )(f1, u.reshape(b, l1, wcols))
    kb = min(FFT_KB, l1)
    y = pl.pallas_call(
        _fft2_kernel,
        grid=(b, l1 // kb),
        in_specs=[pl.BlockSpec((2, 1, kb, l2, D_FNET), lambda i, j: (0, i, j, 0, 0)),
                  pl.BlockSpec((kb, 2 * l2, 2 * l2), lambda i, j: (j, 0, 0)),
                  pl.BlockSpec((2 * FNET_GROUP_DIM, FNET_GROUP_DIM), lambda i, j: (0, 0)),
                  pl.BlockSpec((1, l2, kb * D_FNET), lambda i, j: (i, 0, j))],
        out_specs=pl.BlockSpec((1, l2, kb * D_FNET), lambda i, j: (i, 0, j)),
        out_shape=jax.ShapeDtypeStruct((b, l2, l1 * D_FNET), BF16),
        scratch_shapes=[pltpu.VMEM((kb * FNET_GROUPS * l2, 2 * FNET_GROUP_DIM), BF16)],
        compiler_params=_params(("parallel", "parallel"), 48),
        name="fnet_stage2",
    )(t.reshape(2, b, l1, l2, D_FNET), g2, cs, gate.reshape(b, l2, l1 * D_FNET))
    return y.reshape(b * l, D_FNET)


def _split3(x):
    hi = x.astype(BF16)
    r1 = x - hi.astype(F32)
    mid = r1.astype(BF16)
    lo = (r1 - mid.astype(F32)).astype(BF16)
    return jnp.concatenate([hi, mid, lo], axis=0)


def _hgrn_kernel(q_ref, i_ref, ff_ref, fb_ref, gate_ref, lbl_ref, gn_ref, o_ref, acc_ref):
    c = HGRN_CHUNK
    n_chunks = q_ref.shape[0] // c
    dk = HGRN_HEAD_DIM
    row = lax.broadcasted_iota(jnp.int32, (c, c), 0)
    col = lax.broadcasted_iota(jnp.int32, (c, c), 1)
    row3 = lax.broadcasted_iota(jnp.int32, (c, 3 * c), 0)
    col3 = lax.broadcasted_iota(jnp.int32, (c, 3 * c), 1) & (c - 1)

    logits = lbl_ref[...]
    mx = jnp.max(logits, axis=1, keepdims=True)
    ex = jnp.exp(logits - mx)
    lb_all = ex[:, 0, :] / jnp.sum(ex, axis=1)
    gn = gn_ref[...]

    def direction(f_ref, lb, forward):
        keep = (col <= row) if forward else (col >= row)
        tri3 = jnp.where((col3 <= row3) if forward else (col3 >= row3), 1.0, 0.0).astype(BF16)

        def body(step, state_t):
            ci = step if forward else n_chunks - 1 - step
            r0 = pl.multiple_of(ci * c, c)
            rows = pl.ds(r0, c)
            q = _silu(q_ref[rows, :].astype(F32))
            v = i_ref[rows, :]
            f = lb + (1.0 - lb) * jax.nn.sigmoid(f_ref[rows, :].astype(F32))
            k = 1.0 - f
            g = jnp.log(f)
            cum = jnp.dot(tri3, _split3(g), preferred_element_type=F32)
            last = cum[c - 1:c, :] if forward else cum[0:1, :]
            q_dec = (q * jnp.exp(cum)).astype(BF16)
            k_inv = k * jnp.exp(-cum)
            scores = lax.dot_general(q_dec, k_inv.astype(BF16), (((1,), (1,)), ((), ())),
                                     preferred_element_type=F32)
            scores = jnp.where(keep, scores, 0.0).astype(BF16)
            o = (jnp.dot(scores, v, preferred_element_type=F32)
                 + lax.dot_general(q_dec, state_t.astype(BF16), (((1,), (1,)), ((), ())),
                                   preferred_element_type=F32))
            decay = jnp.exp(last)
            k_end = (k_inv * decay).astype(BF16)
            new_state_t = state_t * decay + lax.dot_general(
                v, k_end, (((0,), (0,)), ((), ())), preferred_element_type=F32)
            if forward:
                acc_ref[rows, :] = o
            else:
                tot = acc_ref[rows, :] + o
                inv = lax.rsqrt(jnp.mean(tot * tot, axis=-1, keepdims=True) + RMS_EPS)
                gate = gate_ref[rows, :].astype(F32)
                o_ref[rows, :] = (tot * inv * gn * _silu(gate)).astype(o_ref.dtype)
            return new_state_t

        lax.fori_loop(0, n_chunks, body, jnp.zeros((dk, dk), F32))

    direction(ff_ref, lb_all[0:1, :], True)
    direction(fb_ref, lb_all[1:2, :], False)


def _hgrn_mix_gated(hg, lb_logits, g_norm, b, l):
    nh, dk = HGRN_HEADS, HGRN_HEAD_DIM

    def col(block):
        return pl.BlockSpec((l, dk), lambda i, h: (i, block * nh + h))

    return pl.pallas_call(
        _hgrn_kernel,
        grid=(b, nh),
        in_specs=[col(0), col(1), col(2), col(3), col(4),
                  pl.BlockSpec((2, lb_logits.shape[1], dk), lambda i, h: (0, 0, h)),
                  pl.BlockSpec((1, dk), lambda i, h: (0, 0))],
        out_specs=pl.BlockSpec((l, dk), lambda i, h: (i, h)),
        out_shape=jax.ShapeDtypeStruct((b * l, D_HGRN), BF16),
        scratch_shapes=[pltpu.VMEM((l, dk), F32)],
        compiler_params=_params(("parallel", "parallel"), 48),
        name="hgrn2",
    )(hg, hg, hg, hg, hg, lb_logits.astype(F32), g_norm.reshape(1, dk).astype(F32))


def _attn_kernel(slopes_ref, q_ref, k_ref, v_ref, gate_ref, lq1_ref, lk1_ref, lq2_ref, lk2_ref,
                 subln_ref, o_ref, m_ref, l_ref, acc_ref, *, tk, lambda_init):
    tq = q_ref.shape[0]
    seq = k_ref.shape[0]
    dh = DIFF_HEAD_DIM
    h = pl.program_id(1)
    qi = pl.program_id(2)
    slope2 = slopes_ref[h] * LOG2E
    scale2 = (dh ** -0.5) * LOG2E
    qpos = qi * tq + lax.broadcasted_iota(jnp.int32, (tq, 1), 0)
    kiota = lax.broadcasted_iota(jnp.int32, (1, tk), 1)

    m_ref[...] = jnp.full(m_ref.shape, -1e30, F32)
    l_ref[...] = jnp.zeros(l_ref.shape, F32)
    acc_ref[...] = jnp.zeros(acc_ref.shape, F32)

    def body(j, carry):
        k0 = pl.multiple_of(j * tk, tk)
        kblk = k_ref[pl.ds(k0, tk), :]
        vblk = v_ref[pl.ds(k0, tk), :]
        dist = jnp.abs(qpos - (k0 + kiota)).astype(F32)
        bias = dist * (-slope2)
        for c in range(2):
            s = lax.dot_general(q_ref[:, c * dh:(c + 1) * dh], kblk[:, c * dh:(c + 1) * dh],
                                (((1,), (1,)), ((), ())), preferred_element_type=F32)
            s = s * scale2 + bias
            m_old = m_ref[c]
            m_new = jnp.maximum(m_old, jnp.max(s, axis=-1, keepdims=True))
            alpha = jnp.exp2(m_old - m_new)
            p = jnp.exp2(s - m_new)
            l_ref[c] = alpha * l_ref[c] + jnp.sum(p, axis=-1, keepdims=True)
            acc_ref[c] = alpha * acc_ref[c] + jnp.dot(p.astype(BF16), vblk,
                                                      preferred_element_type=F32)
            m_ref[c] = m_new
        return carry

    lax.fori_loop(0, seq // tk, body, 0)

    lam = (jnp.exp(jnp.sum(lq1_ref[...] * lk1_ref[...], keepdims=True))
           - jnp.exp(jnp.sum(lq2_ref[...] * lk2_ref[...], keepdims=True)) + lambda_init)
    o = acc_ref[0] / l_ref[0] - lam * (acc_ref[1] / l_ref[1])
    inv = lax.rsqrt(jnp.mean(o * o, axis=-1, keepdims=True) + RMS_EPS)
    o = o * inv * subln_ref[...] * (1.0 - lambda_init)
    o_ref[...] = (o * _silu(gate_ref[...].astype(F32))).astype(o_ref.dtype)


def _diff_attention_gated(proj, lq1, lk1, lq2, lk2, subln, lambda_init, b, l, tq=256, tk=512):
    nh, dv = DIFF_HEADS, DIFF_V_DIM
    nq = l // tq
    slopes = jnp.exp2(-8.0 * (jnp.arange(nh, dtype=F32) + 1.0) / nh)
    vec = lambda a: a.reshape(1, -1).astype(F32)
    small = lambda n: pl.BlockSpec((1, n), lambda i, h, q, *_: (0, 0))
    grid_spec = pltpu.PrefetchScalarGridSpec(
        num_scalar_prefetch=1,
        grid=(b, nh, nq),
        in_specs=[pl.BlockSpec((tq, dv), lambda i, h, q, *_: (i * nq + q, h)),
                  pl.BlockSpec((l, dv), lambda i, h, q, *_: (i, nh + h)),
                  pl.BlockSpec((l, dv), lambda i, h, q, *_: (i, 2 * nh + h)),
                  pl.BlockSpec((tq, dv), lambda i, h, q, *_: (i * nq + q, 3 * nh + h)),
                  small(DIFF_HEAD_DIM), small(DIFF_HEAD_DIM), small(DIFF_HEAD_DIM),
                  small(DIFF_HEAD_DIM), small(dv)],
        out_specs=pl.BlockSpec((tq, dv), lambda i, h, q, *_: (i * nq + q, h)),
        scratch_shapes=[pltpu.VMEM((2, tq, 1), F32), pltpu.VMEM((2, tq, 1), F32),
                        pltpu.VMEM((2, tq, dv), F32)],
    )
    return pl.pallas_call(
        functools.partial(_attn_kernel, tk=tk, lambda_init=lambda_init),
        grid_spec=grid_spec,
        out_shape=jax.ShapeDtypeStruct((b * l, D_ATTN), BF16),
        compiler_params=_params(("parallel", "parallel", "parallel"), 48),
        name="diff_attention",
    )(slopes, proj, proj, proj, proj, vec(lq1), vec(lk1), vec(lq2), vec(lk2), vec(subln))


def _out_kernel(*refs, n_act):
    act_refs, w_refs = refs[:n_act], refs[n_act:2 * n_act]
    x_ref, g_ref, o_ref = refs[2 * n_act:]
    y = jnp.dot(act_refs[0][...], w_refs[0][...], preferred_element_type=F32)
    for a_ref, w_ref in zip(act_refs[1:], w_refs[1:]):
        y = y + jnp.dot(a_ref[...], w_ref[...], preferred_element_type=F32)
    inv = lax.rsqrt(jnp.mean(y * y, axis=-1, keepdims=True) + RMS_EPS)
    o_ref[...] = x_ref[...] + y * inv * g_ref[...]


def _out_proj_residual(acts, ws, x, g, tm=512):
    t, d = x.shape
    n_act = len(acts)
    in_specs = ([pl.BlockSpec((tm, a.shape[1]), lambda i: (i, 0)) for a in acts]
                + [pl.BlockSpec(w.shape, lambda i: (0, 0)) for w in ws]
                + [pl.BlockSpec((tm, d), lambda i: (i, 0)), pl.BlockSpec((1, d), lambda i: (0, 0))])
    return pl.pallas_call(
        functools.partial(_out_kernel, n_act=n_act),
        grid=(t // tm,),
        in_specs=in_specs,
        out_specs=pl.BlockSpec((tm, d), lambda i: (i, 0)),
        out_shape=jax.ShapeDtypeStruct((t, d), F32),
        compiler_params=_params(("parallel",), 48),
        name="out_proj_residual",
    )(*acts, *ws, x, g.reshape(1, d).astype(F32))


def _trunk(x3, wts):
    b, l, d = x3.shape
    x = x3.reshape(b * l, d)

    h = _norm(x, wts["ev_norm_pre"])
    w_in = wts["ev_w_in"]
    u = _matmul_cols(h, w_in, 0, D_FNET)
    gate_a = _matmul_cols(h, w_in, D_FNET, D_FNET)
    hg = _matmul_cols(h, w_in, 2 * D_FNET, 5 * D_HGRN)
    y_a = _fnet_mix_gated(u, gate_a, b, l)
    y_b = _hgrn_mix_gated(hg, wts["hgrn_lb_logits"], wts["hgrn_norm"], b, l)
    w_out = wts["ev_w_out"]
    x = _out_proj_residual([y_a, y_b], [w_out[:D_FNET], w_out[D_FNET:]], x, wts["ev_norm_post"])

    lambda_init = 0.8 - 0.6 * math.exp(-0.3 * 1)
    h = _norm(x, wts["od_norm_pre"])
    proj = _matmul_cols(h, wts["od_w_in"], 0, 4 * D_ATTN)
    o = _diff_attention_gated(proj, wts["lambda_q1"], wts["lambda_k1"], wts["lambda_q2"],
                              wts["lambda_k2"], wts["subln"], lambda_init, b, l)
    x = _out_proj_residual([o], [wts["od_w_out"]], x, wts["od_norm_post"])
    return x.reshape(b, l, d)


def kernel(x_prompt, x_sample, ev_w_in, ev_w_out, ev_norm_pre, ev_norm_post, hgrn_lb_logits,
           hgrn_norm, od_w_in, od_w_out, od_norm_pre, od_norm_post,
           lambda_q1, lambda_k1, lambda_q2, lambda_k2, subln):
    wts = {
        "ev_w_in": ev_w_in[0].astype(BF16), "ev_w_out": ev_w_out[0].astype(BF16),
        "ev_norm_pre": ev_norm_pre[0], "ev_norm_post": ev_norm_post[0],
        "hgrn_lb_logits": hgrn_lb_logits, "hgrn_norm": hgrn_norm[0],
        "od_w_in": od_w_in[0].astype(BF16), "od_w_out": od_w_out[0].astype(BF16),
        "od_norm_pre": od_norm_pre[0], "od_norm_post": od_norm_post[0],
        "lambda_q1": lambda_q1[0], "lambda_k1": lambda_k1[0],
        "lambda_q2": lambda_q2[0], "lambda_k2": lambda_k2[0], "subln": subln[0],
    }
    return (_trunk(x_prompt, wts), _trunk(x_sample, wts))
```

```python
import functools
import math

import jax
import jax.numpy as jnp
from jax import lax
from jax.experimental import pallas as pl
from jax.experimental.pallas import tpu as pltpu

F32 = jnp.float32
BF16 = jnp.bfloat16

D_MODEL = 2048
D_FNET = 1024
FNET_GROUP_DIM = 256
FNET_GROUPS = D_FNET // FNET_GROUP_DIM
D_HGRN = 1024
HGRN_HEAD_DIM = 128
HGRN_HEADS = D_HGRN // HGRN_HEAD_DIM
DIFF_HEADS = 8
DIFF_HEAD_DIM = 128
DIFF_V_DIM = 256
D_ATTN = DIFF_HEADS * DIFF_V_DIM
RMS_EPS = 1e-6
LOG2E = 1.4426950408889634

SUBLANES = 8
FFT_L2 = 128
FFT_KB = 8
HGRN_BLOCK = 128
HGRN_GROUP = 8
MIB = 1024 * 1024


def _params(semantics, vmem_mib):
    return pltpu.CompilerParams(dimension_semantics=semantics, vmem_limit_bytes=vmem_mib * MIB)


def _silu(x):
    return x * jax.nn.sigmoid(x)


def _dot_nt(a, b):
    return lax.dot_general(a, b, (((1,), (1,)), ((), ())), preferred_element_type=F32)


def _norm_kernel(x_ref, g_ref, o_ref):
    x = x_ref[...]
    inv = lax.rsqrt(jnp.mean(x * x, axis=-1, keepdims=True) + RMS_EPS)
    o_ref[...] = (x * inv * g_ref[...]).astype(o_ref.dtype)


def _norm(x, g, tm=512):
    t, d = x.shape
    return pl.pallas_call(
        _norm_kernel,
        grid=(t // tm,),
        in_specs=[pl.BlockSpec((tm, d), lambda i: (i, 0)),
                  pl.BlockSpec((1, d), lambda i: (0, 0))],
        out_specs=pl.BlockSpec((tm, d), lambda i: (i, 0)),
        out_shape=jax.ShapeDtypeStruct((t, d), BF16),
        compiler_params=_params(("parallel",), 32),
        name="rmsnorm_bf16",
    )(x, g.reshape(1, d).astype(F32))


def _mm_kernel(h_ref, w_ref, o_ref):
    o_ref[...] = jnp.dot(h_ref[...], w_ref[...], preferred_element_type=F32).astype(o_ref.dtype)


def _mm_t_kernel(h_ref, w_ref, o_ref, r_ref):
    r_ref[...] = jnp.dot(h_ref[...], w_ref[...], preferred_element_type=F32)
    o_ref[...] = r_ref[...].T.astype(o_ref.dtype)


def _matmul_cols(h, w, col_off, n_cols, transpose_out=False, tm=1024, tn=1024):
    t, k = h.shape
    tm = min(tm, t)
    off = col_off // tn
    if transpose_out:
        body, out_shape, scratch = _mm_t_kernel, (n_cols, t), [pltpu.VMEM((tm, tn), F32)]
        out_spec = pl.BlockSpec((tn, tm), lambda n, m: (n, m))
    else:
        body, out_shape, scratch = _mm_kernel, (t, n_cols), []
        out_spec = pl.BlockSpec((tm, tn), lambda n, m: (m, n))
    return pl.pallas_call(
        body,
        grid=(n_cols // tn, t // tm),
        in_specs=[pl.BlockSpec((tm, k), lambda n, m: (m, 0)),
                  pl.BlockSpec((k, tn), lambda n, m: (0, n + off))],
        out_specs=out_spec,
        out_shape=jax.ShapeDtypeStruct(out_shape, BF16),
        scratch_shapes=scratch,
        compiler_params=_params(("parallel", "parallel"), 40),
        name="in_proj_t" if transpose_out else "in_proj",
    )(h, w)


def _fft_tables(l):
    l1 = l // FFT_L2
    two_pi = 2.0 * math.pi
    k1 = jnp.arange(l1, dtype=jnp.int32)
    a1 = ((k1[:, None] * k1[None, :]) % l1).astype(F32) * (two_pi / l1)
    f1 = jnp.concatenate([jnp.cos(a1), -jnp.sin(a1)], axis=0).astype(BF16)
    k2 = jnp.arange(FFT_L2, dtype=jnp.int32)
    kk = k1[:, None, None] + l1 * k2[None, :, None]
    a2 = ((kk * k2[None, None, :]) % l).astype(F32) * (two_pi / l)
    c2, s2 = jnp.cos(a2), jnp.sin(a2)
    g2 = jnp.concatenate([jnp.concatenate([c2, s2], axis=2),
                          jnp.concatenate([-s2, c2], axis=2)], axis=1).astype(BF16)
    c = jnp.arange(FNET_GROUP_DIM, dtype=jnp.int32)
    a3 = ((c[:, None] * c[None, :]) % FNET_GROUP_DIM).astype(F32) * (two_pi / FNET_GROUP_DIM)
    scale = 1.0 / math.sqrt(l * FNET_GROUP_DIM)
    cs = (jnp.concatenate([jnp.cos(a3), jnp.sin(a3)], axis=0) * scale).astype(BF16)
    return f1, g2, cs


def _fft1_kernel(f_ref, u_ref, t_ref):
    l1 = u_ref.shape[1]
    r = jnp.dot(f_ref[...], u_ref[0], preferred_element_type=F32)
    t_ref[0, 0] = r[:l1].astype(t_ref.dtype)
    t_ref[1, 0] = r[l1:].astype(t_ref.dtype)


def _fft2_kernel(t_ref, g_ref, cs_ref, gate_ref, o_ref, p_scr):
    kb = g_ref.shape[0]
    l2 = FFT_L2
    gd = FNET_GROUP_DIM
    for j in range(kb):
        gm = g_ref[j]
        for g in range(FNET_GROUPS):
            cols = slice(g * gd, (g + 1) * gd)
            rhs = jnp.concatenate([t_ref[0, 0, j, :, cols], t_ref[1, 0, j, :, cols]], axis=0)
            p = jnp.dot(gm, rhs, preferred_element_type=F32)
            r0 = (j * FNET_GROUPS + g) * l2
            p_scr[r0:r0 + l2, 0:gd] = p[:l2].astype(p_scr.dtype)
            p_scr[r0:r0 + l2, gd:2 * gd] = p[l2:].astype(p_scr.dtype)
    y = jnp.dot(p_scr[...], cs_ref[...], preferred_element_type=F32)
    for j in range(kb):
        for g in range(FNET_GROUPS):
            r0 = (j * FNET_GROUPS + g) * l2
            cols = slice(j * D_FNET + g * gd, j * D_FNET + (g + 1) * gd)
            gate = gate_ref[0, :, cols].astype(F32)
            o_ref[0, :, cols] = (y[r0:r0 + l2] * _silu(gate)).astype(o_ref.dtype)


def _fnet_mix_gated(u, gate, b, l):
    l1, l2 = l // FFT_L2, FFT_L2
    f1, g2, cs = _fft_tables(l)
    wcols = l2 * D_FNET
    w = min(wcols, (2 * MIB) // (2 * l1))
    t = pl.pallas_call(
        _fft1_kernel,
        grid=(b, wcols // w),
        in_specs=[pl.BlockSpec((2 * l1, l1), lambda i, j: (0, 0)),
                  pl.BlockSpec((1, l1, w), lambda i, j: (i, 0, j))],
        out_specs=pl.BlockSpec((2, 1, l1, w), lambda i, j: (0, i, 0, j)),
        out_shape=jax.ShapeDtypeStruct((2, b, l1, wcols), BF16),
        compiler_params=_params(("parallel", "parallel"), 32),
        name="fnet_stage1",
    )(f1, u.reshape(b, l1, wcols))
    kb = min(FFT_KB, l1)
    y = pl.pallas_call(
        _fft2_kernel,
        grid=(b, l1 // kb),
        in_specs=[pl.BlockSpec((2, 1, kb, l2, D_FNET), lambda i, j: (0, i, j, 0, 0)),
                  pl.BlockSpec((kb, 2 * l2, 2 * l2), lambda i, j: (j, 0, 0)),
                  pl.BlockSpec((2 * FNET_GROUP_DIM, FNET_GROUP_DIM), lambda i, j: (0, 0)),
                  pl.BlockSpec((1, l2, kb * D_FNET), lambda i, j: (i, 0, j))],
        out_specs=pl.BlockSpec((1, l2, kb * D_FNET), lambda i, j: (i, 0, j)),
        out_shape=jax.ShapeDtypeStruct((b, l2, l1 * D_FNET), BF16),
        scratch_shapes=[pltpu.VMEM((kb * FNET_GROUPS * l2, 2 * FNET_GROUP_DIM), BF16)],
        compiler_params=_params(("parallel", "parallel"), 48),
        name="fnet_stage2",
    )(t.reshape(2, b, l1, l2, D_FNET), g2, cs, gate.reshape(b, l2, l1 * D_FNET))
    return y.reshape(b * l, D_FNET)


def _row_scan(g, forward):
    n = g.shape[0] // SUBLANES
    sub = lax.broadcasted_iota(jnp.int32, (SUBLANES, g.shape[1]), 0)
    outs = [None] * n
    carry = None
    for i in (range(n) if forward else range(n - 1, -1, -1)):
        y = g[i * SUBLANES:(i + 1) * SUBLANES, :]
        for sh in (1, 2, 4):
            if forward:
                y = y + jnp.where(sub >= sh, pltpu.roll(y, sh, 0), 0.0)
            else:
                y = y + jnp.where(sub < SUBLANES - sh, pltpu.roll(y, SUBLANES - sh, 0), 0.0)
        if carry is not None:
            y = y + carry
        edge = y[SUBLANES - 1:SUBLANES, :] if forward else y[0:1, :]
        carry = jnp.broadcast_to(edge, y.shape)
        outs[i] = y
    return jnp.concatenate(outs, axis=0)


def _hgrn_kernel(q_ref, i_ref, ff_ref, fb_ref, gate_ref, lbl_ref, gn_ref, o_ref, acc_ref):
    hb = HGRN_BLOCK
    half = hb // 2
    n_blocks = q_ref.shape[0] // hb
    group = min(HGRN_GROUP, n_blocks)
    dk = HGRN_HEAD_DIM
    row = lax.broadcasted_iota(jnp.int32, (hb, hb), 0)
    col = lax.broadcasted_iota(jnp.int32, (hb, hb), 1)

    logits = lbl_ref[...]
    mx = jnp.max(logits, axis=1, keepdims=True)
    ex = jnp.exp(logits - mx)
    lb_all = ex[:, 0, :] / jnp.sum(ex, axis=1)
    gn = gn_ref[...]

    def direction(f_ref, lb, forward):
        keep = (col <= row) if forward else (col >= row)

        def body(step, state_t):
            gi = step if forward else n_blocks // group - 1 - step
            base = gi * (group * hb)
            units = []
            for u in range(group):
                rows = pl.ds(pl.multiple_of(base + u * hb, hb), hb)
                q = _silu(q_ref[rows, :].astype(F32))
                v = i_ref[rows, :]
                f = lb + (1.0 - lb) * jax.nn.sigmoid(f_ref[rows, :].astype(F32))
                k = 1.0 - f
                a = _row_scan(jnp.log(f), forward)
                ref = a[half - 1:half, :] if forward else a[half:half + 1, :]
                end = a[hb - 1:hb, :] if forward else a[0:1, :]
                qt = q * jnp.exp(a - ref)
                kt = k * jnp.exp(ref - a)
                scores = jnp.where(keep, _dot_nt(qt.astype(BF16), kt.astype(BF16)), 0.0)
                o_intra = jnp.dot(scores.astype(BF16), v, preferred_element_type=F32)
                q_in = (qt * jnp.exp(ref)).astype(BF16)
                k_end = (kt * jnp.exp(end - ref)).astype(BF16)
                kv_t = lax.dot_general(v, k_end, (((0,), (0,)), ((), ())),
                                       preferred_element_type=F32)
                units.append((rows, o_intra, q_in, kv_t, jnp.exp(end)))
            for rows, o_intra, q_in, kv_t, decay in (units if forward else units[::-1]):
                o = o_intra + _dot_nt(q_in, state_t.astype(BF16))
                state_t = state_t * decay + kv_t
                if forward:
                    acc_ref[rows, :] = o
                else:
                    tot = acc_ref[rows, :] + o
                    inv = lax.rsqrt(jnp.mean(tot * tot, axis=-1, keepdims=True) + RMS_EPS)
                    gate = gate_ref[rows, :].astype(F32)
                    o_ref[rows, :] = (tot * inv * gn * _silu(gate)).astype(o_ref.dtype)
            return state_t

        lax.fori_loop(0, n_blocks // group, body, jnp.zeros((dk, dk), F32))

    direction(ff_ref, lb_all[0:1, :], True)
    direction(fb_ref, lb_all[1:2, :], False)


def _hgrn_mix_gated(hg, lb_logits, g_norm, b, l):
    nh, dk = HGRN_HEADS, HGRN_HEAD_DIM

    def col(block):
        return pl.BlockSpec((l, dk), lambda i, h: (i, block * nh + h))

    return pl.pallas_call(
        _hgrn_kernel,
        grid=(b, nh),
        in_specs=[col(0), col(1), col(2), col(3), col(4),
                  pl.BlockSpec((2, lb_logits.shape[1], dk), lambda i, h: (0, 0, h)),
                  pl.BlockSpec((1, dk), lambda i, h: (0, 0))],
        out_specs=pl.BlockSpec((l, dk), lambda i, h: (i, h)),
        out_shape=jax.ShapeDtypeStruct((b * l, D_HGRN), BF16),
        scratch_shapes=[pltpu.VMEM((l, dk), F32)],
        compiler_params=_params(("parallel", "parallel"), 48),
        name="hgrn2",
    )(hg, hg, hg, hg, hg, lb_logits.astype(F32), g_norm.reshape(1, dk).astype(F32))


def _attn_kernel(slopes_ref, q_ref, k_ref, vt_ref, gate_ref, lq1_ref, lk1_ref, lq2_ref, lk2_ref,
                 subln_ref, o_ref, qbd_ref, d0_ref, ta_ref, tb_ref, m_ref, l_ref, acc_ref,
                 *, tk, lambda_init):
    tq = q_ref.shape[0]
    n_blocks = k_ref.shape[0] // tk
    dh = DIFF_HEAD_DIM
    h = pl.program_id(1)
    qi = pl.program_id(2)
    slope2 = slopes_ref[h] * LOG2E
    scale2 = (dh ** -0.5) * LOG2E
    kk = lax.broadcasted_iota(jnp.int32, (tk, tq), 0)
    qq = lax.broadcasted_iota(jnp.int32, (tk, tq), 1)
    d0_ref[...] = (kk - qq).astype(F32) * slope2

    zeros = jnp.zeros((tq, dh), BF16)
    qbd_ref[0:tq, 0:dh] = q_ref[:, 0:dh]
    qbd_ref[0:tq, dh:2 * dh] = zeros
    qbd_ref[tq:2 * tq, 0:dh] = zeros
    qbd_ref[tq:2 * tq, dh:2 * dh] = q_ref[:, dh:2 * dh]

    m_ref[...] = jnp.full(m_ref.shape, -1e30, F32)
    l_ref[...] = jnp.zeros(l_ref.shape, F32)
    acc_ref[...] = jnp.zeros(acc_ref.shape, F32)

    def scores(j, t_ref):
        k0 = pl.multiple_of(j * tk, tk)
        off = (k0 - qi * tq).astype(F32) * slope2
        dist = jnp.abs(d0_ref[...] + off)
        s = _dot_nt(k_ref[pl.ds(k0, tk), :], qbd_ref[...]) * scale2
        t_ref[:, 0:tq] = s[:, 0:tq] - dist
        t_ref[:, tq:2 * tq] = s[:, tq:2 * tq] - dist

    def accumulate(j, t_ref):
        k0 = pl.multiple_of(j * tk, tk)
        t = t_ref[...]
        m_old = m_ref[...]
        m_new = jnp.maximum(m_old, jnp.max(t, axis=0, keepdims=True))
        alpha = jnp.exp2(m_old - m_new)
        p = jnp.exp2(t - m_new)
        l_ref[...] = alpha * l_ref[...] + jnp.sum(p, axis=0, keepdims=True)
        acc_ref[...] = alpha * acc_ref[...] + jnp.dot(vt_ref[:, pl.ds(k0, tk)], p.astype(BF16),
                                                      preferred_element_type=F32)
        m_ref[...] = m_new

    assert n_blocks % 2 == 0
    scores(0, ta_ref)

    def body(i, carry):
        scores(2 * i + 1, tb_ref)
        accumulate(2 * i, ta_ref)
        scores(2 * i + 2, ta_ref)
        accumulate(2 * i + 1, tb_ref)
        return carry

    lax.fori_loop(0, n_blocks // 2 - 1, body, 0)
    scores(n_blocks - 1, tb_ref)
    accumulate(n_blocks - 2, ta_ref)
    accumulate(n_blocks - 1, tb_ref)

    lam = (jnp.exp(jnp.sum(lq1_ref[...] * lk1_ref[...], keepdims=True))
           - jnp.exp(jnp.sum(lq2_ref[...] * lk2_ref[...], keepdims=True)) + lambda_init)
    o_both = acc_ref[...] / l_ref[...]
    o = (o_both[:, 0:tq] - lam * o_both[:, tq:2 * tq]).T
    inv = lax.rsqrt(jnp.mean(o * o, axis=-1, keepdims=True) + RMS_EPS)
    o = o * inv * subln_ref[...] * (1.0 - lambda_init)
    o_ref[...] = (o * _silu(gate_ref[...].astype(F32))).astype(o_ref.dtype)


def _diff_attention_gated(qk, v_t, gate, lq1, lk1, lq2, lk2, subln, lambda_init, b, l,
                          tq=256, tk=512):
    nh, dv = DIFF_HEADS, DIFF_V_DIM
    nq = l // tq
    slopes = jnp.exp2(-8.0 * (jnp.arange(nh, dtype=F32) + 1.0) / nh)
    vec = lambda a: a.reshape(1, -1).astype(F32)
    small = lambda n: pl.BlockSpec((1, n), lambda i, h, q, *_: (0, 0))
    grid_spec = pltpu.PrefetchScalarGridSpec(
        num_scalar_prefetch=1,
        grid=(b, nh, nq),
        in_specs=[pl.BlockSpec((tq, dv), lambda i, h, q, *_: (i * nq + q, h)),
                  pl.BlockSpec((l, dv), lambda i, h, q, *_: (i, nh + h)),
                  pl.BlockSpec((dv, l), lambda i, h, q, *_: (h, i)),
                  pl.BlockSpec((tq, dv), lambda i, h, q, *_: (i * nq + q, h)),
                  small(DIFF_HEAD_DIM), small(DIFF_HEAD_DIM), small(DIFF_HEAD_DIM),
                  small(DIFF_HEAD_DIM), small(dv)],
        out_specs=pl.BlockSpec((tq, dv), lambda i, h, q, *_: (i * nq + q, h)),
        scratch_shapes=[pltpu.VMEM((2 * tq, 2 * DIFF_HEAD_DIM), BF16),
                        pltpu.VMEM((tk, tq), F32),
                        pltpu.VMEM((tk, 2 * tq), F32),
                        pltpu.VMEM((tk, 2 * tq), F32),
                        pltpu.VMEM((1, 2 * tq), F32), pltpu.VMEM((1, 2 * tq), F32),
                        pltpu.VMEM((dv, 2 * tq), F32)],
    )
    return pl.pallas_call(
        functools.partial(_attn_kernel, tk=tk, lambda_init=lambda_init),
        grid_spec=grid_spec,
        out_shape=jax.ShapeDtypeStruct((b * l, D_ATTN), BF16),
        compiler_params=_params(("parallel", "parallel", "parallel"), 48),
        name="diff_attention",
    )(slopes, qk, qk, v_t, gate, vec(lq1), vec(lk1), vec(lq2), vec(lk2), vec(subln))


def _out_kernel(*refs, n_act):
    act_refs, w_refs = refs[:n_act], refs[n_act:2 * n_act]
    x_ref, g_ref, o_ref = refs[2 * n_act:]
    y = jnp.dot(act_refs[0][...], w_refs[0][...], preferred_element_type=F32)
    for a_ref, w_ref in zip(act_refs[1:], w_refs[1:]):
        y = y + jnp.dot(a_ref[...], w_ref[...], preferred_element_type=F32)
    inv = lax.rsqrt(jnp.mean(y * y, axis=-1, keepdims=True) + RMS_EPS)
    o_ref[...] = x_ref[...] + y * inv * g_ref[...]


def _out_proj_residual(acts, ws, x, g, tm=512):
    t, d = x.shape
    n_act = len(acts)
    in_specs = ([pl.BlockSpec((tm, a.shape[1]), lambda i: (i, 0)) for a in acts]
                + [pl.BlockSpec(w.shape, lambda i: (0, 0)) for w in ws]
                + [pl.BlockSpec((tm, d), lambda i: (i, 0)), pl.BlockSpec((1, d), lambda i: (0, 0))])
    return pl.pallas_call(
        functools.partial(_out_kernel, n_act=n_act),
        grid=(t // tm,),
        in_specs=in_specs,
        out_specs=pl.BlockSpec((tm, d), lambda i: (i, 0)),
        out_shape=jax.ShapeDtypeStruct((t, d), F32),
        compiler_params=_params(("parallel",), 48),
        name="out_proj_residual",
    )(*acts, *ws, x, g.reshape(1, d).astype(F32))


def _trunk(x3, wts):
    b, l, d = x3.shape
    x = x3.reshape(b * l, d)

    h = _norm(x, wts["ev_norm_pre"])
    w_in = wts["ev_w_in"]
    u = _matmul_cols(h, w_in, 0, D_FNET)
    gate_a = _matmul_cols(h, w_in, D_FNET, D_FNET)
    hg = _matmul_cols(h, w_in, 2 * D_FNET, 5 * D_HGRN)
    y_a = _fnet_mix_gated(u, gate_a, b, l)
    y_b = _hgrn_mix_gated(hg, wts["hgrn_lb_logits"], wts["hgrn_norm"], b, l)
    w_out = wts["ev_w_out"]
    x = _out_proj_residual([y_a, y_b], [w_out[:D_FNET], w_out[D_FNET:]], x, wts["ev_norm_post"])

    lambda_init = 0.8 - 0.6 * math.exp(-0.3 * 1)
    h = _norm(x, wts["od_norm_pre"])
    w_in = wts["od_w_in"]
    qk = _matmul_cols(h, w_in, 0, 2 * D_ATTN)
    v_t = _matmul_cols(h, w_in, 2 * D_ATTN, D_ATTN, transpose_out=True)
    gate = _matmul_cols(h, w_in, 3 * D_ATTN, D_ATTN)
    o = _diff_attention_gated(qk, v_t, gate, wts["lambda_q1"], wts["lambda_k1"], wts["lambda_q2"],
                              wts["lambda_k2"], wts["subln"], lambda_init, b, l)
    x = _out_proj_residual([o], [wts["od_w_out"]], x, wts["od_norm_post"])
    return x.reshape(b, l, d)


def kernel(x_prompt, x_sample, ev_w_in, ev_w_out, ev_norm_pre, ev_norm_post, hgrn_lb_logits,
           hgrn_norm, od_w_in, od_w_out, od_norm_pre, od_norm_post,
           lambda_q1, lambda_k1, lambda_q2, lambda_k2, subln):
    wts = {
        "ev_w_in": ev_w_in[0].astype(BF16), "ev_w_out": ev_w_out[0].astype(BF16),
        "ev_norm_pre": ev_norm_pre[0], "ev_norm_post": ev_norm_post[0],
        "hgrn_lb_logits": hgrn_lb_logits, "hgrn_norm": hgrn_norm[0],
        "od_w_in": od_w_in[0].astype(BF16), "od_w_out": od_w_out[0].astype(BF16),
        "od_norm_pre": od_norm_pre[0], "od_norm_post": od_norm_post[0],
        "lambda_q1": lambda_q1[0], "lambda_k1": lambda_k1[0],
        "lambda_q2": lambda_q2[0], "lambda_k2": lambda_k2[0], "subln": subln[0],
    }
    return (_trunk(x_prompt, wts), _trunk(x_sample, wts))
```

```python
import functools
import math

import jax
import jax.numpy as jnp
from jax import lax
from jax.experimental import pallas as pl
from jax.experimental.pallas import tpu as pltpu

F32 = jnp.float32
BF16 = jnp.bfloat16

D_MODEL = 2048
D_FNET = 1024
FNET_GROUP_DIM = 256
FNET_GROUPS = D_FNET // FNET_GROUP_DIM
D_HGRN = 1024
HGRN_HEAD_DIM = 128
HGRN_HEADS = D_HGRN // HGRN_HEAD_DIM
DIFF_HEADS = 8
DIFF_HEAD_DIM = 128
DIFF_V_DIM = 256
D_ATTN = DIFF_HEADS * DIFF_V_DIM
RMS_EPS = 1e-6
LOG2E = 1.4426950408889634

SUBLANES = 8
FFT_L2 = 128
FFT_KB = 8
HGRN_BLOCK = 128
HGRN_GROUP = 8
MIB = 1024 * 1024


def _params(semantics, vmem_mib):
    return pltpu.CompilerParams(dimension_semantics=semantics, vmem_limit_bytes=vmem_mib * MIB)


def _silu(x):
    return x * jax.nn.sigmoid(x)


def _dot_nt(a, b):
    return lax.dot_general(a, b, (((1,), (1,)), ((), ())), preferred_element_type=F32)


def _norm_kernel(x_ref, g_ref, o_ref):
    x = x_ref[...]
    inv = lax.rsqrt(jnp.mean(x * x, axis=-1, keepdims=True) + RMS_EPS)
    o_ref[...] = (x * inv * g_ref[...]).astype(o_ref.dtype)


def _norm(x, g, tm=512):
    t, d = x.shape
    return pl.pallas_call(
        _norm_kernel,
        grid=(t // tm,),
        in_specs=[pl.BlockSpec((tm, d), lambda i: (i, 0)),
                  pl.BlockSpec((1, d), lambda i: (0, 0))],
        out_specs=pl.BlockSpec((tm, d), lambda i: (i, 0)),
        out_shape=jax.ShapeDtypeStruct((t, d), BF16),
        compiler_params=_params(("parallel",), 32),
        name="rmsnorm_bf16",
    )(x, g.reshape(1, d).astype(F32))


def _mm_kernel(h_ref, w_ref, o_ref, *, out_scale):
    r = jnp.dot(h_ref[...], w_ref[...], preferred_element_type=F32)
    if out_scale != 1.0:
        r = r * out_scale
    o_ref[...] = r.astype(o_ref.dtype)


def _mm_t_kernel(h_ref, w_ref, o_ref, r_ref):
    r_ref[...] = jnp.dot(h_ref[...], w_ref[...], preferred_element_type=F32)
    o_ref[...] = r_ref[...].T.astype(o_ref.dtype)


def _matmul_cols(h, w, col_off, n_cols, transpose_out=False, out_scale=1.0, tm=1024, tn=1024):
    t, k = h.shape
    tm = min(tm, t)
    off = col_off // tn
    if transpose_out:
        assert out_scale == 1.0
        body, out_shape, scratch = _mm_t_kernel, (n_cols, t), [pltpu.VMEM((tm, tn), F32)]
        out_spec = pl.BlockSpec((tn, tm), lambda n, m: (n, m))
    else:
        body = functools.partial(_mm_kernel, out_scale=out_scale)
        out_shape, scratch = (t, n_cols), []
        out_spec = pl.BlockSpec((tm, tn), lambda n, m: (m, n))
    return pl.pallas_call(
        body,
        grid=(n_cols // tn, t // tm),
        in_specs=[pl.BlockSpec((tm, k), lambda n, m: (m, 0)),
                  pl.BlockSpec((k, tn), lambda n, m: (0, n + off))],
        out_specs=out_spec,
        out_shape=jax.ShapeDtypeStruct(out_shape, BF16),
        scratch_shapes=scratch,
        compiler_params=_params(("parallel", "parallel"), 40),
        name="in_proj_t" if transpose_out else "in_proj",
    )(h, w)


def _fft_tables(l):
    l1 = l // FFT_L2
    two_pi = 2.0 * math.pi
    k1 = jnp.arange(l1, dtype=jnp.int32)
    a1 = ((k1[:, None] * k1[None, :]) % l1).astype(F32) * (two_pi / l1)
    f1 = jnp.concatenate([jnp.cos(a1), -jnp.sin(a1)], axis=0).astype(BF16)
    k2 = jnp.arange(FFT_L2, dtype=jnp.int32)
    kk = k1[:, None, None] + l1 * k2[None, :, None]
    a2 = ((kk * k2[None, None, :]) % l).astype(F32) * (two_pi / l)
    c2, s2 = jnp.cos(a2), jnp.sin(a2)
    g2 = jnp.concatenate([jnp.concatenate([c2, s2], axis=2),
                          jnp.concatenate([-s2, c2], axis=2)], axis=1).astype(BF16)
    c = jnp.arange(FNET_GROUP_DIM, dtype=jnp.int32)
    a3 = ((c[:, None] * c[None, :]) % FNET_GROUP_DIM).astype(F32) * (two_pi / FNET_GROUP_DIM)
    scale = 1.0 / math.sqrt(l * FNET_GROUP_DIM)
    cs = (jnp.concatenate([jnp.cos(a3), jnp.sin(a3)], axis=0) * scale).astype(BF16)
    return f1, g2, cs


def _fft1_kernel(f_ref, u_ref, t_ref):
    l1 = u_ref.shape[1]
    r = jnp.dot(f_ref[...], u_ref[0], preferred_element_type=F32)
    t_ref[0, 0] = r[:l1].astype(t_ref.dtype)
    t_ref[1, 0] = r[l1:].astype(t_ref.dtype)


def _fft2_kernel(t_ref, g_ref, cs_ref, gate_ref, o_ref, p_scr):
    kb = g_ref.shape[0]
    l2 = FFT_L2
    gd = FNET_GROUP_DIM
    for j in range(kb):
        gm = g_ref[j]
        for g in range(FNET_GROUPS):
            cols = slice(g * gd, (g + 1) * gd)
            rhs = jnp.concatenate([t_ref[0, 0, j, :, cols], t_ref[1, 0, j, :, cols]], axis=0)
            p = jnp.dot(gm, rhs, preferred_element_type=F32)
            r0 = (j * FNET_GROUPS + g) * l2
            p_scr[r0:r0 + l2, 0:gd] = p[:l2].astype(p_scr.dtype)
            p_scr[r0:r0 + l2, gd:2 * gd] = p[l2:].astype(p_scr.dtype)
    y = jnp.dot(p_scr[...], cs_ref[...], preferred_element_type=F32)
    for j in range(kb):
        for g in range(FNET_GROUPS):
            r0 = (j * FNET_GROUPS + g) * l2
            cols = slice(j * D_FNET + g * gd, j * D_FNET + (g + 1) * gd)
            gate = gate_ref[0, :, cols].astype(F32)
            o_ref[0, :, cols] = (y[r0:r0 + l2] * _silu(gate)).astype(o_ref.dtype)


def _fnet_mix_gated(u, gate, b, l):
    l1, l2 = l // FFT_L2, FFT_L2
    f1, g2, cs = _fft_tables(l)
    wcols = l2 * D_FNET
    w = min(wcols, (2 * MIB) // (2 * l1))
    t = pl.pallas_call(
        _fft1_kernel,
        grid=(b, wcols // w),
        in_specs=[pl.BlockSpec((2 * l1, l1), lambda i, j: (0, 0)),
                  pl.BlockSpec((1, l1, w), lambda i, j: (i, 0, j))],
        out_specs=pl.BlockSpec((2, 1, l1, w), lambda i, j: (0, i, 0, j)),
        out_shape=jax.ShapeDtypeStruct((2, b, l1, wcols), BF16),
        compiler_params=_params(("parallel", "parallel"), 32),
        name="fnet_stage1",
    )(f1, u.reshape(b, l1, wcols))
    kb = min(FFT_KB, l1)
    y = pl.pallas_call(
        _fft2_kernel,
        grid=(b, l1 // kb),
        in_specs=[pl.BlockSpec((2, 1, kb, l2, D_FNET), lambda i, j: (0, i, j, 0, 0)),
                  pl.BlockSpec((kb, 2 * l2, 2 * l2), lambda i, j: (j, 0, 0)),
                  pl.BlockSpec((2 * FNET_GROUP_DIM, FNET_GROUP_DIM), lambda i, j: (0, 0)),
                  pl.BlockSpec((1, l2, kb * D_FNET), lambda i, j: (i, 0, j))],
        out_specs=pl.BlockSpec((1, l2, kb * D_FNET), lambda i, j: (i, 0, j)),
        out_shape=jax.ShapeDtypeStruct((b, l2, l1 * D_FNET), BF16),
        scratch_shapes=[pltpu.VMEM((kb * FNET_GROUPS * l2, 2 * FNET_GROUP_DIM), BF16)],
        compiler_params=_params(("parallel", "parallel"), 48),
        name="fnet_stage2",
    )(t.reshape(2, b, l1, l2, D_FNET), g2, cs, gate.reshape(b, l2, l1 * D_FNET))
    return y.reshape(b * l, D_FNET)


def _row_scan(g, forward):
    n = g.shape[0] // SUBLANES
    sub = lax.broadcasted_iota(jnp.int32, (SUBLANES, g.shape[1]), 0)
    outs = [None] * n
    carry = None
    for i in (range(n) if forward else range(n - 1, -1, -1)):
        y = g[i * SUBLANES:(i + 1) * SUBLANES, :]
        for sh in (1, 2, 4):
            if forward:
                y = y + jnp.where(sub >= sh, pltpu.roll(y, sh, 0), 0.0)
            else:
                y = y + jnp.where(sub < SUBLANES - sh, pltpu.roll(y, SUBLANES - sh, 0), 0.0)
        if carry is not None:
            y = y + carry
        edge = y[SUBLANES - 1:SUBLANES, :] if forward else y[0:1, :]
        carry = jnp.broadcast_to(edge, y.shape)
        outs[i] = y
    return jnp.concatenate(outs, axis=0)


def _hgrn_kernel(q_ref, i_ref, ff_ref, fb_ref, gate_ref, lbl_ref, gn_ref, o_ref, acc_ref):
    hb = HGRN_BLOCK
    half = hb // 2
    n_blocks = q_ref.shape[0] // hb
    group = min(HGRN_GROUP, n_blocks)
    dk = HGRN_HEAD_DIM
    row = lax.broadcasted_iota(jnp.int32, (hb, hb), 0)
    col = lax.broadcasted_iota(jnp.int32, (hb, hb), 1)

    logits = lbl_ref[...]
    mx = jnp.max(logits, axis=1, keepdims=True)
    ex = jnp.exp(logits - mx)
    lb_all = ex[:, 0, :] / jnp.sum(ex, axis=1)
    gn = gn_ref[...]

    def direction(f_ref, lb, forward):
        keep = (col <= row) if forward else (col >= row)

        def body(step, state_t):
            gi = step if forward else n_blocks // group - 1 - step
            base = gi * (group * hb)
            units = []
            for u in range(group):
                rows = pl.ds(pl.multiple_of(base + u * hb, hb), hb)
                q = _silu(q_ref[rows, :].astype(F32))
                v = i_ref[rows, :]
                f = lb + (1.0 - lb) * jax.nn.sigmoid(f_ref[rows, :].astype(F32))
                k = 1.0 - f
                a = _row_scan(jnp.log(f), forward)
                ref = a[half - 1:half, :] if forward else a[half:half + 1, :]
                end = a[hb - 1:hb, :] if forward else a[0:1, :]
                qt = q * jnp.exp(a - ref)
                kt = k * jnp.exp(ref - a)
                scores = jnp.where(keep, _dot_nt(qt.astype(BF16), kt.astype(BF16)), 0.0)
                o_intra = jnp.dot(scores.astype(BF16), v, preferred_element_type=F32)
                q_in = (qt * jnp.exp(ref)).astype(BF16)
                k_end = (kt * jnp.exp(end - ref)).astype(BF16)
                kv_t = lax.dot_general(v, k_end, (((0,), (0,)), ((), ())),
                                       preferred_element_type=F32)
                units.append((rows, o_intra, q_in, kv_t, jnp.exp(end)))
            for rows, o_intra, q_in, kv_t, decay in (units if forward else units[::-1]):
                o = o_intra + _dot_nt(q_in, state_t.astype(BF16))
                state_t = state_t * decay + kv_t
                if forward:
                    acc_ref[rows, :] = o
                else:
                    tot = acc_ref[rows, :] + o
                    inv = lax.rsqrt(jnp.mean(tot * tot, axis=-1, keepdims=True) + RMS_EPS)
                    gate = gate_ref[rows, :].astype(F32)
                    o_ref[rows, :] = (tot * inv * gn * _silu(gate)).astype(o_ref.dtype)
            return state_t

        lax.fori_loop(0, n_blocks // group, body, jnp.zeros((dk, dk), F32))

    direction(ff_ref, lb_all[0:1, :], True)
    direction(fb_ref, lb_all[1:2, :], False)


def _hgrn_mix_gated(hg, lb_logits, g_norm, b, l):
    nh, dk = HGRN_HEADS, HGRN_HEAD_DIM

    def col(block):
        return pl.BlockSpec((l, dk), lambda i, h: (i, block * nh + h))

    return pl.pallas_call(
        _hgrn_kernel,
        grid=(b, nh),
        in_specs=[col(0), col(1), col(2), col(3), col(4),
                  pl.BlockSpec((2, lb_logits.shape[1], dk), lambda i, h: (0, 0, h)),
                  pl.BlockSpec((1, dk), lambda i, h: (0, 0))],
        out_specs=pl.BlockSpec((l, dk), lambda i, h: (i, h)),
        out_shape=jax.ShapeDtypeStruct((b * l, D_HGRN), BF16),
        scratch_shapes=[pltpu.VMEM((l, dk), F32)],
        compiler_params=_params(("parallel", "parallel"), 48),
        name="hgrn2",
    )(hg, hg, hg, hg, hg, lb_logits.astype(F32), g_norm.reshape(1, dk).astype(F32))


ATTN_TQ = 256
ATTN_TK = 512
ATTN_Q_TILES = 8
ATTN_QK_SCALE = math.sqrt(DIFF_HEAD_DIM ** -0.5 * LOG2E)


def _attn_kernel(slopes_ref, q_ref, k_ref, vt_ref, gate_ref, lq1_ref, lk1_ref, lq2_ref, lk2_ref,
                 subln_ref, o_ref, qbd_ref, d0_ref, ta_ref, tb_ref, m_ref, l_ref, acc_ref,
                 *, lambda_init):
    tq, tk, nqt = ATTN_TQ, ATTN_TK, ATTN_Q_TILES
    n_steps = (k_ref.shape[0] // tk) * nqt
    dh = DIFF_HEAD_DIM
    h = pl.program_id(1)
    sup = pl.program_id(2)
    slope2 = slopes_ref[h] * LOG2E
    kk = lax.broadcasted_iota(jnp.int32, (tk, tq), 0)
    qq = lax.broadcasted_iota(jnp.int32, (tk, tq), 1)
    d0 = (kk - qq).astype(F32) * slope2
    d0_ref[0] = d0
    d0_ref[1] = -d0
    for r in range(tk // tq):
        d0_ref[2 + r] = -jnp.abs(d0 - (r * tq) * slope2)

    zeros = jnp.zeros((tq, dh), BF16)
    for t in range(nqt):
        rows = slice(t * tq, (t + 1) * tq)
        qbd_ref[t, 0:tq, 0:dh] = q_ref[rows, 0:dh]
        qbd_ref[t, 0:tq, dh:2 * dh] = zeros
        qbd_ref[t, tq:2 * tq, 0:dh] = zeros
        qbd_ref[t, tq:2 * tq, dh:2 * dh] = q_ref[rows, dh:2 * dh]

    m_ref[...] = jnp.full(m_ref.shape, -1e30, F32)
    l_ref[...] = jnp.zeros(l_ref.shape, F32)
    acc_ref[...] = jnp.zeros(acc_ref.shape, F32)

    def split(s):
        k0 = pl.multiple_of((s // nqt) * tk, tk)
        t = s % nqt
        rel = (sup * nqt + t) * tq - k0
        before, after = rel >= tk, rel <= -tq
        idx = jnp.where(before, 0, jnp.where(after, 1, 2 + rel // tq))
        lin = rel.astype(F32) * slope2
        const = jnp.where(before, -lin, jnp.where(after, lin, 0.0))
        return k0, t, idx, const

    def scores(s, t_ref):
        k0, t, idx, _ = split(s)
        bias = d0_ref[idx]
        sc = _dot_nt(k_ref[pl.ds(k0, tk), :], qbd_ref[t])
        t_ref[:, 0:tq] = sc[:, 0:tq] + bias
        t_ref[:, tq:2 * tq] = sc[:, tq:2 * tq] + bias

    def accumulate(s, t_ref):
        k0, t, _, const = split(s)
        sc = t_ref[...]
        m_old = m_ref[t]
        m_new = jnp.maximum(m_old, jnp.max(sc, axis=0, keepdims=True) + const)
        alpha = jnp.exp2(m_old - m_new)
        p = jnp.exp2(sc - (m_new - const))
        l_ref[t] = alpha * l_ref[t] + jnp.sum(p, axis=0, keepdims=True)
        acc_ref[t] = alpha * acc_ref[t] + jnp.dot(vt_ref[:, pl.ds(k0, tk)], p.astype(BF16),
                                                  preferred_element_type=F32)
        m_ref[t] = m_new

    scores(0, ta_ref)

    def body(i, carry):
        scores(2 * i + 1, tb_ref)
        accumulate(2 * i, ta_ref)
        scores(2 * i + 2, ta_ref)
        accumulate(2 * i + 1, tb_ref)
        return carry

    lax.fori_loop(0, n_steps // 2 - 1, body, 0, unroll=5)
    scores(n_steps - 1, tb_ref)
    accumulate(n_steps - 2, ta_ref)
    accumulate(n_steps - 1, tb_ref)

    lam = (jnp.exp(jnp.sum(lq1_ref[...] * lk1_ref[...], keepdims=True))
           - jnp.exp(jnp.sum(lq2_ref[...] * lk2_ref[...], keepdims=True)) + lambda_init)
    for t in range(nqt):
        rows = slice(t * tq, (t + 1) * tq)
        o_both = acc_ref[t] / l_ref[t]
        o = (o_both[:, 0:tq] - lam * o_both[:, tq:2 * tq]).T
        inv = lax.rsqrt(jnp.mean(o * o, axis=-1, keepdims=True) + RMS_EPS)
        o = o * inv * subln_ref[...] * (1.0 - lambda_init)
        o_ref[rows, :] = (o * _silu(gate_ref[rows, :].astype(F32))).astype(o_ref.dtype)


def _diff_attention_gated(qk, v_t, gate, lq1, lk1, lq2, lk2, subln, lambda_init, b, l):
    nh, dv = DIFF_HEADS, DIFF_V_DIM
    tq, tk, nqt = ATTN_TQ, ATTN_TK, ATTN_Q_TILES
    qs = tq * nqt
    ns = l // qs
    assert l % qs == 0 and (l // tk * nqt) % 2 == 0
    slopes = jnp.exp2(-8.0 * (jnp.arange(nh, dtype=F32) + 1.0) / nh)
    vec = lambda a: a.reshape(1, -1).astype(F32)
    small = lambda n: pl.BlockSpec((1, n), lambda i, h, s, *_: (0, 0))
    grid_spec = pltpu.PrefetchScalarGridSpec(
        num_scalar_prefetch=1,
        grid=(b, nh, ns),
        in_specs=[pl.BlockSpec((qs, dv), lambda i, h, s, *_: (i * ns + s, h)),
                  pl.BlockSpec((l, dv), lambda i, h, s, *_: (i, nh + h)),
                  pl.BlockSpec((dv, l), lambda i, h, s, *_: (h, i)),
                  pl.BlockSpec((qs, dv), lambda i, h, s, *_: (i * ns + s, h)),
                  small(DIFF_HEAD_DIM), small(DIFF_HEAD_DIM), small(DIFF_HEAD_DIM),
                  small(DIFF_HEAD_DIM), small(dv)],
        out_specs=pl.BlockSpec((qs, dv), lambda i, h, s, *_: (i * ns + s, h)),
        scratch_shapes=[pltpu.VMEM((nqt, 2 * tq, 2 * DIFF_HEAD_DIM), BF16),
                        pltpu.VMEM((2 + tk // tq, tk, tq), F32),
                        pltpu.VMEM((tk, 2 * tq), F32),
                        pltpu.VMEM((tk, 2 * tq), F32),
                        pltpu.VMEM((nqt, 1, 2 * tq), F32), pltpu.VMEM((nqt, 1, 2 * tq), F32),
                        pltpu.VMEM((nqt, dv, 2 * tq), F32)],
    )
    return pl.pallas_call(
        functools.partial(_attn_kernel, lambda_init=lambda_init),
        grid_spec=grid_spec,
        out_shape=jax.ShapeDtypeStruct((b * l, D_ATTN), BF16),
        compiler_params=_params(("parallel", "parallel", "parallel"), 48),
        name="diff_attention",
    )(slopes, qk, qk, v_t, gate, vec(lq1), vec(lk1), vec(lq2), vec(lk2), vec(subln))


def _out_kernel(*refs, n_act):
    act_refs, w_refs = refs[:n_act], refs[n_act:2 * n_act]
    x_ref, g_ref, o_ref = refs[2 * n_act:]
    y = jnp.dot(act_refs[0][...], w_refs[0][...], preferred_element_type=F32)
    for a_ref, w_ref in zip(act_refs[1:], w_refs[1:]):
        y = y + jnp.dot(a_ref[...], w_ref[...], preferred_element_type=F32)
    inv = lax.rsqrt(jnp.mean(y * y, axis=-1, keepdims=True) + RMS_EPS)
    o_ref[...] = x_ref[...] + y * inv * g_ref[...]


def _out_proj_residual(acts, ws, x, g, tm=512):
    t, d = x.shape
    n_act = len(acts)
    in_specs = ([pl.BlockSpec((tm, a.shape[1]), lambda i: (i, 0)) for a in acts]
                + [pl.BlockSpec(w.shape, lambda i: (0, 0)) for w in ws]
                + [pl.BlockSpec((tm, d), lambda i: (i, 0)), pl.BlockSpec((1, d), lambda i: (0, 0))])
    return pl.pallas_call(
        functools.partial(_out_kernel, n_act=n_act),
        grid=(t // tm,),
        in_specs=in_specs,
        out_specs=pl.BlockSpec((tm, d), lambda i: (i, 0)),
        out_shape=jax.ShapeDtypeStruct((t, d), F32),
        compiler_params=_params(("parallel",), 48),
        name="out_proj_residual",
    )(*acts, *ws, x, g.reshape(1, d).astype(F32))


def _trunk(x3, wts):
    b, l, d = x3.shape
    x = x3.reshape(b * l, d)

    h = _norm(x, wts["ev_norm_pre"])
    w_in = wts["ev_w_in"]
    u = _matmul_cols(h, w_in, 0, D_FNET)
    gate_a = _matmul_cols(h, w_in, D_FNET, D_FNET)
    hg = _matmul_cols(h, w_in, 2 * D_FNET, 5 * D_HGRN)
    y_a = _fnet_mix_gated(u, gate_a, b, l)
    y_b = _hgrn_mix_gated(hg, wts["hgrn_lb_logits"], wts["hgrn_norm"], b, l)
    w_out = wts["ev_w_out"]
    x = _out_proj_residual([y_a, y_b], [w_out[:D_FNET], w_out[D_FNET:]], x, wts["ev_norm_post"])

    lambda_init = 0.8 - 0.6 * math.exp(-0.3 * 1)
    h = _norm(x, wts["od_norm_pre"])
    w_in = wts["od_w_in"]
    qk = _matmul_cols(h, w_in, 0, 2 * D_ATTN, out_scale=ATTN_QK_SCALE)
    v_t = _matmul_cols(h, w_in, 2 * D_ATTN, D_ATTN, transpose_out=True)
    gate = _matmul_cols(h, w_in, 3 * D_ATTN, D_ATTN)
    o = _diff_attention_gated(qk, v_t, gate, wts["lambda_q1"], wts["lambda_k1"], wts["lambda_q2"],
                              wts["lambda_k2"], wts["subln"], lambda_init, b, l)
    x = _out_proj_residual([o], [wts["od_w_out"]], x, wts["od_norm_post"])
    return x.reshape(b, l, d)


def kernel(x_prompt, x_sample, ev_w_in, ev_w_out, ev_norm_pre, ev_norm_post, hgrn_lb_logits,
           hgrn_norm, od_w_in, od_w_out, od_norm_pre, od_norm_post,
           lambda_q1, lambda_k1, lambda_q2, lambda_k2, subln):
    wts = {
        "ev_w_in": ev_w_in[0].astype(BF16), "ev_w_out": ev_w_out[0].astype(BF16),
        "ev_norm_pre": ev_norm_pre[0], "ev_norm_post": ev_norm_post[0],
        "hgrn_lb_logits": hgrn_lb_logits, "hgrn_norm": hgrn_norm[0],
        "od_w_in": od_w_in[0].astype(BF16), "od_w_out": od_w_out[0].astype(BF16),
        "od_norm_pre": od_norm_pre[0], "od_norm_post": od_norm_post[0],
        "lambda_q1": lambda_q1[0], "lambda_k1": lambda_k1[0],
        "lambda_q2": lambda_q2[0], "lambda_k2": lambda_k2[0], "subln": subln[0],
    }
    return (_trunk(x_prompt, wts), _trunk(x_sample, wts))
```

```python
import functools
import math

import jax
import jax.numpy as jnp
from jax import lax
from jax.experimental import pallas as pl
from jax.experimental.pallas import tpu as pltpu

F32 = jnp.float32
BF16 = jnp.bfloat16

D_MODEL = 2048
D_FNET = 1024
FNET_GROUP_DIM = 256
FNET_GROUPS = D_FNET // FNET_GROUP_DIM
D_HGRN = 1024
HGRN_HEAD_DIM = 128
HGRN_HEADS = D_HGRN // HGRN_HEAD_DIM
DIFF_HEADS = 8
DIFF_HEAD_DIM = 128
DIFF_V_DIM = 256
D_ATTN = DIFF_HEADS * DIFF_V_DIM
RMS_EPS = 1e-6
LOG2E = 1.4426950408889634

SUBLANES = 8
FFT_L2 = 128
FFT_KB = 8
HGRN_BLOCK = 128
HGRN_GROUP = 8
MIB = 1024 * 1024


def _params(semantics, vmem_mib):
    return pltpu.CompilerParams(dimension_semantics=semantics, vmem_limit_bytes=vmem_mib * MIB)


def _silu(x):
    return x * jax.nn.sigmoid(x)


def _dot_nt(a, b):
    return lax.dot_general(a, b, (((1,), (1,)), ((), ())), preferred_element_type=F32)


def _norm_kernel(x_ref, g_ref, o_ref):
    x = x_ref[...]
    inv = lax.rsqrt(jnp.mean(x * x, axis=-1, keepdims=True) + RMS_EPS)
    o_ref[...] = (x * inv * g_ref[...]).astype(o_ref.dtype)


def _norm(x, g, tm=512):
    t, d = x.shape
    return pl.pallas_call(
        _norm_kernel,
        grid=(t // tm,),
        in_specs=[pl.BlockSpec((tm, d), lambda i: (i, 0)),
                  pl.BlockSpec((1, d), lambda i: (0, 0))],
        out_specs=pl.BlockSpec((tm, d), lambda i: (i, 0)),
        out_shape=jax.ShapeDtypeStruct((t, d), BF16),
        compiler_params=_params(("parallel",), 32),
        name="rmsnorm_bf16",
    )(x, g.reshape(1, d).astype(F32))


def _mm_kernel(h_ref, w_ref, o_ref, *, out_scale):
    r = jnp.dot(h_ref[...], w_ref[...], preferred_element_type=F32)
    if out_scale != 1.0:
        r = r * out_scale
    o_ref[...] = r.astype(o_ref.dtype)


def _mm_t_kernel(h_ref, w_ref, o_ref, r_ref):
    r_ref[...] = jnp.dot(h_ref[...], w_ref[...], preferred_element_type=F32)
    o_ref[...] = r_ref[...].T.astype(o_ref.dtype)


def _matmul_cols(h, w, col_off, n_cols, transpose_out=False, out_scale=1.0, tm=1024, tn=1024):
    t, k = h.shape
    tm = min(tm, t)
    off = col_off // tn
    if transpose_out:
        assert out_scale == 1.0
        body, out_shape, scratch = _mm_t_kernel, (n_cols, t), [pltpu.VMEM((tm, tn), F32)]
        out_spec = pl.BlockSpec((tn, tm), lambda n, m: (n, m))
    else:
        body = functools.partial(_mm_kernel, out_scale=out_scale)
        out_shape, scratch = (t, n_cols), []
        out_spec = pl.BlockSpec((tm, tn), lambda n, m: (m, n))
    return pl.pallas_call(
        body,
        grid=(n_cols // tn, t // tm),
        in_specs=[pl.BlockSpec((tm, k), lambda n, m: (m, 0)),
                  pl.BlockSpec((k, tn), lambda n, m: (0, n + off))],
        out_specs=out_spec,
        out_shape=jax.ShapeDtypeStruct(out_shape, BF16),
        scratch_shapes=scratch,
        compiler_params=_params(("parallel", "parallel"), 40),
        name="in_proj_t" if transpose_out else "in_proj",
    )(h, w)


def _fft_tables(l):
    l1 = l // FFT_L2
    two_pi = 2.0 * math.pi
    k1 = jnp.arange(l1, dtype=jnp.int32)
    a1 = ((k1[:, None] * k1[None, :]) % l1).astype(F32) * (two_pi / l1)
    f1 = jnp.concatenate([jnp.cos(a1), -jnp.sin(a1)], axis=0).astype(BF16)
    k2 = jnp.arange(FFT_L2, dtype=jnp.int32)
    kk = k1[:, None, None] + l1 * k2[None, :, None]
    a2 = ((kk * k2[None, None, :]) % l).astype(F32) * (two_pi / l)
    c2, s2 = jnp.cos(a2), jnp.sin(a2)
    g2 = jnp.concatenate([jnp.concatenate([c2, s2], axis=2),
                          jnp.concatenate([-s2, c2], axis=2)], axis=1).astype(BF16)
    c = jnp.arange(FNET_GROUP_DIM, dtype=jnp.int32)
    a3 = ((c[:, None] * c[None, :]) % FNET_GROUP_DIM).astype(F32) * (two_pi / FNET_GROUP_DIM)
    scale = 1.0 / math.sqrt(l * FNET_GROUP_DIM)
    cs = (jnp.concatenate([jnp.cos(a3), jnp.sin(a3)], axis=0) * scale).astype(BF16)
    return f1, g2, cs


def _fft1_kernel(f_ref, u_ref, t_ref):
    l1 = u_ref.shape[1]
    r = jnp.dot(f_ref[...], u_ref[0], preferred_element_type=F32)
    t_ref[0, 0] = r[:l1].astype(t_ref.dtype)
    t_ref[1, 0] = r[l1:].astype(t_ref.dtype)


def _fft2_kernel(t_ref, g_ref, cs_ref, o_ref, p_scr):
    kb = g_ref.shape[0]
    l2 = FFT_L2
    gd = FNET_GROUP_DIM
    for j in range(kb):
        gm = g_ref[j]
        for g in range(FNET_GROUPS):
            cols = slice(g * gd, (g + 1) * gd)
            rhs = jnp.concatenate([t_ref[0, 0, j, :, cols], t_ref[1, 0, j, :, cols]], axis=0)
            p = jnp.dot(gm, rhs, preferred_element_type=F32)
            r0 = (j * FNET_GROUPS + g) * l2
            p_scr[r0:r0 + l2, 0:gd] = p[:l2].astype(p_scr.dtype)
            p_scr[r0:r0 + l2, gd:2 * gd] = p[l2:].astype(p_scr.dtype)
    y = jnp.dot(p_scr[...], cs_ref[...], preferred_element_type=F32)
    for j in range(kb):
        for g in range(FNET_GROUPS):
            r0 = (j * FNET_GROUPS + g) * l2
            cols = slice(j * D_FNET + g * gd, j * D_FNET + (g + 1) * gd)
            o_ref[0, :, cols] = y[r0:r0 + l2].astype(o_ref.dtype)


def _fnet_mix(u, b, l):
    l1, l2 = l // FFT_L2, FFT_L2
    f1, g2, cs = _fft_tables(l)
    wcols = l2 * D_FNET
    w = min(wcols, (2 * MIB) // (2 * l1))
    t = pl.pallas_call(
        _fft1_kernel,
        grid=(b, wcols // w),
        in_specs=[pl.BlockSpec((2 * l1, l1), lambda i, j: (0, 0)),
                  pl.BlockSpec((1, l1, w), lambda i, j: (i, 0, j))],
        out_specs=pl.BlockSpec((2, 1, l1, w), lambda i, j: (0, i, 0, j)),
        out_shape=jax.ShapeDtypeStruct((2, b, l1, wcols), BF16),
        compiler_params=_params(("parallel", "parallel"), 32),
        name="fnet_stage1",
    )(f1, u.reshape(b, l1, wcols))
    kb = min(FFT_KB, l1)
    y = pl.pallas_call(
        _fft2_kernel,
        grid=(b, l1 // kb),
        in_specs=[pl.BlockSpec((2, 1, kb, l2, D_FNET), lambda i, j: (0, i, j, 0, 0)),
                  pl.BlockSpec((kb, 2 * l2, 2 * l2), lambda i, j: (j, 0, 0)),
                  pl.BlockSpec((2 * FNET_GROUP_DIM, FNET_GROUP_DIM), lambda i, j: (0, 0))],
        out_specs=pl.BlockSpec((1, l2, kb * D_FNET), lambda i, j: (i, 0, j)),
        out_shape=jax.ShapeDtypeStruct((b, l2, l1 * D_FNET), BF16),
        scratch_shapes=[pltpu.VMEM((kb * FNET_GROUPS * l2, 2 * FNET_GROUP_DIM), BF16)],
        compiler_params=_params(("parallel", "parallel"), 48),
        name="fnet_stage2",
    )(t.reshape(2, b, l1, l2, D_FNET), g2, cs)
    return y.reshape(b * l, D_FNET)


def _row_scan(g, forward):
    n = g.shape[0] // SUBLANES
    sub = lax.broadcasted_iota(jnp.int32, (SUBLANES, g.shape[1]), 0)
    outs = [None] * n
    carry = None
    for i in (range(n) if forward else range(n - 1, -1, -1)):
        y = g[i * SUBLANES:(i + 1) * SUBLANES, :]
        for sh in (1, 2, 4):
            if forward:
                y = y + jnp.where(sub >= sh, pltpu.roll(y, sh, 0), 0.0)
            else:
                y = y + jnp.where(sub < SUBLANES - sh, pltpu.roll(y, SUBLANES - sh, 0), 0.0)
        if carry is not None:
            y = y + carry
        edge = y[SUBLANES - 1:SUBLANES, :] if forward else y[0:1, :]
        carry = jnp.broadcast_to(edge, y.shape)
        outs[i] = y
    return jnp.concatenate(outs, axis=0)


def _hgrn_kernel(q_ref, i_ref, ff_ref, fb_ref, gate_ref, lbl_ref, gn_ref, o_ref, acc_ref):
    hb = HGRN_BLOCK
    half = hb // 2
    n_blocks = q_ref.shape[0] // hb
    group = min(HGRN_GROUP, n_blocks)
    dk = HGRN_HEAD_DIM
    row = lax.broadcasted_iota(jnp.int32, (hb, hb), 0)
    col = lax.broadcasted_iota(jnp.int32, (hb, hb), 1)

    logits = lbl_ref[...]
    mx = jnp.max(logits, axis=1, keepdims=True)
    ex = jnp.exp(logits - mx)
    lb_all = ex[:, 0, :] / jnp.sum(ex, axis=1)
    gn = gn_ref[...]

    def direction(f_ref, lb, forward):
        keep = (col <= row) if forward else (col >= row)

        def body(step, state_t):
            gi = step if forward else n_blocks // group - 1 - step
            base = gi * (group * hb)
            units = []
            for u in range(group):
                rows = pl.ds(pl.multiple_of(base + u * hb, hb), hb)
                q = _silu(q_ref[rows, :].astype(F32))
                v = i_ref[rows, :]
                f = lb + (1.0 - lb) * jax.nn.sigmoid(f_ref[rows, :].astype(F32))
                k = 1.0 - f
                a = _row_scan(jnp.log2(f), forward)
                ref = a[half - 1:half, :] if forward else a[half:half + 1, :]
                end = a[hb - 1:hb, :] if forward else a[0:1, :]
                qt = q * jnp.exp2(a - ref)
                kt = k * jnp.exp2(ref - a)
                scores = jnp.where(keep, _dot_nt(qt.astype(BF16), kt.astype(BF16)), 0.0)
                o_intra = jnp.dot(scores.astype(BF16), v, preferred_element_type=F32)
                q_in = (qt * jnp.exp2(ref)).astype(BF16)
                k_end = (kt * jnp.exp2(end - ref)).astype(BF16)
                kv_t = lax.dot_general(v, k_end, (((0,), (0,)), ((), ())),
                                       preferred_element_type=F32)
                units.append((rows, o_intra, q_in, kv_t, jnp.exp2(end)))
            for rows, o_intra, q_in, kv_t, decay in (units if forward else units[::-1]):
                o = o_intra + _dot_nt(q_in, state_t.astype(BF16))
                state_t = state_t * decay + kv_t
                if forward:
                    acc_ref[rows, :] = o
                else:
                    tot = acc_ref[rows, :] + o
                    inv = lax.rsqrt(jnp.mean(tot * tot, axis=-1, keepdims=True) + RMS_EPS)
                    gate = gate_ref[rows, :].astype(F32)
                    o_ref[rows, :] = (tot * inv * gn * _silu(gate)).astype(o_ref.dtype)
            return state_t

        lax.fori_loop(0, n_blocks // group, body, jnp.zeros((dk, dk), F32))

    direction(ff_ref, lb_all[0:1, :], True)
    direction(fb_ref, lb_all[1:2, :], False)


def _hgrn_mix_gated(hg, lb_logits, g_norm, b, l):
    nh, dk = HGRN_HEADS, HGRN_HEAD_DIM

    def col(block):
        return pl.BlockSpec((l, dk), lambda i, h: (i, block * nh + h))

    return pl.pallas_call(
        _hgrn_kernel,
        grid=(b, nh),
        in_specs=[col(0), col(1), col(2), col(3), col(4),
                  pl.BlockSpec((2, lb_logits.shape[1], dk), lambda i, h: (0, 0, h)),
                  pl.BlockSpec((1, dk), lambda i, h: (0, 0))],
        out_specs=pl.BlockSpec((l, dk), lambda i, h: (i, h)),
        out_shape=jax.ShapeDtypeStruct((b * l, D_HGRN), BF16),
        scratch_shapes=[pltpu.VMEM((l, dk), F32)],
        compiler_params=_params(("parallel", "parallel"), 48),
        name="hgrn2",
    )(hg, hg, hg, hg, hg, lb_logits.astype(F32), g_norm.reshape(1, dk).astype(F32))


ATTN_TQ = 256
ATTN_TK = 512
ATTN_Q_TILES = 8
ATTN_QK_SCALE = math.sqrt(DIFF_HEAD_DIM ** -0.5 * LOG2E)


def _attn_kernel(slopes_ref, q_ref, k_ref, vt_ref, gate_ref, lq1_ref, lk1_ref, lq2_ref, lk2_ref,
                 subln_ref, o_ref, qbd_ref, d0_ref, ta_ref, tb_ref, m_ref, l_ref, acc_ref,
                 *, lambda_init):
    tq, tk, nqt = ATTN_TQ, ATTN_TK, ATTN_Q_TILES
    n_steps = (k_ref.shape[0] // tk) * nqt
    dh = DIFF_HEAD_DIM
    h = pl.program_id(1)
    sup = pl.program_id(2)
    slope2 = slopes_ref[h] * LOG2E
    kk = lax.broadcasted_iota(jnp.int32, (tk, tq), 0)
    qq = lax.broadcasted_iota(jnp.int32, (tk, tq), 1)
    d0 = (kk - qq).astype(F32) * slope2
    d0_ref[0] = d0
    d0_ref[1] = -d0
    for r in range(tk // tq):
        d0_ref[2 + r] = -jnp.abs(d0 - (r * tq) * slope2)

    zeros = jnp.zeros((tq, dh), BF16)
    for t in range(nqt):
        rows = slice(t * tq, (t + 1) * tq)
        qbd_ref[t, 0:tq, 0:dh] = q_ref[rows, 0:dh]
        qbd_ref[t, 0:tq, dh:2 * dh] = zeros
        qbd_ref[t, tq:2 * tq, 0:dh] = zeros
        qbd_ref[t, tq:2 * tq, dh:2 * dh] = q_ref[rows, dh:2 * dh]

    m_ref[...] = jnp.full(m_ref.shape, -1e30, F32)
    l_ref[...] = jnp.zeros(l_ref.shape, F32)
    acc_ref[...] = jnp.zeros(acc_ref.shape, F32)

    def split(s):
        k0 = pl.multiple_of((s // nqt) * tk, tk)
        t = s % nqt
        rel = (sup * nqt + t) * tq - k0
        before, after = rel >= tk, rel <= -tq
        idx = jnp.where(before, 0, jnp.where(after, 1, 2 + rel // tq))
        lin = rel.astype(F32) * slope2
        const = jnp.where(before, -lin, jnp.where(after, lin, 0.0))
        return k0, t, idx, const

    def scores(s, t_ref):
        k0, t, idx, _ = split(s)
        bias = d0_ref[idx]
        sc = _dot_nt(k_ref[pl.ds(k0, tk), :], qbd_ref[t])
        t_ref[:, 0:tq] = sc[:, 0:tq] + bias
        t_ref[:, tq:2 * tq] = sc[:, tq:2 * tq] + bias

    def accumulate(s, t_ref):
        k0, t, _, const = split(s)
        sc = t_ref[...]
        m_old = m_ref[t]
        m_new = jnp.maximum(m_old, jnp.max(sc, axis=0, keepdims=True) + const)
        alpha = jnp.exp2(m_old - m_new)
        p = jnp.exp2(sc - (m_new - const))
        l_ref[t] = alpha * l_ref[t] + jnp.sum(p, axis=0, keepdims=True)
        acc_ref[t] = alpha * acc_ref[t] + jnp.dot(vt_ref[:, pl.ds(k0, tk)], p.astype(BF16),
                                                  preferred_element_type=F32)
        m_ref[t] = m_new

    scores(0, ta_ref)

    def body(i, carry):
        scores(2 * i + 1, tb_ref)
        accumulate(2 * i, ta_ref)
        scores(2 * i + 2, ta_ref)
        accumulate(2 * i + 1, tb_ref)
        return carry

    lax.fori_loop(0, n_steps // 2 - 1, body, 0, unroll=5)
    scores(n_steps - 1, tb_ref)
    accumulate(n_steps - 2, ta_ref)
    accumulate(n_steps - 1, tb_ref)

    lam = (jnp.exp(jnp.sum(lq1_ref[...] * lk1_ref[...], keepdims=True))
           - jnp.exp(jnp.sum(lq2_ref[...] * lk2_ref[...], keepdims=True)) + lambda_init)
    gain = subln_ref[...] * (1.0 - lambda_init)
    for t in range(nqt):
        rows = slice(t * tq, (t + 1) * tq)
        r = 1.0 / l_ref[t]
        o = (acc_ref[t, :, 0:tq] * r[:, 0:tq]
             - acc_ref[t, :, tq:2 * tq] * (lam * r[:, tq:2 * tq])).T
        inv = lax.rsqrt(jnp.mean(o * o, axis=-1, keepdims=True) + RMS_EPS)
        o_ref[rows, :] = (o * inv * (gain * _silu(gate_ref[rows, :].astype(F32)))
                          ).astype(o_ref.dtype)


def _diff_attention_gated(qk, v_t, gate, lq1, lk1, lq2, lk2, subln, lambda_init, b, l):
    nh, dv = DIFF_HEADS, DIFF_V_DIM
    tq, tk, nqt = ATTN_TQ, ATTN_TK, ATTN_Q_TILES
    qs = tq * nqt
    ns = l // qs
    assert l % qs == 0 and (l // tk * nqt) % 2 == 0
    slopes = jnp.exp2(-8.0 * (jnp.arange(nh, dtype=F32) + 1.0) / nh)
    vec = lambda a: a.reshape(1, -1).astype(F32)
    small = lambda n: pl.BlockSpec((1, n), lambda i, h, s, *_: (0, 0))
    grid_spec = pltpu.PrefetchScalarGridSpec(
        num_scalar_prefetch=1,
        grid=(b, nh, ns),
        in_specs=[pl.BlockSpec((qs, dv), lambda i, h, s, *_: (i * ns + s, h)),
                  pl.BlockSpec((l, dv), lambda i, h, s, *_: (i, nh + h)),
                  pl.BlockSpec((dv, l), lambda i, h, s, *_: (h, i)),
                  pl.BlockSpec((qs, dv), lambda i, h, s, *_: (i * ns + s, h)),
                  small(DIFF_HEAD_DIM), small(DIFF_HEAD_DIM), small(DIFF_HEAD_DIM),
                  small(DIFF_HEAD_DIM), small(dv)],
        out_specs=pl.BlockSpec((qs, dv), lambda i, h, s, *_: (i * ns + s, h)),
        scratch_shapes=[pltpu.VMEM((nqt, 2 * tq, 2 * DIFF_HEAD_DIM), BF16),
                        pltpu.VMEM((2 + tk // tq, tk, tq), F32),
                        pltpu.VMEM((tk, 2 * tq), F32),
                        pltpu.VMEM((tk, 2 * tq), F32),
                        pltpu.VMEM((nqt, 1, 2 * tq), F32), pltpu.VMEM((nqt, 1, 2 * tq), F32),
                        pltpu.VMEM((nqt, dv, 2 * tq), F32)],
    )
    return pl.pallas_call(
        functools.partial(_attn_kernel, lambda_init=lambda_init),
        grid_spec=grid_spec,
        out_shape=jax.ShapeDtypeStruct((b * l, D_ATTN), BF16),
        compiler_params=_params(("parallel", "parallel", "parallel"), 48),
        name="diff_attention",
    )(slopes, qk, qk, v_t, gate, vec(lq1), vec(lk1), vec(lq2), vec(lk2), vec(subln))


def _out_kernel(*refs, n_act, gated, with_next):
    refs = list(refs)
    act_refs = [refs.pop(0) for _ in range(n_act)]
    gate_ref = refs.pop(0) if gated else None
    w_refs = [refs.pop(0) for _ in range(n_act)]
    x_ref, g_ref = refs.pop(0), refs.pop(0)
    gn_ref = refs.pop(0) if with_next else None
    o_ref = refs.pop(0)
    acts = [a_ref[...] for a_ref in act_refs]
    if gated:
        acts[0] = (acts[0].astype(F32) * _silu(gate_ref[...].astype(F32))).astype(BF16)
    y = jnp.dot(acts[0], w_refs[0][...], preferred_element_type=F32)
    for a, w_ref in zip(acts[1:], w_refs[1:]):
        y = y + jnp.dot(a, w_ref[...], preferred_element_type=F32)
    inv = lax.rsqrt(jnp.mean(y * y, axis=-1, keepdims=True) + RMS_EPS)
    out = x_ref[...] + y * inv * g_ref[...]
    o_ref[...] = out
    if with_next:
        inv_n = lax.rsqrt(jnp.mean(out * out, axis=-1, keepdims=True) + RMS_EPS)
        refs.pop(0)[...] = (out * inv_n * gn_ref[...]).astype(BF16)


def _out_proj_residual(acts, ws, x, g, gate=None, g_next=None, tm=512):
    t, d = x.shape
    n_act = len(acts)
    row = lambda n: pl.BlockSpec((tm, n), lambda i: (i, 0))
    vec = lambda: pl.BlockSpec((1, d), lambda i: (0, 0))
    args = list(acts)
    in_specs = [row(a.shape[1]) for a in acts]
    if gate is not None:
        args.append(gate)
        in_specs.append(row(gate.shape[1]))
    args += list(ws) + [x, g.reshape(1, d).astype(F32)]
    in_specs += [pl.BlockSpec(w.shape, lambda i: (0, 0)) for w in ws] + [row(d), vec()]
    out_specs, out_shape = row(d), jax.ShapeDtypeStruct((t, d), F32)
    if g_next is not None:
        args.append(g_next.reshape(1, d).astype(F32))
        in_specs.append(vec())
        out_specs = [out_specs, row(d)]
        out_shape = [out_shape, jax.ShapeDtypeStruct((t, d), BF16)]
    return pl.pallas_call(
        functools.partial(_out_kernel, n_act=n_act, gated=gate is not None,
                          with_next=g_next is not None),
        grid=(t // tm,),
        in_specs=in_specs,
        out_specs=out_specs,
        out_shape=out_shape,
        compiler_params=_params(("parallel",), 48),
        name="out_proj_residual",
    )(*args)


def _trunk(x3, wts):
    b, l, d = x3.shape
    x = x3.reshape(b * l, d)

    h = _norm(x, wts["ev_norm_pre"])
    w_in = wts["ev_w_in"]
    u = _matmul_cols(h, w_in, 0, D_FNET)
    gate_a = _matmul_cols(h, w_in, D_FNET, D_FNET)
    hg = _matmul_cols(h, w_in, 2 * D_FNET, 5 * D_HGRN)
    y_a = _fnet_mix(u, b, l)
    y_b = _hgrn_mix_gated(hg, wts["hgrn_lb_logits"], wts["hgrn_norm"], b, l)
    w_out = wts["ev_w_out"]
    x, h = _out_proj_residual([y_a, y_b], [w_out[:D_FNET], w_out[D_FNET:]], x, wts["ev_norm_post"],
                              gate=gate_a, g_next=wts["od_norm_pre"])

    lambda_init = 0.8 - 0.6 * math.exp(-0.3 * 1)
    w_in = wts["od_w_in"]
    qk = _matmul_cols(h, w_in, 0, 2 * D_ATTN, out_scale=ATTN_QK_SCALE)
    v_t = _matmul_cols(h, w_in, 2 * D_ATTN, D_ATTN, transpose_out=True)
    gate = _matmul_cols(h, w_in, 3 * D_ATTN, D_ATTN)
    o = _diff_attention_gated(qk, v_t, gate, wts["lambda_q1"], wts["lambda_k1"], wts["lambda_q2"],
                              wts["lambda_k2"], wts["subln"], lambda_init, b, l)
    x = _out_proj_residual([o], [wts["od_w_out"]], x, wts["od_norm_post"])
    return x.reshape(b, l, d)


def kernel(x_prompt, x_sample, ev_w_in, ev_w_out, ev_norm_pre, ev_norm_post, hgrn_lb_logits,
           hgrn_norm, od_w_in, od_w_out, od_norm_pre, od_norm_post,
           lambda_q1, lambda_k1, lambda_q2, lambda_k2, subln):
    wts = {
        "ev_w_in": ev_w_in[0].astype(BF16), "ev_w_out": ev_w_out[0].astype(BF16),
        "ev_norm_pre": ev_norm_pre[0], "ev_norm_post": ev_norm_post[0],
        "hgrn_lb_logits": hgrn_lb_logits, "hgrn_norm": hgrn_norm[0],
        "od_w_in": od_w_in[0].astype(BF16), "od_w_out": od_w_out[0].astype(BF16),
        "od_norm_pre": od_norm_pre[0], "od_norm_post": od_norm_post[0],
        "lambda_q1": lambda_q1[0], "lambda_k1": lambda_k1[0],
        "lambda_q2": lambda_q2[0], "lambda_k2": lambda_k2[0], "subln": subln[0],
    }
    return (_trunk(x_prompt, wts), _trunk(x_sample, wts))
```

```python
import functools
import math

import jax
import jax.numpy as jnp
from jax import lax
from jax.experimental import pallas as pl
from jax.experimental.pallas import tpu as pltpu

F32 = jnp.float32
BF16 = jnp.bfloat16

D_MODEL = 2048
D_FNET = 1024
FNET_GROUP_DIM = 256
FNET_GROUPS = D_FNET // FNET_GROUP_DIM
D_HGRN = 1024
HGRN_HEAD_DIM = 128
HGRN_HEADS = D_HGRN // HGRN_HEAD_DIM
DIFF_HEADS = 8
DIFF_HEAD_DIM = 128
DIFF_V_DIM = 256
D_ATTN = DIFF_HEADS * DIFF_V_DIM
RMS_EPS = 1e-6
LOG2E = 1.4426950408889634

SUBLANES = 8
LANES = 128
NORM_ROWS = 1024
FFT_L2 = 128
FFT_KB = 8
HGRN_BLOCK = 128
HGRN_GROUP = 8
OUT_ROW_SPLIT = 2
MIB = 1024 * 1024


def _params(semantics, vmem_mib):
    return pltpu.CompilerParams(dimension_semantics=semantics, vmem_limit_bytes=vmem_mib * MIB)


def _silu(x):
    return x * jax.nn.sigmoid(x)


def _dot_nt(a, b):
    return lax.dot_general(a, b, (((1,), (1,)), ((), ())), preferred_element_type=F32)


def _norm_kernel(x_ref, g_ref, o_ref):
    x = x_ref[...]
    inv = lax.rsqrt(jnp.mean(x * x, axis=-1, keepdims=True) + RMS_EPS)
    o_ref[...] = (x * inv * g_ref[...]).astype(o_ref.dtype)


def _norm(x, g, tm=512):
    t, d = x.shape
    return pl.pallas_call(
        _norm_kernel,
        grid=(t // tm,),
        in_specs=[pl.BlockSpec((tm, d), lambda i: (i, 0)),
                  pl.BlockSpec((1, d), lambda i: (0, 0))],
        out_specs=pl.BlockSpec((tm, d), lambda i: (i, 0)),
        out_shape=jax.ShapeDtypeStruct((t, d), BF16),
        compiler_params=_params(("parallel",), 32),
        name="rmsnorm_bf16",
    )(x, g.reshape(1, d).astype(F32))


def _mm_kernel(h_ref, w_ref, o_ref, *, out_scale):
    r = jnp.dot(h_ref[...], w_ref[...], preferred_element_type=F32)
    if out_scale != 1.0:
        r = r * out_scale
    o_ref[...] = r.astype(o_ref.dtype)


def _mm_norm_kernel(h_ref, w_ref, o_ref, n_ref, *, out_scale):
    r = jnp.dot(h_ref[...], w_ref[...], preferred_element_type=F32) * out_scale
    o = r.astype(o_ref.dtype)
    o_ref[...] = o
    of = o.astype(F32)
    sq = of * of
    for g in range(sq.shape[1] // LANES):
        ss = jnp.sum(sq[:, g * LANES:(g + 1) * LANES], axis=1, keepdims=True)
        n_ref[0, 0, g:g + 1, :] = jnp.broadcast_to(jnp.max(ss, axis=0, keepdims=True), (1, LANES))


def _mm_t_kernel(h_ref, w_ref, o_ref, r_ref):
    r_ref[...] = jnp.dot(h_ref[...], w_ref[...], preferred_element_type=F32)
    o_ref[...] = r_ref[...].T.astype(o_ref.dtype)


def _matmul_cols(h, w, col_off, n_cols, transpose_out=False, out_scale=1.0, row_norms=False,
                 tm=1024, tn=1024):
    t, k = h.shape
    tm = min(tm, t)
    off = col_off // tn
    if row_norms:
        assert not transpose_out and tn // LANES == SUBLANES
        out, nrm = pl.pallas_call(
            functools.partial(_mm_norm_kernel, out_scale=out_scale),
            grid=(n_cols // tn, t // tm),
            in_specs=[pl.BlockSpec((tm, k), lambda n, m: (m, 0)),
                      pl.BlockSpec((k, tn), lambda n, m: (0, n + off))],
            out_specs=[pl.BlockSpec((tm, tn), lambda n, m: (m, n)),
                       pl.BlockSpec((1, 1, SUBLANES, LANES), lambda n, m: (n, m, 0, 0))],
            out_shape=[jax.ShapeDtypeStruct((t, n_cols), BF16),
                       jax.ShapeDtypeStruct((n_cols // tn, t // tm, SUBLANES, LANES), F32)],
            compiler_params=_params(("parallel", "parallel"), 40),
            name="in_proj_norms",
        )(h, w)
        nrm = jnp.sqrt(nrm[:, :, :, 0]).transpose(1, 0, 2).reshape(t // tm, n_cols // LANES)
        return out, nrm
    if transpose_out:
        assert out_scale == 1.0
        body, out_shape, scratch = _mm_t_kernel, (n_cols, t), [pltpu.VMEM((tm, tn), F32)]
        out_spec = pl.BlockSpec((tn, tm), lambda n, m: (n, m))
    else:
        body = functools.partial(_mm_kernel, out_scale=out_scale)
        out_shape, scratch = (t, n_cols), []
        out_spec = pl.BlockSpec((tm, tn), lambda n, m: (m, n))
    return pl.pallas_call(
        body,
        grid=(n_cols // tn, t // tm),
        in_specs=[pl.BlockSpec((tm, k), lambda n, m: (m, 0)),
                  pl.BlockSpec((k, tn), lambda n, m: (0, n + off))],
        out_specs=out_spec,
        out_shape=jax.ShapeDtypeStruct(out_shape, BF16),
        scratch_shapes=scratch,
        compiler_params=_params(("parallel", "parallel"), 40),
        name="in_proj_t" if transpose_out else "in_proj",
    )(h, w)


def _fft_tables(l):
    l1 = l // FFT_L2
    two_pi = 2.0 * math.pi
    k1 = jnp.arange(l1, dtype=jnp.int32)
    a1 = ((k1[:, None] * k1[None, :]) % l1).astype(F32) * (two_pi / l1)
    f1 = jnp.concatenate([jnp.cos(a1), -jnp.sin(a1)], axis=0).astype(BF16)
    k2 = jnp.arange(FFT_L2, dtype=jnp.int32)
    kk = k1[:, None, None] + l1 * k2[None, :, None]
    a2 = ((kk * k2[None, None, :]) % l).astype(F32) * (two_pi / l)
    c2, s2 = jnp.cos(a2), jnp.sin(a2)
    g2 = jnp.concatenate([jnp.concatenate([c2, s2], axis=2),
                          jnp.concatenate([-s2, c2], axis=2)], axis=1).astype(BF16)
    c = jnp.arange(FNET_GROUP_DIM, dtype=jnp.int32)
    a3 = ((c[:, None] * c[None, :]) % FNET_GROUP_DIM).astype(F32) * (two_pi / FNET_GROUP_DIM)
    scale = 1.0 / math.sqrt(l * FNET_GROUP_DIM)
    cs = (jnp.concatenate([jnp.cos(a3), jnp.sin(a3)], axis=0) * scale).astype(BF16)
    return f1, g2, cs


def _fft1_kernel(f_ref, u_ref, t_ref):
    l1 = u_ref.shape[1]
    r = jnp.dot(f_ref[...], u_ref[0], preferred_element_type=F32)
    t_ref[0, 0] = r[:l1].astype(t_ref.dtype)
    t_ref[1, 0] = r[l1:].astype(t_ref.dtype)


def _fft2_kernel(t_ref, g_ref, cs_ref, o_ref, p_scr):
    kb = g_ref.shape[0]
    l2 = FFT_L2
    gd = FNET_GROUP_DIM
    for j in range(kb):
        gm = g_ref[j]
        for g in range(FNET_GROUPS):
            cols = slice(g * gd, (g + 1) * gd)
            rhs = jnp.concatenate([t_ref[0, 0, j, :, cols], t_ref[1, 0, j, :, cols]], axis=0)
            p = jnp.dot(gm, rhs, preferred_element_type=F32)
            r0 = (j * FNET_GROUPS + g) * l2
            p_scr[r0:r0 + l2, 0:gd] = p[:l2].astype(p_scr.dtype)
            p_scr[r0:r0 + l2, gd:2 * gd] = p[l2:].astype(p_scr.dtype)
    y = jnp.dot(p_scr[...], cs_ref[...], preferred_element_type=F32)
    for j in range(kb):
        for g in range(FNET_GROUPS):
            r0 = (j * FNET_GROUPS + g) * l2
            cols = slice(j * D_FNET + g * gd, j * D_FNET + (g + 1) * gd)
            o_ref[0, :, cols] = y[r0:r0 + l2].astype(o_ref.dtype)


def _fnet_mix(u, b, l):
    l1, l2 = l // FFT_L2, FFT_L2
    f1, g2, cs = _fft_tables(l)
    wcols = l2 * D_FNET
    w = min(wcols, (2 * MIB) // (2 * l1))
    t = pl.pallas_call(
        _fft1_kernel,
        grid=(b, wcols // w),
        in_specs=[pl.BlockSpec((2 * l1, l1), lambda i, j: (0, 0)),
                  pl.BlockSpec((1, l1, w), lambda i, j: (i, 0, j))],
        out_specs=pl.BlockSpec((2, 1, l1, w), lambda i, j: (0, i, 0, j)),
        out_shape=jax.ShapeDtypeStruct((2, b, l1, wcols), BF16),
        compiler_params=_params(("parallel", "parallel"), 32),
        name="fnet_stage1",
    )(f1, u.reshape(b, l1, wcols))
    kb = min(FFT_KB, l1)
    y = pl.pallas_call(
        _fft2_kernel,
        grid=(b, l1 // kb),
        in_specs=[pl.BlockSpec((2, 1, kb, l2, D_FNET), lambda i, j: (0, i, j, 0, 0)),
                  pl.BlockSpec((kb, 2 * l2, 2 * l2), lambda i, j: (j, 0, 0)),
                  pl.BlockSpec((2 * FNET_GROUP_DIM, FNET_GROUP_DIM), lambda i, j: (0, 0))],
        out_specs=pl.BlockSpec((1, l2, kb * D_FNET), lambda i, j: (i, 0, j)),
        out_shape=jax.ShapeDtypeStruct((b, l2, l1 * D_FNET), BF16),
        scratch_shapes=[pltpu.VMEM((kb * FNET_GROUPS * l2, 2 * FNET_GROUP_DIM), BF16)],
        compiler_params=_params(("parallel", "parallel"), 48),
        name="fnet_stage2",
    )(t.reshape(2, b, l1, l2, D_FNET), g2, cs)
    return y.reshape(b * l, D_FNET)


def _row_scan(g, forward):
    n = g.shape[0] // SUBLANES
    sub = lax.broadcasted_iota(jnp.int32, (SUBLANES, g.shape[1]), 0)
    outs = [None] * n
    carry = None
    for i in (range(n) if forward else range(n - 1, -1, -1)):
        y = g[i * SUBLANES:(i + 1) * SUBLANES, :]
        for sh in (1, 2, 4):
            if forward:
                y = y + jnp.where(sub >= sh, pltpu.roll(y, sh, 0), 0.0)
            else:
                y = y + jnp.where(sub < SUBLANES - sh, pltpu.roll(y, SUBLANES - sh, 0), 0.0)
        if carry is not None:
            y = y + carry
        edge = y[SUBLANES - 1:SUBLANES, :] if forward else y[0:1, :]
        carry = jnp.broadcast_to(edge, y.shape)
        outs[i] = y
    return jnp.concatenate(outs, axis=0)


def _hgrn_kernel(q_ref, i_ref, ff_ref, fb_ref, gate_ref, lbl_ref, gn_ref, o_ref, acc_ref):
    hb = HGRN_BLOCK
    half = hb // 2
    n_blocks = q_ref.shape[0] // hb
    group = min(HGRN_GROUP, n_blocks)
    dk = HGRN_HEAD_DIM
    row = lax.broadcasted_iota(jnp.int32, (hb, hb), 0)
    col = lax.broadcasted_iota(jnp.int32, (hb, hb), 1)

    logits = lbl_ref[...]
    mx = jnp.max(logits, axis=1, keepdims=True)
    ex = jnp.exp(logits - mx)
    lb_all = ex[:, 0, :] / jnp.sum(ex, axis=1)
    gn = gn_ref[...]

    def direction(f_ref, lb, forward):
        keep = (col <= row) if forward else (col >= row)

        def body(step, state_t):
            gi = step if forward else n_blocks // group - 1 - step
            base = gi * (group * hb)
            units = []
            for u in range(group):
                rows = pl.ds(pl.multiple_of(base + u * hb, hb), hb)
                q = _silu(q_ref[rows, :].astype(F32))
                v = i_ref[rows, :]
                f = lb + (1.0 - lb) * jax.nn.sigmoid(f_ref[rows, :].astype(F32))
                k = 1.0 - f
                a = _row_scan(jnp.log2(f), forward)
                ref = a[half - 1:half, :] if forward else a[half:half + 1, :]
                end = a[hb - 1:hb, :] if forward else a[0:1, :]
                qt = q * jnp.exp2(a - ref)
                kt = k * jnp.exp2(ref - a)
                scores = jnp.where(keep, _dot_nt(qt.astype(BF16), kt.astype(BF16)), 0.0)
                o_intra = jnp.dot(scores.astype(BF16), v, preferred_element_type=F32)
                q_in = (qt * jnp.exp2(ref)).astype(BF16)
                k_end = (kt * jnp.exp2(end - ref)).astype(BF16)
                kv_t = lax.dot_general(v, k_end, (((0,), (0,)), ((), ())),
                                       preferred_element_type=F32)
                units.append((rows, o_intra, q_in, kv_t, jnp.exp2(end)))
            for rows, o_intra, q_in, kv_t, decay in (units if forward else units[::-1]):
                o = o_intra + _dot_nt(q_in, state_t.astype(BF16))
                state_t = state_t * decay + kv_t
                if forward:
                    acc_ref[rows, :] = o
                else:
                    tot = acc_ref[rows, :] + o
                    inv = lax.rsqrt(jnp.mean(tot * tot, axis=-1, keepdims=True) + RMS_EPS)
                    gate = gate_ref[rows, :].astype(F32)
                    o_ref[rows, :] = (tot * inv * gn * _silu(gate)).astype(o_ref.dtype)
            return state_t

        lax.fori_loop(0, n_blocks // group, body, jnp.zeros((dk, dk), F32))

    direction(ff_ref, lb_all[0:1, :], True)
    direction(fb_ref, lb_all[1:2, :], False)


def _hgrn_mix_gated(hg, lb_logits, g_norm, b, l):
    nh, dk = HGRN_HEADS, HGRN_HEAD_DIM

    def col(block):
        return pl.BlockSpec((l, dk), lambda i, h: (i, block * nh + h))

    return pl.pallas_call(
        _hgrn_kernel,
        grid=(b, nh),
        in_specs=[col(0), col(1), col(2), col(3), col(4),
                  pl.BlockSpec((2, lb_logits.shape[1], dk), lambda i, h: (0, 0, h)),
                  pl.BlockSpec((1, dk), lambda i, h: (0, 0))],
        out_specs=pl.BlockSpec((l, dk), lambda i, h: (i, h)),
        out_shape=jax.ShapeDtypeStruct((b * l, D_HGRN), BF16),
        scratch_shapes=[pltpu.VMEM((l, dk), F32)],
        compiler_params=_params(("parallel", "parallel"), 48),
        name="hgrn2",
    )(hg, hg, hg, hg, hg, lb_logits.astype(F32), g_norm.reshape(1, dk).astype(F32))


ATTN_TQ = 256
ATTN_TK = 512
ATTN_Q_TILES = 8
ATTN_QK_SCALE = math.sqrt(DIFF_HEAD_DIM ** -0.5 * LOG2E)
ATTN_UNROLL_PAIRS = 5
ATTN_SKIP_MARGIN = 140.0


def _attn_kernel(slopes_ref, nrm_ref, q_ref, k_ref, vt_ref, gate_ref, lq1_ref, lk1_ref, lq2_ref,
                 lk2_ref, subln_ref, o_ref, qbd_ref, d0_ref, ta_ref, tb_ref, m_ref, l_ref, acc_ref,
                 steps_ref, *, lambda_init):
    tq, tk, nqt = ATTN_TQ, ATTN_TK, ATTN_Q_TILES
    n_blocks = k_ref.shape[0] // tk
    n_steps = n_blocks * nqt
    dh = DIFF_HEAD_DIM
    h = pl.program_id(1)
    sup = pl.program_id(2)
    slope2 = slopes_ref[h] * LOG2E
    kk = lax.broadcasted_iota(jnp.int32, (tk, tq), 0)
    qq = lax.broadcasted_iota(jnp.int32, (tk, tq), 1)
    d0 = (kk - qq).astype(F32) * slope2
    d0_ref[0] = d0
    d0_ref[1] = -d0
    for r in range(tk // tq):
        d0_ref[2 + r] = -jnp.abs(d0 - (r * tq) * slope2)

    zeros = jnp.zeros((tq, dh), BF16)
    for t in range(nqt):
        rows = slice(t * tq, (t + 1) * tq)
        qbd_ref[t, 0:tq, 0:dh] = q_ref[rows, 0:dh]
        qbd_ref[t, 0:tq, dh:2 * dh] = zeros
        qbd_ref[t, tq:2 * tq, 0:dh] = zeros
        qbd_ref[t, tq:2 * tq, dh:2 * dh] = q_ref[rows, dh:2 * dh]

    m_ref[...] = jnp.full(m_ref.shape, -1e30, F32)
    l_ref[...] = jnp.zeros(l_ref.shape, F32)
    acc_ref[...] = jnp.zeros(acc_ref.shape, F32)

    seq_tiles = k_ref.shape[0] // NORM_ROWS
    groups = 2 * D_ATTN // LANES
    tile0 = pl.program_id(0) * seq_tiles
    qk_bound = None
    for c in range(2):
        qn = None
        for r in range(nqt * tq // NORM_ROWS):
            v = nrm_ref[(tile0 + sup * (nqt * tq // NORM_ROWS) + r) * groups + 2 * h + c]
            qn = v if qn is None else jnp.maximum(qn, v)
        kn = None
        for r in range(seq_tiles):
            v = nrm_ref[(tile0 + r) * groups + groups // 2 + 2 * h + c]
            kn = v if kn is None else jnp.maximum(kn, v)
        qk_bound = qn * kn if qk_bound is None else jnp.maximum(qk_bound, qn * kn)
    skip_thr = ATTN_SKIP_MARGIN + 2.03125 * qk_bound

    def offsets(s):
        t = s % nqt
        rel = (sup * nqt + t) * tq - (s // nqt) * tk
        return t, rel, rel >= tk, rel <= -tq

    def list_step(s, n):
        t, rel, before, after = offsets(s)
        gap = jnp.where(before, rel - tk + 1, jnp.where(after, -rel - tq + 1, 0))
        steps_ref[n] = s
        return n + (gap.astype(F32) * slope2 < skip_thr).astype(jnp.int32)

    n_listed = lax.fori_loop(0, n_steps, list_step, 0)
    n_pairs = (n_listed + 1) // 2
    n_chunks = n_pairs // ATTN_UNROLL_PAIRS
    for extra in range(2):
        steps_ref[n_listed + extra] = n_steps

    def split(i):
        code = steps_ref[i]
        filler = code >= n_steps
        s = jnp.where(filler, 0, code)
        k0 = pl.multiple_of((s // nqt) * tk, tk)
        t, rel, before, after = offsets(s)
        idx = jnp.where(before, 0, jnp.where(after, 1, 2 + rel // tq))
        lin = rel.astype(F32) * slope2
        const = jnp.where(before, -lin, jnp.where(after, lin, 0.0))
        return k0, t, idx, jnp.where(filler, -3e38, const)

    def scores(s, t_ref):
        k0, t, idx, _ = split(s)
        bias = d0_ref[idx]
        sc = _dot_nt(k_ref[pl.ds(k0, tk), :], qbd_ref[t])
        t_ref[:, 0:tq] = sc[:, 0:tq] + bias
        t_ref[:, tq:2 * tq] = sc[:, tq:2 * tq] + bias

    def accumulate(s, t_ref):
        k0, t, _, const = split(s)
        sc = t_ref[...]
        m_old = m_ref[t]
        m_new = jnp.maximum(m_old, jnp.max(sc, axis=0, keepdims=True) + const)
        alpha = jnp.exp2(m_old - m_new)
        p = jnp.exp2(sc - (m_new - const))
        l_ref[t] = alpha * l_ref[t] + jnp.sum(p, axis=0, keepdims=True)
        acc_ref[t] = alpha * acc_ref[t] + jnp.dot(vt_ref[:, pl.ds(k0, tk)], p.astype(BF16),
                                                  preferred_element_type=F32)
        m_ref[t] = m_new

    scores(0, ta_ref)

    def pair(i):
        scores(2 * i + 1, tb_ref)
        accumulate(2 * i, ta_ref)
        scores(2 * i + 2, ta_ref)
        accumulate(2 * i + 1, tb_ref)

    def body(c, carry):
        for u in range(ATTN_UNROLL_PAIRS):
            pair(c * ATTN_UNROLL_PAIRS + u)
        return carry

    def single(i, carry):
        pair(i)
        return carry

    lax.fori_loop(0, n_chunks, body, 0)
    lax.fori_loop(n_chunks * ATTN_UNROLL_PAIRS, n_pairs, single, 0)

    lam = (jnp.exp(jnp.sum(lq1_ref[...] * lk1_ref[...], keepdims=True))
           - jnp.exp(jnp.sum(lq2_ref[...] * lk2_ref[...], keepdims=True)) + lambda_init)
    gain = subln_ref[...] * (1.0 - lambda_init)
    for t in range(nqt):
        rows = slice(t * tq, (t + 1) * tq)
        r = 1.0 / l_ref[t]
        o = (acc_ref[t, :, 0:tq] * r[:, 0:tq]
             - acc_ref[t, :, tq:2 * tq] * (lam * r[:, tq:2 * tq])).T
        inv = lax.rsqrt(jnp.mean(o * o, axis=-1, keepdims=True) + RMS_EPS)
        o_ref[rows, :] = (o * inv * (gain * _silu(gate_ref[rows, :].astype(F32)))
                          ).astype(o_ref.dtype)


def _diff_attention_gated(qk, qk_norms, v_t, gate, lq1, lk1, lq2, lk2, subln, lambda_init, b, l):
    nh, dv = DIFF_HEADS, DIFF_V_DIM
    tq, tk, nqt = ATTN_TQ, ATTN_TK, ATTN_Q_TILES
    qs = tq * nqt
    ns = l // qs
    assert l % qs == 0 and qs % NORM_ROWS == 0 and l % NORM_ROWS == 0
    slopes = jnp.exp2(-8.0 * (jnp.arange(nh, dtype=F32) + 1.0) / nh)
    vec = lambda a: a.reshape(1, -1).astype(F32)
    small = lambda n: pl.BlockSpec((1, n), lambda i, h, s, *_: (0, 0))
    grid_spec = pltpu.PrefetchScalarGridSpec(
        num_scalar_prefetch=2,
        grid=(b, nh, ns),
        in_specs=[pl.BlockSpec((qs, dv), lambda i, h, s, *_: (i * ns + s, h)),
                  pl.BlockSpec((l, dv), lambda i, h, s, *_: (i, nh + h)),
                  pl.BlockSpec((dv, l), lambda i, h, s, *_: (h, i)),
                  pl.BlockSpec((qs, dv), lambda i, h, s, *_: (i * ns + s, h)),
                  small(DIFF_HEAD_DIM), small(DIFF_HEAD_DIM), small(DIFF_HEAD_DIM),
                  small(DIFF_HEAD_DIM), small(dv)],
        out_specs=pl.BlockSpec((qs, dv), lambda i, h, s, *_: (i * ns + s, h)),
        scratch_shapes=[pltpu.VMEM((nqt, 2 * tq, 2 * DIFF_HEAD_DIM), BF16),
                        pltpu.VMEM((2 + tk // tq, tk, tq), F32),
                        pltpu.VMEM((tk, 2 * tq), F32),
                        pltpu.VMEM((tk, 2 * tq), F32),
                        pltpu.VMEM((nqt, 1, 2 * tq), F32), pltpu.VMEM((nqt, 1, 2 * tq), F32),
                        pltpu.VMEM((nqt, dv, 2 * tq), F32),
                        pltpu.SMEM((l // tk * nqt + 8,), jnp.int32)],
    )
    return pl.pallas_call(
        functools.partial(_attn_kernel, lambda_init=lambda_init),
        grid_spec=grid_spec,
        out_shape=jax.ShapeDtypeStruct((b * l, D_ATTN), BF16),
        compiler_params=_params(("parallel", "parallel", "parallel"), 48),
        name="diff_attention",
    )(slopes, qk_norms.reshape(-1), qk, qk, v_t, gate, vec(lq1), vec(lk1), vec(lq2), vec(lk2), vec(subln))


def _out_kernel(*refs, n_act, gated, with_next):
    refs = list(refs)
    act_refs = [refs.pop(0) for _ in range(n_act)]
    gate_ref = refs.pop(0) if gated else None
    w_refs = [refs.pop(0) for _ in range(n_act)]
    x_ref, g_ref = refs.pop(0), refs.pop(0)
    gn_ref = refs.pop(0) if with_next else None
    o_ref = refs.pop(0)
    h_ref = refs.pop(0) if with_next else None
    tm = x_ref.shape[0]
    sub = tm // OUT_ROW_SPLIT
    for part in range(OUT_ROW_SPLIT):
        rows = slice(part * sub, (part + 1) * sub)
        acts = [a_ref[rows, :] for a_ref in act_refs]
        if gated:
            acts[0] = (acts[0].astype(F32) * _silu(gate_ref[rows, :].astype(F32))).astype(BF16)
        y = jnp.dot(acts[0], w_refs[0][...], preferred_element_type=F32)
        for a, w_ref in zip(acts[1:], w_refs[1:]):
            y = y + jnp.dot(a, w_ref[...], preferred_element_type=F32)
        inv = lax.rsqrt(jnp.mean(y * y, axis=-1, keepdims=True) + RMS_EPS)
        out = x_ref[rows, :] + y * inv * g_ref[...]
        o_ref[rows, :] = out
        if with_next:
            inv_n = lax.rsqrt(jnp.mean(out * out, axis=-1, keepdims=True) + RMS_EPS)
            h_ref[rows, :] = (out * inv_n * gn_ref[...]).astype(BF16)


def _out_proj_residual(acts, ws, x, g, gate=None, g_next=None, tm=512):
    t, d = x.shape
    n_act = len(acts)
    row = lambda n: pl.BlockSpec((tm, n), lambda i: (i, 0))
    vec = lambda: pl.BlockSpec((1, d), lambda i: (0, 0))
    args = list(acts)
    in_specs = [row(a.shape[1]) for a in acts]
    if gate is not None:
        args.append(gate)
        in_specs.append(row(gate.shape[1]))
    args += list(ws) + [x, g.reshape(1, d).astype(F32)]
    in_specs += [pl.BlockSpec(w.shape, lambda i: (0, 0)) for w in ws] + [row(d), vec()]
    out_specs, out_shape = row(d), jax.ShapeDtypeStruct((t, d), F32)
    if g_next is not None:
        args.append(g_next.reshape(1, d).astype(F32))
        in_specs.append(vec())
        out_specs = [out_specs, row(d)]
        out_shape = [out_shape, jax.ShapeDtypeStruct((t, d), BF16)]
    return pl.pallas_call(
        functools.partial(_out_kernel, n_act=n_act, gated=gate is not None,
                          with_next=g_next is not None),
        grid=(t // tm,),
        in_specs=in_specs,
        out_specs=out_specs,
        out_shape=out_shape,
        compiler_params=_params(("parallel",), 48),
        name="out_proj_residual",
    )(*args)


def _trunk(x3, wts):
    b, l, d = x3.shape
    x = x3.reshape(b * l, d)

    h = _norm(x, wts["ev_norm_pre"])
    w_in = wts["ev_w_in"]
    u = _matmul_cols(h, w_in, 0, D_FNET)
    gate_a = _matmul_cols(h, w_in, D_FNET, D_FNET)
    hg = _matmul_cols(h, w_in, 2 * D_FNET, 5 * D_HGRN)
    y_a = _fnet_mix(u, b, l)
    y_b = _hgrn_mix_gated(hg, wts["hgrn_lb_logits"], wts["hgrn_norm"], b, l)
    w_out = wts["ev_w_out"]
    x, h = _out_proj_residual([y_a, y_b], [w_out[:D_FNET], w_out[D_FNET:]], x, wts["ev_norm_post"],
                              gate=gate_a, g_next=wts["od_norm_pre"])

    lambda_init = 0.8 - 0.6 * math.exp(-0.3 * 1)
    w_in = wts["od_w_in"]
    qk, qk_norms = _matmul_cols(h, w_in, 0, 2 * D_ATTN, out_scale=ATTN_QK_SCALE, row_norms=True,
                                tm=NORM_ROWS)
    v_t = _matmul_cols(h, w_in, 2 * D_ATTN, D_ATTN, transpose_out=True)
    gate = _matmul_cols(h, w_in, 3 * D_ATTN, D_ATTN)
    o = _diff_attention_gated(qk, qk_norms, v_t, gate, wts["lambda_q1"], wts["lambda_k1"], wts["lambda_q2"],
                              wts["lambda_k2"], wts["subln"], lambda_init, b, l)
    x = _out_proj_residual([o], [wts["od_w_out"]], x, wts["od_norm_post"])
    return x.reshape(b, l, d)


def kernel(x_prompt, x_sample, ev_w_in, ev_w_out, ev_norm_pre, ev_norm_post, hgrn_lb_logits,
           hgrn_norm, od_w_in, od_w_out, od_norm_pre, od_norm_post,
           lambda_q1, lambda_k1, lambda_q2, lambda_k2, subln):
    wts = {
        "ev_w_in": ev_w_in[0].astype(BF16), "ev_w_out": ev_w_out[0].astype(BF16),
        "ev_norm_pre": ev_norm_pre[0], "ev_norm_post": ev_norm_post[0],
        "hgrn_lb_logits": hgrn_lb_logits, "hgrn_norm": hgrn_norm[0],
        "od_w_in": od_w_in[0].astype(BF16), "od_w_out": od_w_out[0].astype(BF16),
        "od_norm_pre": od_norm_pre[0], "od_norm_post": od_norm_post[0],
        "lambda_q1": lambda_q1[0], "lambda_k1": lambda_k1[0],
        "lambda_q2": lambda_q2[0], "lambda_k2": lambda_k2[0], "subln": subln[0],
    }
    return (_trunk(x_prompt, wts), _trunk(x_sample, wts))
```

```python
import functools
import math

import jax
import jax.numpy as jnp
from jax import lax
from jax.experimental import pallas as pl
from jax.experimental.pallas import tpu as pltpu

F32 = jnp.float32
BF16 = jnp.bfloat16

D_MODEL = 2048
D_FNET = 1024
FNET_GROUP_DIM = 256
FNET_GROUPS = D_FNET // FNET_GROUP_DIM
D_HGRN = 1024
HGRN_HEAD_DIM = 128
HGRN_HEADS = D_HGRN // HGRN_HEAD_DIM
DIFF_HEADS = 8
DIFF_HEAD_DIM = 128
DIFF_V_DIM = 256
D_ATTN = DIFF_HEADS * DIFF_V_DIM
RMS_EPS = 1e-6
LOG2E = 1.4426950408889634

SUBLANES = 8
LANES = 128
NORM_ROWS = 1024
FFT_L2 = 128
FFT_KB = 8
HGRN_BLOCK = 128
HGRN_GROUP = 16
OUT_ROW_SPLIT = 2
MIB = 1024 * 1024


def _params(semantics, vmem_mib):
    return pltpu.CompilerParams(dimension_semantics=semantics, vmem_limit_bytes=vmem_mib * MIB)


def _silu(x):
    return x * jax.nn.sigmoid(x)


def _dot_nt(a, b):
    return lax.dot_general(a, b, (((1,), (1,)), ((), ())), preferred_element_type=F32)


def _mm_kernel(h_ref, w_ref, o_ref, *, out_scale):
    r = jnp.dot(h_ref[...], w_ref[...], preferred_element_type=F32)
    if out_scale != 1.0:
        r = r * out_scale
    o_ref[...] = r.astype(o_ref.dtype)


def _mm_prenorm_kernel(x_ref, g_ref, w_ref, o_ref):
    x = x_ref[...]
    inv = lax.rsqrt(jnp.mean(x * x, axis=-1, keepdims=True) + RMS_EPS)
    r = jnp.dot((x * g_ref[...]).astype(BF16), w_ref[...], preferred_element_type=F32)
    o_ref[...] = (r * inv).astype(o_ref.dtype)


def _mm_norm_kernel(h_ref, w_ref, o_ref, n_ref, *, out_scale):
    r = jnp.dot(h_ref[...], w_ref[...], preferred_element_type=F32) * out_scale
    o = r.astype(o_ref.dtype)
    o_ref[...] = o
    of = o.astype(F32)
    sq = of * of
    for g in range(sq.shape[1] // LANES):
        ss = jnp.sum(sq[:, g * LANES:(g + 1) * LANES], axis=1, keepdims=True)
        n_ref[0, 0, g:g + 1, :] = jnp.broadcast_to(jnp.max(ss, axis=0, keepdims=True), (1, LANES))


def _mm_t_kernel(h_ref, w_ref, o_ref, r_ref):
    r_ref[...] = jnp.dot(h_ref[...], w_ref[...], preferred_element_type=F32)
    o_ref[...] = r_ref[...].T.astype(o_ref.dtype)


def _matmul_cols(h, w, col_off, n_cols, transpose_out=False, out_scale=1.0, row_norms=False,
                 prenorm_gain=None, tm=1024, tn=1024):
    t, k = h.shape
    tm = min(tm, t)
    off = col_off // tn
    if prenorm_gain is not None:
        assert not transpose_out and not row_norms and out_scale == 1.0
        return pl.pallas_call(
            _mm_prenorm_kernel,
            grid=(n_cols // tn, t // tm),
            in_specs=[pl.BlockSpec((tm, k), lambda n, m: (m, 0)),
                      pl.BlockSpec((1, k), lambda n, m: (0, 0)),
                      pl.BlockSpec((k, tn), lambda n, m: (0, n + off))],
            out_specs=pl.BlockSpec((tm, tn), lambda n, m: (m, n)),
            out_shape=jax.ShapeDtypeStruct((t, n_cols), BF16),
            compiler_params=_params(("parallel", "parallel"), 48),
            name="in_proj_prenorm",
        )(h, prenorm_gain.reshape(1, k).astype(F32), w)
    if row_norms:
        assert not transpose_out and tn // LANES == SUBLANES
        out, nrm = pl.pallas_call(
            functools.partial(_mm_norm_kernel, out_scale=out_scale),
            grid=(n_cols // tn, t // tm),
            in_specs=[pl.BlockSpec((tm, k), lambda n, m: (m, 0)),
                      pl.BlockSpec((k, tn), lambda n, m: (0, n + off))],
            out_specs=[pl.BlockSpec((tm, tn), lambda n, m: (m, n)),
                       pl.BlockSpec((1, 1, SUBLANES, LANES), lambda n, m: (n, m, 0, 0))],
            out_shape=[jax.ShapeDtypeStruct((t, n_cols), BF16),
                       jax.ShapeDtypeStruct((n_cols // tn, t // tm, SUBLANES, LANES), F32)],
            compiler_params=_params(("parallel", "parallel"), 40),
            name="in_proj_norms",
        )(h, w)
        nrm = jnp.sqrt(nrm[:, :, :, 0]).transpose(1, 0, 2).reshape(t // tm, n_cols // LANES)
        return out, nrm
    if transpose_out:
        assert out_scale == 1.0
        body, out_shape, scratch = _mm_t_kernel, (n_cols, t), [pltpu.VMEM((tm, tn), F32)]
        out_spec = pl.BlockSpec((tn, tm), lambda n, m: (n, m))
    else:
        body = functools.partial(_mm_kernel, out_scale=out_scale)
        out_shape, scratch = (t, n_cols), []
        out_spec = pl.BlockSpec((tm, tn), lambda n, m: (m, n))
    return pl.pallas_call(
        body,
        grid=(n_cols // tn, t // tm),
        in_specs=[pl.BlockSpec((tm, k), lambda n, m: (m, 0)),
                  pl.BlockSpec((k, tn), lambda n, m: (0, n + off))],
        out_specs=out_spec,
        out_shape=jax.ShapeDtypeStruct(out_shape, BF16),
        scratch_shapes=scratch,
        compiler_params=_params(("parallel", "parallel"), 40),
        name="in_proj_t" if transpose_out else "in_proj",
    )(h, w)


def _fft_tables(l):
    l1 = l // FFT_L2
    two_pi = 2.0 * math.pi
    k1 = jnp.arange(l1, dtype=jnp.int32)
    a1 = ((k1[:, None] * k1[None, :]) % l1).astype(F32) * (two_pi / l1)
    f1 = jnp.concatenate([jnp.cos(a1), -jnp.sin(a1)], axis=0).astype(BF16)
    k2 = jnp.arange(FFT_L2, dtype=jnp.int32)
    kk = k1[:, None, None] + l1 * k2[None, :, None]
    a2 = ((kk * k2[None, None, :]) % l).astype(F32) * (two_pi / l)
    c2, s2 = jnp.cos(a2), jnp.sin(a2)
    g2 = jnp.concatenate([jnp.concatenate([c2, s2], axis=2),
                          jnp.concatenate([-s2, c2], axis=2)], axis=1).astype(BF16)
    c = jnp.arange(FNET_GROUP_DIM, dtype=jnp.int32)
    a3 = ((c[:, None] * c[None, :]) % FNET_GROUP_DIM).astype(F32) * (two_pi / FNET_GROUP_DIM)
    scale = 1.0 / math.sqrt(l * FNET_GROUP_DIM)
    cs = (jnp.concatenate([jnp.cos(a3), jnp.sin(a3)], axis=0) * scale).astype(BF16)
    return f1, g2, cs


def _fft1_kernel(f_ref, u_ref, t_ref):
    l1 = u_ref.shape[1]
    r = jnp.dot(f_ref[...], u_ref[0], preferred_element_type=F32)
    t_ref[0, 0] = r[:l1].astype(t_ref.dtype)
    t_ref[1, 0] = r[l1:].astype(t_ref.dtype)


def _fft2_kernel(t_ref, g_ref, cs_ref, o_ref, p_scr):
    kb = g_ref.shape[0]
    l2 = FFT_L2
    gd = FNET_GROUP_DIM
    for j in range(kb):
        gm = g_ref[j]
        for g in range(FNET_GROUPS):
            cols = slice(g * gd, (g + 1) * gd)
            rhs = jnp.concatenate([t_ref[0, 0, j, :, cols], t_ref[1, 0, j, :, cols]], axis=0)
            p = jnp.dot(gm, rhs, preferred_element_type=F32)
            r0 = (j * FNET_GROUPS + g) * l2
            p_scr[r0:r0 + l2, 0:gd] = p[:l2].astype(p_scr.dtype)
            p_scr[r0:r0 + l2, gd:2 * gd] = p[l2:].astype(p_scr.dtype)
    y = jnp.dot(p_scr[...], cs_ref[...], preferred_element_type=F32)
    for j in range(kb):
        for g in range(FNET_GROUPS):
            r0 = (j * FNET_GROUPS + g) * l2
            cols = slice(j * D_FNET + g * gd, j * D_FNET + (g + 1) * gd)
            o_ref[0, :, cols] = y[r0:r0 + l2].astype(o_ref.dtype)


def _fnet_mix(u, b, l):
    l1, l2 = l // FFT_L2, FFT_L2
    f1, g2, cs = _fft_tables(l)
    wcols = l2 * D_FNET
    w = min(wcols, (2 * MIB) // (2 * l1))
    t = pl.pallas_call(
        _fft1_kernel,
        grid=(b, wcols // w),
        in_specs=[pl.BlockSpec((2 * l1, l1), lambda i, j: (0, 0)),
                  pl.BlockSpec((1, l1, w), lambda i, j: (i, 0, j))],
        out_specs=pl.BlockSpec((2, 1, l1, w), lambda i, j: (0, i, 0, j)),
        out_shape=jax.ShapeDtypeStruct((2, b, l1, wcols), BF16),
        compiler_params=_params(("parallel", "parallel"), 32),
        name="fnet_stage1",
    )(f1, u.reshape(b, l1, wcols))
    kb = min(FFT_KB, l1)
    y = pl.pallas_call(
        _fft2_kernel,
        grid=(b, l1 // kb),
        in_specs=[pl.BlockSpec((2, 1, kb, l2, D_FNET), lambda i, j: (0, i, j, 0, 0)),
                  pl.BlockSpec((kb, 2 * l2, 2 * l2), lambda i, j: (j, 0, 0)),
                  pl.BlockSpec((2 * FNET_GROUP_DIM, FNET_GROUP_DIM), lambda i, j: (0, 0))],
        out_specs=pl.BlockSpec((1, l2, kb * D_FNET), lambda i, j: (i, 0, j)),
        out_shape=jax.ShapeDtypeStruct((b, l2, l1 * D_FNET), BF16),
        scratch_shapes=[pltpu.VMEM((kb * FNET_GROUPS * l2, 2 * FNET_GROUP_DIM), BF16)],
        compiler_params=_params(("parallel", "parallel"), 48),
        name="fnet_stage2",
    )(t.reshape(2, b, l1, l2, D_FNET), g2, cs)
    return y.reshape(b * l, D_FNET)


def _row_scan(g, forward):
    n = g.shape[0] // SUBLANES
    sub = lax.broadcasted_iota(jnp.int32, (SUBLANES, g.shape[1]), 0)
    outs = [None] * n
    carry = None
    for i in (range(n) if forward else range(n - 1, -1, -1)):
        y = g[i * SUBLANES:(i + 1) * SUBLANES, :]
        for sh in (1, 2, 4):
            if forward:
                y = y + jnp.where(sub >= sh, pltpu.roll(y, sh, 0), 0.0)
            else:
                y = y + jnp.where(sub < SUBLANES - sh, pltpu.roll(y, SUBLANES - sh, 0), 0.0)
        if carry is not None:
            y = y + carry
        edge = y[SUBLANES - 1:SUBLANES, :] if forward else y[0:1, :]
        carry = jnp.broadcast_to(edge, y.shape)
        outs[i] = y
    return jnp.concatenate(outs, axis=0)


def _hgrn_kernel(q_ref, i_ref, ff_ref, fb_ref, gate_ref, lbl_ref, gn_ref, o_ref, acc_ref, qs_ref):
    hb = HGRN_BLOCK
    half = hb // 2
    n_blocks = q_ref.shape[0] // hb
    group = min(HGRN_GROUP, n_blocks)
    dk = HGRN_HEAD_DIM
    row = lax.broadcasted_iota(jnp.int32, (hb, hb), 0)
    col = lax.broadcasted_iota(jnp.int32, (hb, hb), 1)

    logits = lbl_ref[...]
    mx = jnp.max(logits, axis=1, keepdims=True)
    ex = jnp.exp(logits - mx)
    lb_all = ex[:, 0, :] / jnp.sum(ex, axis=1)
    gn = gn_ref[...]

    def direction(f_ref, lb, forward):
        keep = (col <= row) if forward else (col >= row)

        def body(step, state_t):
            gi = step if forward else n_blocks // group - 1 - step
            base = gi * (group * hb)
            units = []
            for u in range(group):
                rows = pl.ds(pl.multiple_of(base + u * hb, hb), hb)
                if forward:
                    q = _silu(q_ref[rows, :].astype(F32))
                    qs_ref[rows, :] = q
                else:
                    q = qs_ref[rows, :]
                v = i_ref[rows, :]
                f = lb + (1.0 - lb) * jax.nn.sigmoid(f_ref[rows, :].astype(F32))
                k = 1.0 - f
                a = _row_scan(jnp.log2(f), forward)
                ref = a[half - 1:half, :] if forward else a[half:half + 1, :]
                end = a[hb - 1:hb, :] if forward else a[0:1, :]
                qt = q * jnp.exp2(a - ref)
                kt = k * jnp.exp2(ref - a)
                scores = jnp.where(keep, _dot_nt(qt.astype(BF16), kt.astype(BF16)), 0.0)
                o_intra = jnp.dot(scores.astype(BF16), v, preferred_element_type=F32)
                q_in = (qt * jnp.exp2(ref)).astype(BF16)
                k_end = (kt * jnp.exp2(end - ref)).astype(BF16)
                kv_t = lax.dot_general(v, k_end, (((0,), (0,)), ((), ())),
                                       preferred_element_type=F32)
                units.append((rows, o_intra, q_in, kv_t, jnp.exp2(end)))
            for rows, o_intra, q_in, kv_t, decay in (units if forward else units[::-1]):
                o = o_intra + _dot_nt(q_in, state_t.astype(BF16))
                state_t = state_t * decay + kv_t
                if forward:
                    acc_ref[rows, :] = o
                else:
                    tot = acc_ref[rows, :] + o
                    inv = lax.rsqrt(jnp.mean(tot * tot, axis=-1, keepdims=True) + RMS_EPS)
                    gate = gate_ref[rows, :].astype(F32)
                    o_ref[rows, :] = (tot * inv * gn * _silu(gate)).astype(o_ref.dtype)
            return state_t

        lax.fori_loop(0, n_blocks // group, body, jnp.zeros((dk, dk), F32))

    direction(ff_ref, lb_all[0:1, :], True)
    direction(fb_ref, lb_all[1:2, :], False)


def _hgrn_mix_gated(hg, lb_logits, g_norm, b, l):
    nh, dk = HGRN_HEADS, HGRN_HEAD_DIM

    def col(block):
        return pl.BlockSpec((l, dk), lambda i, h: (i, block * nh + h))

    return pl.pallas_call(
        _hgrn_kernel,
        grid=(b, nh),
        in_specs=[col(0), col(1), col(2), col(3), col(4),
                  pl.BlockSpec((2, lb_logits.shape[1], dk), lambda i, h: (0, 0, h)),
                  pl.BlockSpec((1, dk), lambda i, h: (0, 0))],
        out_specs=pl.BlockSpec((l, dk), lambda i, h: (i, h)),
        out_shape=jax.ShapeDtypeStruct((b * l, D_HGRN), BF16),
        scratch_shapes=[pltpu.VMEM((l, dk), F32), pltpu.VMEM((l, dk), F32)],
        compiler_params=_params(("parallel", "parallel"), 48),
        name="hgrn2",
    )(hg, hg, hg, hg, hg, lb_logits.astype(F32), g_norm.reshape(1, dk).astype(F32))


ATTN_TQ = 256
ATTN_TK = 512
ATTN_Q_TILES = 8
ATTN_QK_SCALE = math.sqrt(DIFF_HEAD_DIM ** -0.5 * LOG2E)
ATTN_UNROLL_PAIRS = 5
ATTN_SKIP_MARGIN = 140.0


def _attn_kernel(slopes_ref, nrm_ref, q_ref, k_ref, vt_ref, gate_ref, lq1_ref, lk1_ref, lq2_ref,
                 lk2_ref, subln_ref, o_ref, qbd_ref, d0_ref, ta_ref, tb_ref, m_ref, l_ref, acc_ref,
                 steps_ref, *, lambda_init):
    tq, tk, nqt = ATTN_TQ, ATTN_TK, ATTN_Q_TILES
    n_blocks = k_ref.shape[0] // tk
    n_steps = n_blocks * nqt
    dh = DIFF_HEAD_DIM
    h = pl.program_id(1)
    sup = pl.program_id(2)
    slope2 = slopes_ref[h] * LOG2E
    kk = lax.broadcasted_iota(jnp.int32, (tk, tq), 0)
    qq = lax.broadcasted_iota(jnp.int32, (tk, tq), 1)
    d0 = (kk - qq).astype(F32) * slope2
    d0_ref[0] = d0
    d0_ref[1] = -d0
    for r in range(tk // tq):
        d0_ref[2 + r] = -jnp.abs(d0 - (r * tq) * slope2)

    zeros = jnp.zeros((tq, dh), BF16)
    for t in range(nqt):
        rows = slice(t * tq, (t + 1) * tq)
        qbd_ref[t, 0:tq, 0:dh] = q_ref[rows, 0:dh]
        qbd_ref[t, 0:tq, dh:2 * dh] = zeros
        qbd_ref[t, tq:2 * tq, 0:dh] = zeros
        qbd_ref[t, tq:2 * tq, dh:2 * dh] = q_ref[rows, dh:2 * dh]

    m_ref[...] = jnp.full(m_ref.shape, -1e30, F32)
    l_ref[...] = jnp.zeros(l_ref.shape, F32)
    acc_ref[...] = jnp.zeros(acc_ref.shape, F32)

    seq_tiles = k_ref.shape[0] // NORM_ROWS
    groups = 2 * D_ATTN // LANES
    tile0 = pl.program_id(0) * seq_tiles
    qk_bound = None
    for c in range(2):
        qn = None
        for r in range(nqt * tq // NORM_ROWS):
            v = nrm_ref[(tile0 + sup * (nqt * tq // NORM_ROWS) + r) * groups + 2 * h + c]
            qn = v if qn is None else jnp.maximum(qn, v)
        kn = None
        for r in range(seq_tiles):
            v = nrm_ref[(tile0 + r) * groups + groups // 2 + 2 * h + c]
            kn = v if kn is None else jnp.maximum(kn, v)
        qk_bound = qn * kn if qk_bound is None else jnp.maximum(qk_bound, qn * kn)
    skip_thr = ATTN_SKIP_MARGIN + 2.03125 * qk_bound

    def offsets(s):
        t = s % nqt
        rel = (sup * nqt + t) * tq - (s // nqt) * tk
        return t, rel, rel >= tk, rel <= -tq

    def list_step(s, n):
        t, rel, before, after = offsets(s)
        gap = jnp.where(before, rel - tk + 1, jnp.where(after, -rel - tq + 1, 0))
        steps_ref[n] = s
        return n + (gap.astype(F32) * slope2 < skip_thr).astype(jnp.int32)

    n_listed = lax.fori_loop(0, n_steps, list_step, 0)
    n_pairs = (n_listed + 1) // 2
    n_chunks = n_pairs // ATTN_UNROLL_PAIRS
    for extra in range(2):
        steps_ref[n_listed + extra] = n_steps

    def split(i):
        code = steps_ref[i]
        filler = code >= n_steps
        s = jnp.where(filler, 0, code)
        k0 = pl.multiple_of((s // nqt) * tk, tk)
        t, rel, before, after = offsets(s)
        idx = jnp.where(before, 0, jnp.where(after, 1, 2 + rel // tq))
        lin = rel.astype(F32) * slope2
        const = jnp.where(before, -lin, jnp.where(after, lin, 0.0))
        return k0, t, idx, jnp.where(filler, -3e38, const)

    def scores(s, t_ref):
        k0, t, idx, _ = split(s)
        bias = d0_ref[idx]
        sc = _dot_nt(k_ref[pl.ds(k0, tk), :], qbd_ref[t])
        t_ref[:, 0:tq] = sc[:, 0:tq] + bias
        t_ref[:, tq:2 * tq] = sc[:, tq:2 * tq] + bias

    def accumulate(s, t_ref):
        k0, t, _, const = split(s)
        sc = t_ref[...]
        m_old = m_ref[t]
        m_new = jnp.maximum(m_old, jnp.max(sc, axis=0, keepdims=True) + const)
        alpha = jnp.exp2(m_old - m_new)
        p = jnp.exp2(sc - (m_new - const))
        l_ref[t] = alpha * l_ref[t] + jnp.sum(p, axis=0, keepdims=True)
        acc_ref[t] = alpha * acc_ref[t] + jnp.dot(vt_ref[:, pl.ds(k0, tk)], p.astype(BF16),
                                                  preferred_element_type=F32)
        m_ref[t] = m_new

    scores(0, ta_ref)

    def pair(i):
        scores(2 * i + 1, tb_ref)
        accumulate(2 * i, ta_ref)
        scores(2 * i + 2, ta_ref)
        accumulate(2 * i + 1, tb_ref)

    def body(c, carry):
        for u in range(ATTN_UNROLL_PAIRS):
            pair(c * ATTN_UNROLL_PAIRS + u)
        return carry

    def single(i, carry):
        pair(i)
        return carry

    lax.fori_loop(0, n_chunks, body, 0)
    lax.fori_loop(n_chunks * ATTN_UNROLL_PAIRS, n_pairs, single, 0)

    lam = (jnp.exp(jnp.sum(lq1_ref[...] * lk1_ref[...], keepdims=True))
           - jnp.exp(jnp.sum(lq2_ref[...] * lk2_ref[...], keepdims=True)) + lambda_init)
    gain = subln_ref[...] * (1.0 - lambda_init)
    for t in range(nqt):
        rows = slice(t * tq, (t + 1) * tq)
        r = 1.0 / l_ref[t]
        o = (acc_ref[t, :, 0:tq] * r[:, 0:tq]
             - acc_ref[t, :, tq:2 * tq] * (lam * r[:, tq:2 * tq])).T
        inv = lax.rsqrt(jnp.mean(o * o, axis=-1, keepdims=True) + RMS_EPS)
        o_ref[rows, :] = (o * inv * (gain * _silu(gate_ref[rows, :].astype(F32)))
                          ).astype(o_ref.dtype)


def _diff_attention_gated(qk, qk_norms, v_t, gate, lq1, lk1, lq2, lk2, subln, lambda_init, b, l):
    nh, dv = DIFF_HEADS, DIFF_V_DIM
    tq, tk, nqt = ATTN_TQ, ATTN_TK, ATTN_Q_TILES
    qs = tq * nqt
    ns = l // qs
    assert l % qs == 0 and qs % NORM_ROWS == 0 and l % NORM_ROWS == 0
    slopes = jnp.exp2(-8.0 * (jnp.arange(nh, dtype=F32) + 1.0) / nh)
    vec = lambda a: a.reshape(1, -1).astype(F32)
    small = lambda n: pl.BlockSpec((1, n), lambda i, h, s, *_: (0, 0))
    grid_spec = pltpu.PrefetchScalarGridSpec(
        num_scalar_prefetch=2,
        grid=(b, nh, ns),
        in_specs=[pl.BlockSpec((qs, dv), lambda i, h, s, *_: (i * ns + s, h)),
                  pl.BlockSpec((l, dv), lambda i, h, s, *_: (i, nh + h)),
                  pl.BlockSpec((dv, l), lambda i, h, s, *_: (h, i)),
                  pl.BlockSpec((qs, dv), lambda i, h, s, *_: (i * ns + s, h)),
                  small(DIFF_HEAD_DIM), small(DIFF_HEAD_DIM), small(DIFF_HEAD_DIM),
                  small(DIFF_HEAD_DIM), small(dv)],
        out_specs=pl.BlockSpec((qs, dv), lambda i, h, s, *_: (i * ns + s, h)),
        scratch_shapes=[pltpu.VMEM((nqt, 2 * tq, 2 * DIFF_HEAD_DIM), BF16),
                        pltpu.VMEM((2 + tk // tq, tk, tq), F32),
                        pltpu.VMEM((tk, 2 * tq), F32),
                        pltpu.VMEM((tk, 2 * tq), F32),
                        pltpu.VMEM((nqt, 1, 2 * tq), F32), pltpu.VMEM((nqt, 1, 2 * tq), F32),
                        pltpu.VMEM((nqt, dv, 2 * tq), F32),
                        pltpu.SMEM((l // tk * nqt + 8,), jnp.int32)],
    )
    return pl.pallas_call(
        functools.partial(_attn_kernel, lambda_init=lambda_init),
        grid_spec=grid_spec,
        out_shape=jax.ShapeDtypeStruct((b * l, D_ATTN), BF16),
        compiler_params=_params(("parallel", "parallel", "parallel"), 48),
        name="diff_attention",
    )(slopes, qk_norms.reshape(-1), qk, qk, v_t, gate, vec(lq1), vec(lk1), vec(lq2), vec(lk2), vec(subln))


def _out_kernel(*refs, n_act, gated, with_next):
    refs = list(refs)
    act_refs = [refs.pop(0) for _ in range(n_act)]
    gate_ref = refs.pop(0) if gated else None
    w_refs = [refs.pop(0) for _ in range(n_act)]
    x_ref, g_ref = refs.pop(0), refs.pop(0)
    gn_ref = refs.pop(0) if with_next else None
    o_ref = refs.pop(0)
    h_ref = refs.pop(0) if with_next else None
    tm = x_ref.shape[0]
    sub = tm // OUT_ROW_SPLIT
    for part in range(OUT_ROW_SPLIT):
        rows = slice(part * sub, (part + 1) * sub)
        acts = [a_ref[rows, :] for a_ref in act_refs]
        if gated:
            acts[0] = (acts[0].astype(F32) * _silu(gate_ref[rows, :].astype(F32))).astype(BF16)
        y = jnp.dot(acts[0], w_refs[0][...], preferred_element_type=F32)
        for a, w_ref in zip(acts[1:], w_refs[1:]):
            y = y + jnp.dot(a, w_ref[...], preferred_element_type=F32)
        inv = lax.rsqrt(jnp.mean(y * y, axis=-1, keepdims=True) + RMS_EPS)
        out = x_ref[rows, :] + y * inv * g_ref[...]
        o_ref[rows, :] = out
        if with_next:
            inv_n = lax.rsqrt(jnp.mean(out * out, axis=-1, keepdims=True) + RMS_EPS)
            h_ref[rows, :] = (out * inv_n * gn_ref[...]).astype(BF16)


def _out_proj_residual(acts, ws, x, g, gate=None, g_next=None, tm=512):
    t, d = x.shape
    n_act = len(acts)
    row = lambda n: pl.BlockSpec((tm, n), lambda i: (i, 0))
    vec = lambda: pl.BlockSpec((1, d), lambda i: (0, 0))
    args = list(acts)
    in_specs = [row(a.shape[1]) for a in acts]
    if gate is not None:
        args.append(gate)
        in_specs.append(row(gate.shape[1]))
    args += list(ws) + [x, g.reshape(1, d).astype(F32)]
    in_specs += [pl.BlockSpec(w.shape, lambda i: (0, 0)) for w in ws] + [row(d), vec()]
    out_specs, out_shape = row(d), jax.ShapeDtypeStruct((t, d), F32)
    if g_next is not None:
        args.append(g_next.reshape(1, d).astype(F32))
        in_specs.append(vec())
        out_specs = [out_specs, row(d)]
        out_shape = [out_shape, jax.ShapeDtypeStruct((t, d), BF16)]
    return pl.pallas_call(
        functools.partial(_out_kernel, n_act=n_act, gated=gate is not None,
                          with_next=g_next is not None),
        grid=(t // tm,),
        in_specs=in_specs,
        out_specs=out_specs,
        out_shape=out_shape,
        compiler_params=_params(("parallel",), 48),
        name="out_proj_residual",
    )(*args)


def _trunk(x3, wts):
    b, l, d = x3.shape
    x = x3.reshape(b * l, d)

    w_in, g_pre = wts["ev_w_in"], wts["ev_norm_pre"]
    u = _matmul_cols(x, w_in, 0, D_FNET, prenorm_gain=g_pre)
    gate_a = _matmul_cols(x, w_in, D_FNET, D_FNET, prenorm_gain=g_pre)
    hg = _matmul_cols(x, w_in, 2 * D_FNET, 5 * D_HGRN, prenorm_gain=g_pre)
    y_a = _fnet_mix(u, b, l)
    y_b = _hgrn_mix_gated(hg, wts["hgrn_lb_logits"], wts["hgrn_norm"], b, l)
    w_out = wts["ev_w_out"]
    x, h = _out_proj_residual([y_a, y_b], [w_out[:D_FNET], w_out[D_FNET:]], x, wts["ev_norm_post"],
                              gate=gate_a, g_next=wts["od_norm_pre"])

    lambda_init = 0.8 - 0.6 * math.exp(-0.3 * 1)
    w_in = wts["od_w_in"]
    qk, qk_norms = _matmul_cols(h, w_in, 0, 2 * D_ATTN, out_scale=ATTN_QK_SCALE, row_norms=True,
                                tm=NORM_ROWS)
    v_t = _matmul_cols(h, w_in, 2 * D_ATTN, D_ATTN, transpose_out=True)
    gate = _matmul_cols(h, w_in, 3 * D_ATTN, D_ATTN)
    o = _diff_attention_gated(qk, qk_norms, v_t, gate, wts["lambda_q1"], wts["lambda_k1"], wts["lambda_q2"],
                              wts["lambda_k2"], wts["subln"], lambda_init, b, l)
    x = _out_proj_residual([o], [wts["od_w_out"]], x, wts["od_norm_post"])
    return x.reshape(b, l, d)


def kernel(x_prompt, x_sample, ev_w_in, ev_w_out, ev_norm_pre, ev_norm_post, hgrn_lb_logits,
           hgrn_norm, od_w_in, od_w_out, od_norm_pre, od_norm_post,
           lambda_q1, lambda_k1, lambda_q2, lambda_k2, subln):
    wts = {
        "ev_w_in": ev_w_in[0].astype(BF16), "ev_w_out": ev_w_out[0].astype(BF16),
        "ev_norm_pre": ev_norm_pre[0], "ev_norm_post": ev_norm_post[0],
        "hgrn_lb_logits": hgrn_lb_logits, "hgrn_norm": hgrn_norm[0],
        "od_w_in": od_w_in[0].astype(BF16), "od_w_out": od_w_out[0].astype(BF16),
        "od_norm_pre": od_norm_pre[0], "od_norm_post": od_norm_post[0],
        "lambda_q1": lambda_q1[0], "lambda_k1": lambda_k1[0],
        "lambda_q2": lambda_q2[0], "lambda_k2": lambda_k2[0], "subln": subln[0],
    }
    return (_trunk(x_prompt, wts), _trunk(x_sample, wts))
```

```python
import functools
import math

import jax
import jax.numpy as jnp
from jax import lax
from jax.experimental import pallas as pl
from jax.experimental.pallas import tpu as pltpu

F32 = jnp.float32
BF16 = jnp.bfloat16

D_MODEL = 2048
D_FNET = 1024
FNET_GROUP_DIM = 256
FNET_GROUPS = D_FNET // FNET_GROUP_DIM
D_HGRN = 1024
HGRN_HEAD_DIM = 128
HGRN_HEADS = D_HGRN // HGRN_HEAD_DIM
DIFF_HEADS = 8
DIFF_HEAD_DIM = 128
DIFF_V_DIM = 256
D_ATTN = DIFF_HEADS * DIFF_V_DIM
RMS_EPS = 1e-6
LOG2E = 1.4426950408889634

SUBLANES = 8
LANES = 128
NORM_ROWS = 1024
FFT_L2 = 128
FFT_KB = 8
HGRN_BLOCK = 128
HGRN_GROUP = 16
OUT_ROW_SPLIT = 2
MIB = 1024 * 1024


def _params(semantics, vmem_mib):
    return pltpu.CompilerParams(dimension_semantics=semantics, vmem_limit_bytes=vmem_mib * MIB)


def _silu(x):
    return x * jax.nn.sigmoid(x)


def _dot_nt(a, b):
    return lax.dot_general(a, b, (((1,), (1,)), ((), ())), preferred_element_type=F32)


def _mm_kernel(h_ref, w_ref, o_ref, *, out_scale):
    r = jnp.dot(h_ref[...], w_ref[...], preferred_element_type=F32)
    if out_scale != 1.0:
        r = r * out_scale
    o_ref[...] = r.astype(o_ref.dtype)


def _mm_prenorm_kernel(x_ref, g_ref, w_ref, o_ref):
    x = x_ref[...]
    inv = lax.rsqrt(jnp.mean(x * x, axis=-1, keepdims=True) + RMS_EPS)
    r = jnp.dot((x * g_ref[...]).astype(BF16), w_ref[...], preferred_element_type=F32)
    o_ref[...] = (r * inv).astype(o_ref.dtype)


def _mm_norm_kernel(h_ref, w_ref, o_ref, n_ref, *, out_scale):
    r = jnp.dot(h_ref[...], w_ref[...], preferred_element_type=F32) * out_scale
    o = r.astype(o_ref.dtype)
    o_ref[...] = o
    of = o.astype(F32)
    sq = of * of
    for g in range(sq.shape[1] // LANES):
        ss = jnp.sum(sq[:, g * LANES:(g + 1) * LANES], axis=1, keepdims=True)
        n_ref[0, 0, g:g + 1, :] = jnp.broadcast_to(jnp.max(ss, axis=0, keepdims=True), (1, LANES))


def _mm_t_kernel(h_ref, w_ref, o_ref, r_ref):
    r_ref[...] = jnp.dot(h_ref[...], w_ref[...], preferred_element_type=F32)
    o_ref[...] = r_ref[...].T.astype(o_ref.dtype)


def _matmul_cols(h, w, col_off, n_cols, transpose_out=False, out_scale=1.0, row_norms=False,
                 prenorm_gain=None, tm=1024, tn=1024):
    t, k = h.shape
    tm = min(tm, t)
    off = col_off // tn
    if prenorm_gain is not None:
        assert not transpose_out and not row_norms and out_scale == 1.0
        return pl.pallas_call(
            _mm_prenorm_kernel,
            grid=(n_cols // tn, t // tm),
            in_specs=[pl.BlockSpec((tm, k), lambda n, m: (m, 0)),
                      pl.BlockSpec((1, k), lambda n, m: (0, 0)),
                      pl.BlockSpec((k, tn), lambda n, m: (0, n + off))],
            out_specs=pl.BlockSpec((tm, tn), lambda n, m: (m, n)),
            out_shape=jax.ShapeDtypeStruct((t, n_cols), BF16),
            compiler_params=_params(("parallel", "parallel"), 48),
            name="in_proj_prenorm",
        )(h, prenorm_gain.reshape(1, k).astype(F32), w)
    if row_norms:
        assert not transpose_out and tn // LANES == SUBLANES
        out, nrm = pl.pallas_call(
            functools.partial(_mm_norm_kernel, out_scale=out_scale),
            grid=(n_cols // tn, t // tm),
            in_specs=[pl.BlockSpec((tm, k), lambda n, m: (m, 0)),
                      pl.BlockSpec((k, tn), lambda n, m: (0, n + off))],
            out_specs=[pl.BlockSpec((tm, tn), lambda n, m: (m, n)),
                       pl.BlockSpec((1, 1, SUBLANES, LANES), lambda n, m: (n, m, 0, 0))],
            out_shape=[jax.ShapeDtypeStruct((t, n_cols), BF16),
                       jax.ShapeDtypeStruct((n_cols // tn, t // tm, SUBLANES, LANES), F32)],
            compiler_params=_params(("parallel", "parallel"), 40),
            name="in_proj_norms",
        )(h, w)
        nrm = jnp.sqrt(nrm[:, :, :, 0]).transpose(1, 0, 2).reshape(t // tm, n_cols // LANES)
        return out, nrm
    if transpose_out:
        assert out_scale == 1.0
        body, out_shape, scratch = _mm_t_kernel, (n_cols, t), [pltpu.VMEM((tm, tn), F32)]
        out_spec = pl.BlockSpec((tn, tm), lambda n, m: (n, m))
    else:
        body = functools.partial(_mm_kernel, out_scale=out_scale)
        out_shape, scratch = (t, n_cols), []
        out_spec = pl.BlockSpec((tm, tn), lambda n, m: (m, n))
    return pl.pallas_call(
        body,
        grid=(n_cols // tn, t // tm),
        in_specs=[pl.BlockSpec((tm, k), lambda n, m: (m, 0)),
                  pl.BlockSpec((k, tn), lambda n, m: (0, n + off))],
        out_specs=out_spec,
        out_shape=jax.ShapeDtypeStruct(out_shape, BF16),
        scratch_shapes=scratch,
        compiler_params=_params(("parallel", "parallel"), 40),
        name="in_proj_t" if transpose_out else "in_proj",
    )(h, w)


def _fft_tables(l):
    l1 = l // FFT_L2
    two_pi = 2.0 * math.pi
    k1 = jnp.arange(l1, dtype=jnp.int32)
    a1 = ((k1[:, None] * k1[None, :]) % l1).astype(F32) * (two_pi / l1)
    f1 = jnp.concatenate([jnp.cos(a1), -jnp.sin(a1)], axis=0).astype(BF16)
    k2 = jnp.arange(FFT_L2, dtype=jnp.int32)
    kk = k1[:, None, None] + l1 * k2[None, :, None]
    a2 = ((kk * k2[None, None, :]) % l).astype(F32) * (two_pi / l)
    c2, s2 = jnp.cos(a2), jnp.sin(a2)
    g2 = jnp.concatenate([jnp.concatenate([c2, s2], axis=2),
                          jnp.concatenate([-s2, c2], axis=2)], axis=1).astype(BF16)
    c = jnp.arange(FNET_GROUP_DIM, dtype=jnp.int32)
    a3 = ((c[:, None] * c[None, :]) % FNET_GROUP_DIM).astype(F32) * (two_pi / FNET_GROUP_DIM)
    scale = 1.0 / math.sqrt(l * FNET_GROUP_DIM)
    cs = (jnp.concatenate([jnp.cos(a3), jnp.sin(a3)], axis=0) * scale).astype(BF16)
    return f1, g2, cs


def _fft1_kernel(f_ref, u_ref, t_ref):
    l1 = u_ref.shape[1]
    r = jnp.dot(f_ref[...], u_ref[0], preferred_element_type=F32)
    t_ref[0, 0] = r[:l1].astype(t_ref.dtype)
    t_ref[1, 0] = r[l1:].astype(t_ref.dtype)


def _fft2_kernel(t_ref, g_ref, cs_ref, o_ref, p_scr):
    kb = g_ref.shape[0]
    l2 = FFT_L2
    gd = FNET_GROUP_DIM
    for j in range(kb):
        gm = g_ref[j]
        for g in range(FNET_GROUPS):
            cols = slice(g * gd, (g + 1) * gd)
            rhs = jnp.concatenate([t_ref[0, 0, j, :, cols], t_ref[1, 0, j, :, cols]], axis=0)
            p = jnp.dot(gm, rhs, preferred_element_type=F32)
            r0 = (j * FNET_GROUPS + g) * l2
            p_scr[r0:r0 + l2, 0:gd] = p[:l2].astype(p_scr.dtype)
            p_scr[r0:r0 + l2, gd:2 * gd] = p[l2:].astype(p_scr.dtype)
    y = jnp.dot(p_scr[...], cs_ref[...], preferred_element_type=F32)
    for j in range(kb):
        for g in range(FNET_GROUPS):
            r0 = (j * FNET_GROUPS + g) * l2
            cols = slice(j * D_FNET + g * gd, j * D_FNET + (g + 1) * gd)
            o_ref[0, :, cols] = y[r0:r0 + l2].astype(o_ref.dtype)


def _fnet_mix(u, b, l):
    l1, l2 = l // FFT_L2, FFT_L2
    f1, g2, cs = _fft_tables(l)
    wcols = l2 * D_FNET
    w = min(wcols, (2 * MIB) // (2 * l1))
    t = pl.pallas_call(
        _fft1_kernel,
        grid=(b, wcols // w),
        in_specs=[pl.BlockSpec((2 * l1, l1), lambda i, j: (0, 0)),
                  pl.BlockSpec((1, l1, w), lambda i, j: (i, 0, j))],
        out_specs=pl.BlockSpec((2, 1, l1, w), lambda i, j: (0, i, 0, j)),
        out_shape=jax.ShapeDtypeStruct((2, b, l1, wcols), BF16),
        compiler_params=_params(("parallel", "parallel"), 32),
        name="fnet_stage1",
    )(f1, u.reshape(b, l1, wcols))
    kb = min(FFT_KB, l1)
    y = pl.pallas_call(
        _fft2_kernel,
        grid=(b, l1 // kb),
        in_specs=[pl.BlockSpec((2, 1, kb, l2, D_FNET), lambda i, j: (0, i, j, 0, 0)),
                  pl.BlockSpec((kb, 2 * l2, 2 * l2), lambda i, j: (j, 0, 0)),
                  pl.BlockSpec((2 * FNET_GROUP_DIM, FNET_GROUP_DIM), lambda i, j: (0, 0))],
        out_specs=pl.BlockSpec((1, l2, kb * D_FNET), lambda i, j: (i, 0, j)),
        out_shape=jax.ShapeDtypeStruct((b, l2, l1 * D_FNET), BF16),
        scratch_shapes=[pltpu.VMEM((kb * FNET_GROUPS * l2, 2 * FNET_GROUP_DIM), BF16)],
        compiler_params=_params(("parallel", "parallel"), 48),
        name="fnet_stage2",
    )(t.reshape(2, b, l1, l2, D_FNET), g2, cs)
    return y.reshape(b * l, D_FNET)


def _row_scan(g, forward):
    n = g.shape[0] // SUBLANES
    sub = lax.broadcasted_iota(jnp.int32, (SUBLANES, g.shape[1]), 0)
    outs = [None] * n
    carry = None
    for i in (range(n) if forward else range(n - 1, -1, -1)):
        y = g[i * SUBLANES:(i + 1) * SUBLANES, :]
        for sh in (1, 2, 4):
            if forward:
                y = y + jnp.where(sub >= sh, pltpu.roll(y, sh, 0), 0.0)
            else:
                y = y + jnp.where(sub < SUBLANES - sh, pltpu.roll(y, SUBLANES - sh, 0), 0.0)
        if carry is not None:
            y = y + carry
        edge = y[SUBLANES - 1:SUBLANES, :] if forward else y[0:1, :]
        carry = jnp.broadcast_to(edge, y.shape)
        outs[i] = y
    return jnp.concatenate(outs, axis=0)


def _hgrn_kernel(q_ref, i_ref, ff_ref, fb_ref, gate_ref, lbl_ref, gn_ref, o_ref, acc_ref, qs_ref):
    hb = HGRN_BLOCK
    half = hb // 2
    n_blocks = q_ref.shape[0] // hb
    group = min(HGRN_GROUP, n_blocks)
    dk = HGRN_HEAD_DIM
    row = lax.broadcasted_iota(jnp.int32, (hb, hb), 0)
    col = lax.broadcasted_iota(jnp.int32, (hb, hb), 1)

    logits = lbl_ref[...]
    mx = jnp.max(logits, axis=1, keepdims=True)
    ex = jnp.exp(logits - mx)
    lb_all = ex[:, 0, :] / jnp.sum(ex, axis=1)
    gn = gn_ref[...]

    def direction(f_ref, lb, forward):
        keep = (col <= row) if forward else (col >= row)

        def body(step, state_t):
            gi = step if forward else n_blocks // group - 1 - step
            base = gi * (group * hb)
            units = []
            for u in range(group):
                rows = pl.ds(pl.multiple_of(base + u * hb, hb), hb)
                if forward:
                    q = _silu(q_ref[rows, :].astype(F32))
                    qs_ref[rows, :] = q
                else:
                    q = qs_ref[rows, :]
                v = i_ref[rows, :]
                f = lb + (1.0 - lb) * jax.nn.sigmoid(f_ref[rows, :].astype(F32))
                k = 1.0 - f
                a = _row_scan(jnp.log2(f), forward)
                ref = a[half - 1:half, :] if forward else a[half:half + 1, :]
                end = a[hb - 1:hb, :] if forward else a[0:1, :]
                qt = q * jnp.exp2(a - ref)
                kt = k * jnp.exp2(ref - a)
                scores = jnp.where(keep, _dot_nt(qt.astype(BF16), kt.astype(BF16)), 0.0)
                o_intra = jnp.dot(scores.astype(BF16), v, preferred_element_type=F32)
                q_in = (qt * jnp.exp2(ref)).astype(BF16)
                k_end = (kt * jnp.exp2(end - ref)).astype(BF16)
                kv_t = lax.dot_general(v, k_end, (((0,), (0,)), ((), ())),
                                       preferred_element_type=F32)
                units.append((rows, o_intra, q_in, kv_t, jnp.exp2(end)))
            for rows, o_intra, q_in, kv_t, decay in (units if forward else units[::-1]):
                o = o_intra + _dot_nt(q_in, state_t.astype(BF16))
                state_t = state_t * decay + kv_t
                if forward:
                    acc_ref[rows, :] = o
                else:
                    tot = acc_ref[rows, :] + o
                    inv = lax.rsqrt(jnp.mean(tot * tot, axis=-1, keepdims=True) + RMS_EPS)
                    gate = gate_ref[rows, :].astype(F32)
                    o_ref[rows, :] = (tot * inv * gn * _silu(gate)).astype(o_ref.dtype)
            return state_t

        lax.fori_loop(0, n_blocks // group, body, jnp.zeros((dk, dk), F32))

    direction(ff_ref, lb_all[0:1, :], True)
    direction(fb_ref, lb_all[1:2, :], False)


def _hgrn_mix_gated(hg, lb_logits, g_norm, b, l):
    nh, dk = HGRN_HEADS, HGRN_HEAD_DIM

    def col(block):
        return pl.BlockSpec((l, dk), lambda i, h: (i, block * nh + h))

    return pl.pallas_call(
        _hgrn_kernel,
        grid=(b, nh),
        in_specs=[col(0), col(1), col(2), col(3), col(4),
                  pl.BlockSpec((2, lb_logits.shape[1], dk), lambda i, h: (0, 0, h)),
                  pl.BlockSpec((1, dk), lambda i, h: (0, 0))],
        out_specs=pl.BlockSpec((l, dk), lambda i, h: (i, h)),
        out_shape=jax.ShapeDtypeStruct((b * l, D_HGRN), BF16),
        scratch_shapes=[pltpu.VMEM((l, dk), F32), pltpu.VMEM((l, dk), F32)],
        compiler_params=_params(("parallel", "parallel"), 48),
        name="hgrn2",
    )(hg, hg, hg, hg, hg, lb_logits.astype(F32), g_norm.reshape(1, dk).astype(F32))


ATTN_TQ = 256
ATTN_TK = 512
ATTN_Q_TILES = 8
ATTN_QK_SCALE = math.sqrt(DIFF_HEAD_DIM ** -0.5 * LOG2E)
ATTN_UNROLL_PAIRS = (8, 2, 1)
ATTN_SKIP_MARGIN = 140.0


def _attn_kernel(slopes_ref, nrm_ref, q_ref, k_ref, vt_ref, gate_ref, lq1_ref, lk1_ref, lq2_ref,
                 lk2_ref, subln_ref, o_ref, qbd_ref, d0_ref, ta_ref, tb_ref, m_ref, l_ref, acc_ref,
                 steps_ref, *, lambda_init):
    tq, tk, nqt = ATTN_TQ, ATTN_TK, ATTN_Q_TILES
    n_blocks = k_ref.shape[0] // tk
    n_steps = n_blocks * nqt
    dh = DIFF_HEAD_DIM
    h = pl.program_id(1)
    sup = pl.program_id(2)
    slope2 = slopes_ref[h] * LOG2E
    kk = lax.broadcasted_iota(jnp.int32, (tk, tq), 0)
    qq = lax.broadcasted_iota(jnp.int32, (tk, tq), 1)
    d0 = (kk - qq).astype(F32) * slope2
    d0_ref[0] = d0
    d0_ref[1] = -d0
    for r in range(tk // tq):
        d0_ref[2 + r] = -jnp.abs(d0 - (r * tq) * slope2)

    zeros = jnp.zeros((tq, dh), BF16)
    for t in range(nqt):
        rows = slice(t * tq, (t + 1) * tq)
        qbd_ref[t, 0:tq, 0:dh] = q_ref[rows, 0:dh]
        qbd_ref[t, 0:tq, dh:2 * dh] = zeros
        qbd_ref[t, tq:2 * tq, 0:dh] = zeros
        qbd_ref[t, tq:2 * tq, dh:2 * dh] = q_ref[rows, dh:2 * dh]

    m_ref[...] = jnp.full(m_ref.shape, -1e30, F32)
    l_ref[...] = jnp.zeros(l_ref.shape, F32)
    acc_ref[...] = jnp.zeros(acc_ref.shape, F32)

    seq_tiles = k_ref.shape[0] // NORM_ROWS
    groups = 2 * D_ATTN // LANES
    tile0 = pl.program_id(0) * seq_tiles
    qk_bound = None
    for c in range(2):
        qn = None
        for r in range(nqt * tq // NORM_ROWS):
            v = nrm_ref[(tile0 + sup * (nqt * tq // NORM_ROWS) + r) * groups + 2 * h + c]
            qn = v if qn is None else jnp.maximum(qn, v)
        kn = None
        for r in range(seq_tiles):
            v = nrm_ref[(tile0 + r) * groups + groups // 2 + 2 * h + c]
            kn = v if kn is None else jnp.maximum(kn, v)
        qk_bound = qn * kn if qk_bound is None else jnp.maximum(qk_bound, qn * kn)
    skip_thr = ATTN_SKIP_MARGIN + 2.03125 * qk_bound

    def offsets(s):
        t = s % nqt
        rel = (sup * nqt + t) * tq - (s // nqt) * tk
        return t, rel, rel >= tk, rel <= -tq

    def list_step(s, n):
        t, rel, before, after = offsets(s)
        gap = jnp.where(before, rel - tk + 1, jnp.where(after, -rel - tq + 1, 0))
        steps_ref[n] = s
        return n + (gap.astype(F32) * slope2 < skip_thr).astype(jnp.int32)

    n_listed = lax.fori_loop(0, n_steps, list_step, 0, unroll=8)
    n_pairs = (n_listed + 1) // 2
    for extra in range(2):
        steps_ref[n_listed + extra] = n_steps

    def split(i):
        code = steps_ref[i]
        filler = code >= n_steps
        s = jnp.where(filler, 0, code)
        k0 = pl.multiple_of((s // nqt) * tk, tk)
        t, rel, before, after = offsets(s)
        idx = jnp.where(before, 0, jnp.where(after, 1, 2 + rel // tq))
        lin = rel.astype(F32) * slope2
        const = jnp.where(before, -lin, jnp.where(after, lin, 0.0))
        return k0, t, idx, jnp.where(filler, -3e38, const)

    def scores(s, t_ref):
        k0, t, idx, _ = split(s)
        bias = d0_ref[idx]
        sc = _dot_nt(k_ref[pl.ds(k0, tk), :], qbd_ref[t])
        t_ref[:, 0:tq] = sc[:, 0:tq] + bias
        t_ref[:, tq:2 * tq] = sc[:, tq:2 * tq] + bias

    def accumulate(s, t_ref):
        k0, t, _, const = split(s)
        sc = t_ref[...]
        m_old = m_ref[t]
        m_new = jnp.maximum(m_old, jnp.max(sc, axis=0, keepdims=True) + const)
        alpha = jnp.exp2(m_old - m_new)
        p = jnp.exp2(sc - (m_new - const))
        l_ref[t] = alpha * l_ref[t] + jnp.sum(p, axis=0, keepdims=True)
        acc_ref[t] = alpha * acc_ref[t] + jnp.dot(vt_ref[:, pl.ds(k0, tk)], p.astype(BF16),
                                                  preferred_element_type=F32)
        m_ref[t] = m_new

    scores(0, ta_ref)

    def pair(i):
        scores(2 * i + 1, tb_ref)
        accumulate(2 * i, ta_ref)
        scores(2 * i + 2, ta_ref)
        accumulate(2 * i + 1, tb_ref)

    done = 0
    for width in ATTN_UNROLL_PAIRS:
        def body(c, carry, width=width, done=done):
            for u in range(width):
                pair(done + c * width + u)
            return carry

        trips = (n_pairs - done) // width
        lax.fori_loop(0, trips, body, 0)
        done = done + trips * width

    lam = (jnp.exp(jnp.sum(lq1_ref[...] * lk1_ref[...], keepdims=True))
           - jnp.exp(jnp.sum(lq2_ref[...] * lk2_ref[...], keepdims=True)) + lambda_init)
    gain = subln_ref[...] * (1.0 - lambda_init)
    for t in range(nqt):
        rows = slice(t * tq, (t + 1) * tq)
        r = 1.0 / l_ref[t]
        o_t = (acc_ref[t, :, 0:tq] * r[:, 0:tq]
               - acc_ref[t, :, tq:2 * tq] * (lam * r[:, tq:2 * tq]))
        inv = lax.rsqrt(jnp.mean(o_t * o_t, axis=0, keepdims=True) + RMS_EPS)
        o_ref[rows, :] = ((o_t * inv).T * (gain * _silu(gate_ref[rows, :].astype(F32)))
                          ).astype(o_ref.dtype)


def _diff_attention_gated(qk, qk_norms, v_t, gate, lq1, lk1, lq2, lk2, subln, lambda_init, b, l):
    nh, dv = DIFF_HEADS, DIFF_V_DIM
    tq, tk, nqt = ATTN_TQ, ATTN_TK, ATTN_Q_TILES
    qs = tq * nqt
    ns = l // qs
    assert l % qs == 0 and qs % NORM_ROWS == 0 and l % NORM_ROWS == 0
    slopes = jnp.exp2(-8.0 * (jnp.arange(nh, dtype=F32) + 1.0) / nh)
    vec = lambda a: a.reshape(1, -1).astype(F32)
    small = lambda n: pl.BlockSpec((1, n), lambda i, h, s, *_: (0, 0))
    grid_spec = pltpu.PrefetchScalarGridSpec(
        num_scalar_prefetch=2,
        grid=(b, nh, ns),
        in_specs=[pl.BlockSpec((qs, dv), lambda i, h, s, *_: (i * ns + s, h)),
                  pl.BlockSpec((l, dv), lambda i, h, s, *_: (i, nh + h)),
                  pl.BlockSpec((dv, l), lambda i, h, s, *_: (h, i)),
                  pl.BlockSpec((qs, dv), lambda i, h, s, *_: (i * ns + s, h)),
                  small(DIFF_HEAD_DIM), small(DIFF_HEAD_DIM), small(DIFF_HEAD_DIM),
                  small(DIFF_HEAD_DIM), small(dv)],
        out_specs=pl.BlockSpec((qs, dv), lambda i, h, s, *_: (i * ns + s, h)),
        scratch_shapes=[pltpu.VMEM((nqt, 2 * tq, 2 * DIFF_HEAD_DIM), BF16),
                        pltpu.VMEM((2 + tk // tq, tk, tq), F32),
                        pltpu.VMEM((tk, 2 * tq), F32),
                        pltpu.VMEM((tk, 2 * tq), F32),
                        pltpu.VMEM((nqt, 1, 2 * tq), F32), pltpu.VMEM((nqt, 1, 2 * tq), F32),
                        pltpu.VMEM((nqt, dv, 2 * tq), F32),
                        pltpu.SMEM((l // tk * nqt + 8,), jnp.int32)],
    )
    return pl.pallas_call(
        functools.partial(_attn_kernel, lambda_init=lambda_init),
        grid_spec=grid_spec,
        out_shape=jax.ShapeDtypeStruct((b * l, D_ATTN), BF16),
        compiler_params=_params(("parallel", "parallel", "parallel"), 48),
        name="diff_attention",
    )(slopes, qk_norms.reshape(-1), qk, qk, v_t, gate, vec(lq1), vec(lk1), vec(lq2), vec(lk2), vec(subln))


def _out_kernel(*refs, n_act, gated, with_next):
    refs = list(refs)
    act_refs = [refs.pop(0) for _ in range(n_act)]
    gate_ref = refs.pop(0) if gated else None
    w_refs = [refs.pop(0) for _ in range(n_act)]
    x_ref, g_ref = refs.pop(0), refs.pop(0)
    gn_ref = refs.pop(0) if with_next else None
    o_ref = refs.pop(0)
    h_ref = refs.pop(0) if with_next else None
    tm = x_ref.shape[0]
    sub = tm // OUT_ROW_SPLIT
    for part in range(OUT_ROW_SPLIT):
        rows = slice(part * sub, (part + 1) * sub)
        acts = [a_ref[rows, :] for a_ref in act_refs]
        if gated:
            acts[0] = (acts[0].astype(F32) * _silu(gate_ref[rows, :].astype(F32))).astype(BF16)
        y = jnp.dot(acts[0], w_refs[0][...], preferred_element_type=F32)
        for a, w_ref in zip(acts[1:], w_refs[1:]):
            y = y + jnp.dot(a, w_ref[...], preferred_element_type=F32)
        inv = lax.rsqrt(jnp.mean(y * y, axis=-1, keepdims=True) + RMS_EPS)
        out = x_ref[rows, :] + y * inv * g_ref[...]
        o_ref[rows, :] = out
        if with_next:
            inv_n = lax.rsqrt(jnp.mean(out * out, axis=-1, keepdims=True) + RMS_EPS)
            h_ref[rows, :] = (out * inv_n * gn_ref[...]).astype(BF16)


def _out_proj_residual(acts, ws, x, g, gate=None, g_next=None, tm=512):
    t, d = x.shape
    n_act = len(acts)
    row = lambda n: pl.BlockSpec((tm, n), lambda i: (i, 0))
    vec = lambda: pl.BlockSpec((1, d), lambda i: (0, 0))
    args = list(acts)
    in_specs = [row(a.shape[1]) for a in acts]
    if gate is not None:
        args.append(gate)
        in_specs.append(row(gate.shape[1]))
    args += list(ws) + [x, g.reshape(1, d).astype(F32)]
    in_specs += [pl.BlockSpec(w.shape, lambda i: (0, 0)) for w in ws] + [row(d), vec()]
    out_specs, out_shape = row(d), jax.ShapeDtypeStruct((t, d), F32)
    if g_next is not None:
        args.append(g_next.reshape(1, d).astype(F32))
        in_specs.append(vec())
        out_specs = [out_specs, row(d)]
        out_shape = [out_shape, jax.ShapeDtypeStruct((t, d), BF16)]
    return pl.pallas_call(
        functools.partial(_out_kernel, n_act=n_act, gated=gate is not None,
                          with_next=g_next is not None),
        grid=(t // tm,),
        in_specs=in_specs,
        out_specs=out_specs,
        out_shape=out_shape,
        compiler_params=_params(("parallel",), 48),
        name="out_proj_residual",
    )(*args)


def _trunk(x3, wts):
    b, l, d = x3.shape
    x = x3.reshape(b * l, d)

    w_in, g_pre = wts["ev_w_in"], wts["ev_norm_pre"]
    u = _matmul_cols(x, w_in, 0, D_FNET, prenorm_gain=g_pre)
    gate_a = _matmul_cols(x, w_in, D_FNET, D_FNET, prenorm_gain=g_pre)
    hg = _matmul_cols(x, w_in, 2 * D_FNET, 5 * D_HGRN, prenorm_gain=g_pre)
    y_a = _fnet_mix(u, b, l)
    y_b = _hgrn_mix_gated(hg, wts["hgrn_lb_logits"], wts["hgrn_norm"], b, l)
    w_out = wts["ev_w_out"]
    x, h = _out_proj_residual([y_a, y_b], [w_out[:D_FNET], w_out[D_FNET:]], x, wts["ev_norm_post"],
                              gate=gate_a, g_next=wts["od_norm_pre"])

    lambda_init = 0.8 - 0.6 * math.exp(-0.3 * 1)
    w_in = wts["od_w_in"]
    qk, qk_norms = _matmul_cols(h, w_in, 0, 2 * D_ATTN, out_scale=ATTN_QK_SCALE, row_norms=True,
                                tm=NORM_ROWS)
    v_t = _matmul_cols(h, w_in, 2 * D_ATTN, D_ATTN, transpose_out=True)
    gate = _matmul_cols(h, w_in, 3 * D_ATTN, D_ATTN)
    o = _diff_attention_gated(qk, qk_norms, v_t, gate, wts["lambda_q1"], wts["lambda_k1"], wts["lambda_q2"],
                              wts["lambda_k2"], wts["subln"], lambda_init, b, l)
    x = _out_proj_residual([o], [wts["od_w_out"]], x, wts["od_norm_post"])
    return x.reshape(b, l, d)


def kernel(x_prompt, x_sample, ev_w_in, ev_w_out, ev_norm_pre, ev_norm_post, hgrn_lb_logits,
           hgrn_norm, od_w_in, od_w_out, od_norm_pre, od_norm_post,
           lambda_q1, lambda_k1, lambda_q2, lambda_k2, subln):
    wts = {
        "ev_w_in": ev_w_in[0].astype(BF16), "ev_w_out": ev_w_out[0].astype(BF16),
        "ev_norm_pre": ev_norm_pre[0], "ev_norm_post": ev_norm_post[0],
        "hgrn_lb_logits": hgrn_lb_logits, "hgrn_norm": hgrn_norm[0],
        "od_w_in": od_w_in[0].astype(BF16), "od_w_out": od_w_out[0].astype(BF16),
        "od_norm_pre": od_norm_pre[0], "od_norm_post": od_norm_post[0],
        "lambda_q1": lambda_q1[0], "lambda_k1": lambda_k1[0],
        "lambda_q2": lambda_q2[0], "lambda_k2": lambda_k2[0], "subln": subln[0],
    }
    return (_trunk(x_prompt, wts), _trunk(x_sample, wts))
```

```python
import functools
import math

import jax
import jax.numpy as jnp
from jax import lax
from jax.experimental import pallas as pl
from jax.experimental.pallas import tpu as pltpu

F32 = jnp.float32
BF16 = jnp.bfloat16

D_MODEL = 2048
D_FNET = 1024
FNET_GROUP_DIM = 256
FNET_GROUPS = D_FNET // FNET_GROUP_DIM
D_HGRN = 1024
HGRN_HEAD_DIM = 128
HGRN_HEADS = D_HGRN // HGRN_HEAD_DIM
DIFF_HEADS = 8
DIFF_HEAD_DIM = 128
DIFF_V_DIM = 256
D_ATTN = DIFF_HEADS * DIFF_V_DIM
RMS_EPS = 1e-6
LOG2E = 1.4426950408889634

SUBLANES = 8
LANES = 128
NORM_ROWS = 1024
FFT_L2 = 128
FFT_KB = 8
HGRN_BLOCK = 128
HGRN_GROUP = 16
OUT_ROW_SPLIT = 2
MIB = 1024 * 1024


def _params(semantics, vmem_mib):
    return pltpu.CompilerParams(dimension_semantics=semantics, vmem_limit_bytes=vmem_mib * MIB)


def _silu(x):
    return x * jax.nn.sigmoid(x)


def _dot_nt(a, b):
    return lax.dot_general(a, b, (((1,), (1,)), ((), ())), preferred_element_type=F32)


def _mm_kernel(h_ref, w_ref, o_ref, *, out_scale):
    r = jnp.dot(h_ref[...], w_ref[...], preferred_element_type=F32)
    if out_scale != 1.0:
        r = r * out_scale
    o_ref[...] = r.astype(o_ref.dtype)


def _mm_prenorm_kernel(x_ref, g_ref, w_ref, o_ref):
    x = x_ref[...]
    inv = lax.rsqrt(jnp.mean(x * x, axis=-1, keepdims=True) + RMS_EPS)
    r = jnp.dot((x * g_ref[...]).astype(BF16), w_ref[...], preferred_element_type=F32)
    o_ref[...] = (r * inv).astype(o_ref.dtype)


def _mm_norm_kernel(h_ref, w_ref, o_ref, n_ref, *, out_scale):
    r = jnp.dot(h_ref[...], w_ref[...], preferred_element_type=F32) * out_scale
    o = r.astype(o_ref.dtype)
    o_ref[...] = o
    of = o.astype(F32)
    sq = of * of
    for g in range(sq.shape[1] // LANES):
        ss = jnp.sum(sq[:, g * LANES:(g + 1) * LANES], axis=1, keepdims=True)
        n_ref[0, 0, g:g + 1, :] = jnp.broadcast_to(jnp.max(ss, axis=0, keepdims=True), (1, LANES))


def _mm_t_kernel(h_ref, w_ref, o_ref, r_ref):
    r_ref[...] = jnp.dot(h_ref[...], w_ref[...], preferred_element_type=F32)
    o_ref[...] = r_ref[...].T.astype(o_ref.dtype)


def _matmul_cols(h, w, col_off, n_cols, transpose_out=False, out_scale=1.0, row_norms=False,
                 prenorm_gain=None, tm=1024, tn=1024):
    t, k = h.shape
    tm = min(tm, t)
    off = col_off // tn
    if prenorm_gain is not None:
        assert not transpose_out and not row_norms and out_scale == 1.0
        return pl.pallas_call(
            _mm_prenorm_kernel,
            grid=(n_cols // tn, t // tm),
            in_specs=[pl.BlockSpec((tm, k), lambda n, m: (m, 0)),
                      pl.BlockSpec((1, k), lambda n, m: (0, 0)),
                      pl.BlockSpec((k, tn), lambda n, m: (0, n + off))],
            out_specs=pl.BlockSpec((tm, tn), lambda n, m: (m, n)),
            out_shape=jax.ShapeDtypeStruct((t, n_cols), BF16),
            compiler_params=_params(("parallel", "parallel"), 48),
            name="in_proj_prenorm",
        )(h, prenorm_gain.reshape(1, k).astype(F32), w)
    if row_norms:
        assert not transpose_out and tn // LANES == SUBLANES
        out, nrm = pl.pallas_call(
            functools.partial(_mm_norm_kernel, out_scale=out_scale),
            grid=(n_cols // tn, t // tm),
            in_specs=[pl.BlockSpec((tm, k), lambda n, m: (m, 0)),
                      pl.BlockSpec((k, tn), lambda n, m: (0, n + off))],
            out_specs=[pl.BlockSpec((tm, tn), lambda n, m: (m, n)),
                       pl.BlockSpec((1, 1, SUBLANES, LANES), lambda n, m: (n, m, 0, 0))],
            out_shape=[jax.ShapeDtypeStruct((t, n_cols), BF16),
                       jax.ShapeDtypeStruct((n_cols // tn, t // tm, SUBLANES, LANES), F32)],
            compiler_params=_params(("parallel", "parallel"), 40),
            name="in_proj_norms",
        )(h, w)
        nrm = jnp.sqrt(nrm[:, :, :, 0]).transpose(1, 0, 2).reshape(t // tm, n_cols // LANES)
        return out, nrm
    if transpose_out:
        assert out_scale == 1.0
        body, out_shape, scratch = _mm_t_kernel, (n_cols, t), [pltpu.VMEM((tm, tn), F32)]
        out_spec = pl.BlockSpec((tn, tm), lambda n, m: (n, m))
    else:
        body = functools.partial(_mm_kernel, out_scale=out_scale)
        out_shape, scratch = (t, n_cols), []
        out_spec = pl.BlockSpec((tm, tn), lambda n, m: (m, n))
    return pl.pallas_call(
        body,
        grid=(n_cols // tn, t // tm),
        in_specs=[pl.BlockSpec((tm, k), lambda n, m: (m, 0)),
                  pl.BlockSpec((k, tn), lambda n, m: (0, n + off))],
        out_specs=out_spec,
        out_shape=jax.ShapeDtypeStruct(out_shape, BF16),
        scratch_shapes=scratch,
        compiler_params=_params(("parallel", "parallel"), 40),
        name="in_proj_t" if transpose_out else "in_proj",
    )(h, w)


def _fft_tables(l):
    l1 = l // FFT_L2
    two_pi = 2.0 * math.pi
    k1 = jnp.arange(l1, dtype=jnp.int32)
    a1 = ((k1[:, None] * k1[None, :]) % l1).astype(F32) * (two_pi / l1)
    f1 = jnp.concatenate([jnp.cos(a1), -jnp.sin(a1)], axis=0).astype(BF16)
    k2 = jnp.arange(FFT_L2, dtype=jnp.int32)
    kk = k1[:, None, None] + l1 * k2[None, :, None]
    a2 = ((kk * k2[None, None, :]) % l).astype(F32) * (two_pi / l)
    c2, s2 = jnp.cos(a2), jnp.sin(a2)
    g2 = jnp.concatenate([jnp.concatenate([c2, s2], axis=2),
                          jnp.concatenate([-s2, c2], axis=2)], axis=1).astype(BF16)
    c = jnp.arange(FNET_GROUP_DIM, dtype=jnp.int32)
    a3 = ((c[:, None] * c[None, :]) % FNET_GROUP_DIM).astype(F32) * (two_pi / FNET_GROUP_DIM)
    scale = 1.0 / math.sqrt(l * FNET_GROUP_DIM)
    cs = (jnp.concatenate([jnp.cos(a3), jnp.sin(a3)], axis=0) * scale).astype(BF16)
    return f1, g2, cs


def _fft1_kernel(f_ref, u_ref, t_ref):
    l1 = u_ref.shape[1]
    r = jnp.dot(f_ref[...], u_ref[0], preferred_element_type=F32)
    t_ref[0, 0] = r[:l1].astype(t_ref.dtype)
    t_ref[1, 0] = r[l1:].astype(t_ref.dtype)


def _fft2_kernel(t_ref, g_ref, cs_ref, o_ref, p_scr):
    kb = g_ref.shape[0]
    l2 = FFT_L2
    gd = FNET_GROUP_DIM
    for j in range(kb):
        gm = g_ref[j]
        for g in range(FNET_GROUPS):
            cols = slice(g * gd, (g + 1) * gd)
            rhs = jnp.concatenate([t_ref[0, 0, j, :, cols], t_ref[1, 0, j, :, cols]], axis=0)
            p = jnp.dot(gm, rhs, preferred_element_type=F32)
            r0 = (j * FNET_GROUPS + g) * l2
            p_scr[r0:r0 + l2, 0:gd] = p[:l2].astype(p_scr.dtype)
            p_scr[r0:r0 + l2, gd:2 * gd] = p[l2:].astype(p_scr.dtype)
    y = jnp.dot(p_scr[...], cs_ref[...], preferred_element_type=F32)
    for j in range(kb):
        for g in range(FNET_GROUPS):
            r0 = (j * FNET_GROUPS + g) * l2
            cols = slice(j * D_FNET + g * gd, j * D_FNET + (g + 1) * gd)
            o_ref[0, :, cols] = y[r0:r0 + l2].astype(o_ref.dtype)


def _fnet_mix(u, b, l):
    l1, l2 = l // FFT_L2, FFT_L2
    f1, g2, cs = _fft_tables(l)
    wcols = l2 * D_FNET
    w = min(wcols, (2 * MIB) // (2 * l1))
    t = pl.pallas_call(
        _fft1_kernel,
        grid=(b, wcols // w),
        in_specs=[pl.BlockSpec((2 * l1, l1), lambda i, j: (0, 0)),
                  pl.BlockSpec((1, l1, w), lambda i, j: (i, 0, j))],
        out_specs=pl.BlockSpec((2, 1, l1, w), lambda i, j: (0, i, 0, j)),
        out_shape=jax.ShapeDtypeStruct((2, b, l1, wcols), BF16),
        compiler_params=_params(("parallel", "parallel"), 32),
        name="fnet_stage1",
    )(f1, u.reshape(b, l1, wcols))
    kb = min(FFT_KB, l1)
    y = pl.pallas_call(
        _fft2_kernel,
        grid=(b, l1 // kb),
        in_specs=[pl.BlockSpec((2, 1, kb, l2, D_FNET), lambda i, j: (0, i, j, 0, 0)),
                  pl.BlockSpec((kb, 2 * l2, 2 * l2), lambda i, j: (j, 0, 0)),
                  pl.BlockSpec((2 * FNET_GROUP_DIM, FNET_GROUP_DIM), lambda i, j: (0, 0))],
        out_specs=pl.BlockSpec((1, l2, kb * D_FNET), lambda i, j: (i, 0, j)),
        out_shape=jax.ShapeDtypeStruct((b, l2, l1 * D_FNET), BF16),
        scratch_shapes=[pltpu.VMEM((kb * FNET_GROUPS * l2, 2 * FNET_GROUP_DIM), BF16)],
        compiler_params=_params(("parallel", "parallel"), 48),
        name="fnet_stage2",
    )(t.reshape(2, b, l1, l2, D_FNET), g2, cs)
    return y


def _row_scan(g, forward):
    n = g.shape[0] // SUBLANES
    sub = lax.broadcasted_iota(jnp.int32, (SUBLANES, g.shape[1]), 0)
    outs = [None] * n
    carry = None
    for i in (range(n) if forward else range(n - 1, -1, -1)):
        y = g[i * SUBLANES:(i + 1) * SUBLANES, :]
        for sh in (1, 2, 4):
            if forward:
                y = y + jnp.where(sub >= sh, pltpu.roll(y, sh, 0), 0.0)
            else:
                y = y + jnp.where(sub < SUBLANES - sh, pltpu.roll(y, SUBLANES - sh, 0), 0.0)
        if carry is not None:
            y = y + carry
        edge = y[SUBLANES - 1:SUBLANES, :] if forward else y[0:1, :]
        carry = jnp.broadcast_to(edge, y.shape)
        outs[i] = y
    return jnp.concatenate(outs, axis=0)


def _hgrn_kernel(q_ref, i_ref, ff_ref, fb_ref, gate_ref, lbl_ref, gn_ref, o_ref, acc_ref, qs_ref):
    hb = HGRN_BLOCK
    half = hb // 2
    n_blocks = q_ref.shape[0] // hb
    group = min(HGRN_GROUP, n_blocks)
    dk = HGRN_HEAD_DIM
    row = lax.broadcasted_iota(jnp.int32, (hb, hb), 0)
    col = lax.broadcasted_iota(jnp.int32, (hb, hb), 1)

    logits = lbl_ref[...]
    mx = jnp.max(logits, axis=1, keepdims=True)
    ex = jnp.exp(logits - mx)
    lb_all = ex[:, 0, :] / jnp.sum(ex, axis=1)
    gn = gn_ref[...]

    def direction(f_ref, lb, forward):
        keep = (col <= row) if forward else (col >= row)

        def body(step, state_t):
            gi = step if forward else n_blocks // group - 1 - step
            base = gi * (group * hb)
            units = []
            for u in range(group):
                rows = pl.ds(pl.multiple_of(base + u * hb, hb), hb)
                if forward:
                    q = _silu(q_ref[rows, :].astype(F32))
                    qs_ref[rows, :] = q
                else:
                    q = qs_ref[rows, :]
                v = i_ref[rows, :]
                f = lb + (1.0 - lb) * jax.nn.sigmoid(f_ref[rows, :].astype(F32))
                k = 1.0 - f
                a = _row_scan(jnp.log2(f), forward)
                ref = a[half - 1:half, :] if forward else a[half:half + 1, :]
                end = a[hb - 1:hb, :] if forward else a[0:1, :]
                qt = q * jnp.exp2(a - ref)
                kt = k * jnp.exp2(ref - a)
                scores = jnp.where(keep, _dot_nt(qt.astype(BF16), kt.astype(BF16)), 0.0)
                o_intra = jnp.dot(scores.astype(BF16), v, preferred_element_type=F32)
                q_in = (qt * jnp.exp2(ref)).astype(BF16)
                k_end = (kt * jnp.exp2(end - ref)).astype(BF16)
                kv_t = lax.dot_general(v, k_end, (((0,), (0,)), ((), ())),
                                       preferred_element_type=F32)
                units.append((rows, o_intra, q_in, kv_t, jnp.exp2(end)))
            for rows, o_intra, q_in, kv_t, decay in (units if forward else units[::-1]):
                o = o_intra + _dot_nt(q_in, state_t.astype(BF16))
                state_t = state_t * decay + kv_t
                if forward:
                    acc_ref[rows, :] = o
                else:
                    tot = acc_ref[rows, :] + o
                    inv = lax.rsqrt(jnp.mean(tot * tot, axis=-1, keepdims=True) + RMS_EPS)
                    gate = gate_ref[rows, :].astype(F32)
                    o_ref[rows, :] = (tot * inv * gn * _silu(gate)).astype(o_ref.dtype)
            return state_t

        lax.fori_loop(0, n_blocks // group, body, jnp.zeros((dk, dk), F32))

    direction(ff_ref, lb_all[0:1, :], True)
    direction(fb_ref, lb_all[1:2, :], False)


def _hgrn_mix_gated(hg, lb_logits, g_norm, b, l):
    nh, dk = HGRN_HEADS, HGRN_HEAD_DIM

    def col(block):
        return pl.BlockSpec((l, dk), lambda i, h: (i, block * nh + h))

    return pl.pallas_call(
        _hgrn_kernel,
        grid=(b, nh),
        in_specs=[col(0), col(1), col(2), col(3), col(4),
                  pl.BlockSpec((2, lb_logits.shape[1], dk), lambda i, h: (0, 0, h)),
                  pl.BlockSpec((1, dk), lambda i, h: (0, 0))],
        out_specs=pl.BlockSpec((l, dk), lambda i, h: (i, h)),
        out_shape=jax.ShapeDtypeStruct((b * l, D_HGRN), BF16),
        scratch_shapes=[pltpu.VMEM((l, dk), F32), pltpu.VMEM((l, dk), F32)],
        compiler_params=_params(("parallel", "parallel"), 48),
        name="hgrn2",
    )(hg, hg, hg, hg, hg, lb_logits.astype(F32), g_norm.reshape(1, dk).astype(F32))


ATTN_TQ = 256
ATTN_TK = 512
ATTN_Q_TILES = 8
ATTN_QK_SCALE = math.sqrt(DIFF_HEAD_DIM ** -0.5 * LOG2E)
ATTN_UNROLL_PAIRS = (8, 2, 1)
ATTN_SKIP_MARGIN = 140.0


def _attn_kernel(slopes_ref, nrm_ref, q_ref, k_ref, vt_ref, gate_ref, lq1_ref, lk1_ref, lq2_ref,
                 lk2_ref, subln_ref, o_ref, qbd_ref, d0_ref, ta_ref, tb_ref, m_ref, l_ref, acc_ref,
                 steps_ref, *, lambda_init):
    tq, tk, nqt = ATTN_TQ, ATTN_TK, ATTN_Q_TILES
    n_blocks = k_ref.shape[0] // tk
    n_steps = n_blocks * nqt
    dh = DIFF_HEAD_DIM
    h = pl.program_id(1)
    sup = pl.program_id(2)
    slope2 = slopes_ref[h] * LOG2E
    kk = lax.broadcasted_iota(jnp.int32, (tk, tq), 0)
    qq = lax.broadcasted_iota(jnp.int32, (tk, tq), 1)
    d0 = (kk - qq).astype(F32) * slope2
    d0_ref[0] = d0
    d0_ref[1] = -d0
    for r in range(tk // tq):
        d0_ref[2 + r] = -jnp.abs(d0 - (r * tq) * slope2)

    zeros = jnp.zeros((tq, dh), BF16)
    for t in range(nqt):
        rows = slice(t * tq, (t + 1) * tq)
        qbd_ref[t, 0:tq, 0:dh] = q_ref[rows, 0:dh]
        qbd_ref[t, 0:tq, dh:2 * dh] = zeros
        qbd_ref[t, tq:2 * tq, 0:dh] = zeros
        qbd_ref[t, tq:2 * tq, dh:2 * dh] = q_ref[rows, dh:2 * dh]

    m_ref[...] = jnp.full(m_ref.shape, -1e30, F32)
    l_ref[...] = jnp.zeros(l_ref.shape, F32)
    acc_ref[...] = jnp.zeros(acc_ref.shape, F32)

    seq_tiles = k_ref.shape[0] // NORM_ROWS
    groups = 2 * D_ATTN // LANES
    tile0 = pl.program_id(0) * seq_tiles
    qk_bound = None
    for c in range(2):
        qn = None
        for r in range(nqt * tq // NORM_ROWS):
            v = nrm_ref[(tile0 + sup * (nqt * tq // NORM_ROWS) + r) * groups + 2 * h + c]
            qn = v if qn is None else jnp.maximum(qn, v)
        kn = None
        for r in range(seq_tiles):
            v = nrm_ref[(tile0 + r) * groups + groups // 2 + 2 * h + c]
            kn = v if kn is None else jnp.maximum(kn, v)
        qk_bound = qn * kn if qk_bound is None else jnp.maximum(qk_bound, qn * kn)
    skip_thr = ATTN_SKIP_MARGIN + 2.03125 * qk_bound

    def offsets(s):
        t = s % nqt
        rel = (sup * nqt + t) * tq - (s // nqt) * tk
        return t, rel, rel >= tk, rel <= -tq

    def list_step(s, n):
        t, rel, before, after = offsets(s)
        gap = jnp.where(before, rel - tk + 1, jnp.where(after, -rel - tq + 1, 0))
        steps_ref[n] = s
        return n + (gap.astype(F32) * slope2 < skip_thr).astype(jnp.int32)

    n_listed = lax.fori_loop(0, n_steps, list_step, 0, unroll=8)
    n_pairs = (n_listed + 1) // 2
    for extra in range(2):
        steps_ref[n_listed + extra] = n_steps

    def split(i):
        code = steps_ref[i]
        filler = code >= n_steps
        s = jnp.where(filler, 0, code)
        k0 = pl.multiple_of((s // nqt) * tk, tk)
        t, rel, before, after = offsets(s)
        idx = jnp.where(before, 0, jnp.where(after, 1, 2 + rel // tq))
        lin = rel.astype(F32) * slope2
        const = jnp.where(before, -lin, jnp.where(after, lin, 0.0))
        return k0, t, idx, jnp.where(filler, -3e38, const)

    def scores(s, t_ref):
        k0, t, idx, _ = split(s)
        bias = d0_ref[idx]
        sc = _dot_nt(k_ref[pl.ds(k0, tk), :], qbd_ref[t])
        t_ref[:, 0:tq] = sc[:, 0:tq] + bias
        t_ref[:, tq:2 * tq] = sc[:, tq:2 * tq] + bias

    def accumulate(s, t_ref):
        k0, t, _, const = split(s)
        sc = t_ref[...]
        m_old = m_ref[t]
        m_new = jnp.maximum(m_old, jnp.max(sc, axis=0, keepdims=True) + const)
        alpha = jnp.exp2(m_old - m_new)
        p = jnp.exp2(sc - (m_new - const))
        l_ref[t] = alpha * l_ref[t] + jnp.sum(p, axis=0, keepdims=True)
        acc_ref[t] = alpha * acc_ref[t] + jnp.dot(vt_ref[:, pl.ds(k0, tk)], p.astype(BF16),
                                                  preferred_element_type=F32)
        m_ref[t] = m_new

    scores(0, ta_ref)

    def pair(i):
        scores(2 * i + 1, tb_ref)
        accumulate(2 * i, ta_ref)
        scores(2 * i + 2, ta_ref)
        accumulate(2 * i + 1, tb_ref)

    done = 0
    for width in ATTN_UNROLL_PAIRS:
        def body(c, carry, width=width, done=done):
            for u in range(width):
                pair(done + c * width + u)
            return carry

        trips = (n_pairs - done) // width
        lax.fori_loop(0, trips, body, 0)
        done = done + trips * width

    lam = (jnp.exp(jnp.sum(lq1_ref[...] * lk1_ref[...], keepdims=True))
           - jnp.exp(jnp.sum(lq2_ref[...] * lk2_ref[...], keepdims=True)) + lambda_init)
    gain = subln_ref[...] * (1.0 - lambda_init)
    for t in range(nqt):
        rows = slice(t * tq, (t + 1) * tq)
        r = 1.0 / l_ref[t]
        o_t = (acc_ref[t, :, 0:tq] * r[:, 0:tq]
               - acc_ref[t, :, tq:2 * tq] * (lam * r[:, tq:2 * tq]))
        inv = lax.rsqrt(jnp.mean(o_t * o_t, axis=0, keepdims=True) + RMS_EPS)
        o_ref[rows, :] = ((o_t * inv).T * (gain * _silu(gate_ref[rows, :].astype(F32)))
                          ).astype(o_ref.dtype)


def _diff_attention_gated(qk, qk_norms, v_t, gate, lq1, lk1, lq2, lk2, subln, lambda_init, b, l):
    nh, dv = DIFF_HEADS, DIFF_V_DIM
    tq, tk, nqt = ATTN_TQ, ATTN_TK, ATTN_Q_TILES
    qs = tq * nqt
    ns = l // qs
    assert l % qs == 0 and qs % NORM_ROWS == 0 and l % NORM_ROWS == 0
    slopes = jnp.exp2(-8.0 * (jnp.arange(nh, dtype=F32) + 1.0) / nh)
    vec = lambda a: a.reshape(1, -1).astype(F32)
    small = lambda n: pl.BlockSpec((1, n), lambda i, h, s, *_: (0, 0))
    grid_spec = pltpu.PrefetchScalarGridSpec(
        num_scalar_prefetch=2,
        grid=(b, nh, ns),
        in_specs=[pl.BlockSpec((qs, dv), lambda i, h, s, *_: (i * ns + s, h)),
                  pl.BlockSpec((l, dv), lambda i, h, s, *_: (i, nh + h)),
                  pl.BlockSpec((dv, l), lambda i, h, s, *_: (h, i)),
                  pl.BlockSpec((qs, dv), lambda i, h, s, *_: (i * ns + s, h)),
                  small(DIFF_HEAD_DIM), small(DIFF_HEAD_DIM), small(DIFF_HEAD_DIM),
                  small(DIFF_HEAD_DIM), small(dv)],
        out_specs=pl.BlockSpec((qs, dv), lambda i, h, s, *_: (i * ns + s, h)),
        scratch_shapes=[pltpu.VMEM((nqt, 2 * tq, 2 * DIFF_HEAD_DIM), BF16),
                        pltpu.VMEM((2 + tk // tq, tk, tq), F32),
                        pltpu.VMEM((tk, 2 * tq), F32),
                        pltpu.VMEM((tk, 2 * tq), F32),
                        pltpu.VMEM((nqt, 1, 2 * tq), F32), pltpu.VMEM((nqt, 1, 2 * tq), F32),
                        pltpu.VMEM((nqt, dv, 2 * tq), F32),
                        pltpu.SMEM((l // tk * nqt + 8,), jnp.int32)],
    )
    return pl.pallas_call(
        functools.partial(_attn_kernel, lambda_init=lambda_init),
        grid_spec=grid_spec,
        out_shape=jax.ShapeDtypeStruct((b * l, D_ATTN), BF16),
        compiler_params=_params(("parallel", "parallel", "parallel"), 48),
        name="diff_attention",
    )(slopes, qk_norms.reshape(-1), qk, qk, v_t, gate, vec(lq1), vec(lk1), vec(lq2), vec(lk2), vec(subln))


def _out_kernel(*refs, n_act, gated, with_next, fft_l1):
    refs = list(refs)
    act_refs = [refs.pop(0) for _ in range(n_act)]
    gate_ref = refs.pop(0) if gated else None
    w_refs = [refs.pop(0) for _ in range(n_act)]
    x_ref, g_ref = refs.pop(0), refs.pop(0)
    gn_ref = refs.pop(0) if with_next else None
    o_ref = refs.pop(0)
    h_ref = refs.pop(0) if with_next else None
    tm = x_ref.shape[0]
    if fft_l1:
        rows_ref = refs.pop(0)
        width = act_refs[0].shape[2] // fft_l1
        for k1 in range(fft_l1):
            blk = act_refs[0][0, :, k1 * width:(k1 + 1) * width].astype(F32)
            for t in range(width // LANES):
                rows_ref[t, pl.ds(k1, tm // fft_l1, stride=fft_l1), :] = blk[:, t * LANES:(t + 1) * LANES]
    sub = tm // OUT_ROW_SPLIT
    for part in range(OUT_ROW_SPLIT):
        rows = slice(part * sub, (part + 1) * sub)
        acts = [a_ref[rows, :] for a_ref in act_refs[1 if fft_l1 else 0:]]
        if fft_l1:
            acts.insert(0, jnp.concatenate([rows_ref[t, rows, :] for t in range(rows_ref.shape[0])],
                                           axis=1))
        if gated:
            acts[0] = (acts[0].astype(F32) * _silu(gate_ref[rows, :].astype(F32))).astype(BF16)
        y = jnp.dot(acts[0], w_refs[0][...], preferred_element_type=F32)
        for a, w_ref in zip(acts[1:], w_refs[1:]):
            y = y + jnp.dot(a, w_ref[...], preferred_element_type=F32)
        inv = lax.rsqrt(jnp.mean(y * y, axis=-1, keepdims=True) + RMS_EPS)
        out = x_ref[rows, :] + y * inv * g_ref[...]
        o_ref[rows, :] = out
        if with_next:
            inv_n = lax.rsqrt(jnp.mean(out * out, axis=-1, keepdims=True) + RMS_EPS)
            h_ref[rows, :] = (out * inv_n * gn_ref[...]).astype(BF16)


def _out_proj_residual(acts, ws, x, g, gate=None, g_next=None, fft_l1=0, tm=512):
    t, d = x.shape
    n_act = len(acts)
    row = lambda n: pl.BlockSpec((tm, n), lambda i: (i, 0))
    vec = lambda: pl.BlockSpec((1, d), lambda i: (0, 0))
    args = list(acts)
    in_specs = [row(a.shape[-1]) for a in acts]
    scratch = []
    if fft_l1:
        per_seq = acts[0].shape[1] * fft_l1 // tm
        in_specs[0] = pl.BlockSpec((1, tm // fft_l1, acts[0].shape[2]),
                                   lambda i: (i // per_seq, i % per_seq, 0))
        width = acts[0].shape[2] // fft_l1
        scratch = [pltpu.VMEM((width // LANES, tm, LANES), F32)]
    if gate is not None:
        args.append(gate)
        in_specs.append(row(gate.shape[1]))
    args += list(ws) + [x, g.reshape(1, d).astype(F32)]
    in_specs += [pl.BlockSpec(w.shape, lambda i: (0, 0)) for w in ws] + [row(d), vec()]
    out_specs, out_shape = row(d), jax.ShapeDtypeStruct((t, d), F32)
    if g_next is not None:
        args.append(g_next.reshape(1, d).astype(F32))
        in_specs.append(vec())
        out_specs = [out_specs, row(d)]
        out_shape = [out_shape, jax.ShapeDtypeStruct((t, d), BF16)]
    return pl.pallas_call(
        functools.partial(_out_kernel, n_act=n_act, gated=gate is not None,
                          with_next=g_next is not None, fft_l1=fft_l1),
        grid=(t // tm,),
        in_specs=in_specs,
        out_specs=out_specs,
        out_shape=out_shape,
        scratch_shapes=scratch,
        compiler_params=_params(("parallel",), 52),
        name="out_proj_residual",
    )(*args)


def _trunk(x3, wts):
    b, l, d = x3.shape
    x = x3.reshape(b * l, d)

    w_in, g_pre = wts["ev_w_in"], wts["ev_norm_pre"]
    u = _matmul_cols(x, w_in, 0, D_FNET, prenorm_gain=g_pre)
    gate_a = _matmul_cols(x, w_in, D_FNET, D_FNET, prenorm_gain=g_pre)
    hg = _matmul_cols(x, w_in, 2 * D_FNET, 5 * D_HGRN, prenorm_gain=g_pre)
    y_a = _fnet_mix(u, b, l)
    y_b = _hgrn_mix_gated(hg, wts["hgrn_lb_logits"], wts["hgrn_norm"], b, l)
    w_out = wts["ev_w_out"]
    l1 = l // FFT_L2
    tm_out = 512
    if (tm_out // l1) % 16 == 0:
        fft_l1 = l1
    else:
        fft_l1, y_a = 0, y_a.reshape(b * l, D_FNET)
    x, h = _out_proj_residual([y_a, y_b], [w_out[:D_FNET], w_out[D_FNET:]], x, wts["ev_norm_post"],
                              gate=gate_a, g_next=wts["od_norm_pre"], fft_l1=fft_l1, tm=tm_out)

    lambda_init = 0.8 - 0.6 * math.exp(-0.3 * 1)
    w_in = wts["od_w_in"]
    qk, qk_norms = _matmul_cols(h, w_in, 0, 2 * D_ATTN, out_scale=ATTN_QK_SCALE, row_norms=True,
                                tm=NORM_ROWS)
    v_t = _matmul_cols(h, w_in, 2 * D_ATTN, D_ATTN, transpose_out=True)
    gate = _matmul_cols(h, w_in, 3 * D_ATTN, D_ATTN)
    o = _diff_attention_gated(qk, qk_norms, v_t, gate, wts["lambda_q1"], wts["lambda_k1"], wts["lambda_q2"],
                              wts["lambda_k2"], wts["subln"], lambda_init, b, l)
    x = _out_proj_residual([o], [wts["od_w_out"]], x, wts["od_norm_post"])
    return x.reshape(b, l, d)


def kernel(x_prompt, x_sample, ev_w_in, ev_w_out, ev_norm_pre, ev_norm_post, hgrn_lb_logits,
           hgrn_norm, od_w_in, od_w_out, od_norm_pre, od_norm_post,
           lambda_q1, lambda_k1, lambda_q2, lambda_k2, subln):
    wts = {
        "ev_w_in": ev_w_in[0].astype(BF16), "ev_w_out": ev_w_out[0].astype(BF16),
        "ev_norm_pre": ev_norm_pre[0], "ev_norm_post": ev_norm_post[0],
        "hgrn_lb_logits": hgrn_lb_logits, "hgrn_norm": hgrn_norm[0],
        "od_w_in": od_w_in[0].astype(BF16), "od_w_out": od_w_out[0].astype(BF16),
        "od_norm_pre": od_norm_pre[0], "od_norm_post": od_norm_post[0],
        "lambda_q1": lambda_q1[0], "lambda_k1": lambda_k1[0],
        "lambda_q2": lambda_q2[0], "lambda_k2": lambda_k2[0], "subln": subln[0],
    }
    return (_trunk(x_prompt, wts), _trunk(x_sample, wts))
```

```python
import functools
import math

import jax
import jax.numpy as jnp
from jax import lax
from jax.experimental import pallas as pl
from jax.experimental.pallas import tpu as pltpu

F32 = jnp.float32
BF16 = jnp.bfloat16

D_MODEL = 2048
D_FNET = 1024
FNET_GROUP_DIM = 256
FNET_GROUPS = D_FNET // FNET_GROUP_DIM
D_HGRN = 1024
HGRN_HEAD_DIM = 128
HGRN_HEADS = D_HGRN // HGRN_HEAD_DIM
DIFF_HEADS = 8
DIFF_HEAD_DIM = 128
DIFF_V_DIM = 256
D_ATTN = DIFF_HEADS * DIFF_V_DIM
RMS_EPS = 1e-6
LOG2E = 1.4426950408889634

SUBLANES = 8
LANES = 128
NORM_ROWS = 1024
FFT_L2 = 128
FFT_KB = 8
HGRN_BLOCK = 128
HGRN_GROUP = 16
OUT_ROW_SPLIT = 2
MIB = 1024 * 1024


def _params(semantics, vmem_mib):
    return pltpu.CompilerParams(dimension_semantics=semantics, vmem_limit_bytes=vmem_mib * MIB)


def _silu(x):
    return x * jax.nn.sigmoid(x)


def _dot_nt(a, b):
    return lax.dot_general(a, b, (((1,), (1,)), ((), ())), preferred_element_type=F32)


def _mm_kernel(h_ref, w_ref, o_ref, *, out_scale):
    r = jnp.dot(h_ref[...], w_ref[...], preferred_element_type=F32)
    if out_scale != 1.0:
        r = r * out_scale
    o_ref[...] = r.astype(o_ref.dtype)


def _mm_prenorm_kernel(x_ref, g_ref, w_ref, o_ref):
    x = x_ref[...]
    inv = lax.rsqrt(jnp.mean(x * x, axis=-1, keepdims=True) + RMS_EPS)
    r = jnp.dot((x * g_ref[...]).astype(BF16), w_ref[...], preferred_element_type=F32)
    o_ref[...] = (r * inv).astype(o_ref.dtype)


def _mm_norm_kernel(h_ref, w_ref, o_ref, n_ref, *, out_scale):
    r = jnp.dot(h_ref[...], w_ref[...], preferred_element_type=F32) * out_scale
    o = r.astype(o_ref.dtype)
    o_ref[...] = o
    of = o.astype(F32)
    sq = of * of
    for g in range(sq.shape[1] // LANES):
        ss = jnp.sum(sq[:, g * LANES:(g + 1) * LANES], axis=1, keepdims=True)
        n_ref[0, 0, g:g + 1, :] = jnp.broadcast_to(jnp.max(ss, axis=0, keepdims=True), (1, LANES))


def _mm_t_kernel(h_ref, w_ref, o_ref, r_ref):
    r_ref[...] = jnp.dot(h_ref[...], w_ref[...], preferred_element_type=F32)
    o_ref[...] = r_ref[...].T.astype(o_ref.dtype)


def _matmul_cols(h, w, col_off, n_cols, transpose_out=False, out_scale=1.0, row_norms=False,
                 prenorm_gain=None, tm=1024, tn=1024):
    t, k = h.shape
    tm = min(tm, t)
    off = col_off // tn
    if prenorm_gain is not None:
        assert not transpose_out and not row_norms and out_scale == 1.0
        return pl.pallas_call(
            _mm_prenorm_kernel,
            grid=(n_cols // tn, t // tm),
            in_specs=[pl.BlockSpec((tm, k), lambda n, m: (m, 0)),
                      pl.BlockSpec((1, k), lambda n, m: (0, 0)),
                      pl.BlockSpec((k, tn), lambda n, m: (0, n + off))],
            out_specs=pl.BlockSpec((tm, tn), lambda n, m: (m, n)),
            out_shape=jax.ShapeDtypeStruct((t, n_cols), BF16),
            compiler_params=_params(("parallel", "parallel"), 48),
            name="in_proj_prenorm",
        )(h, prenorm_gain.reshape(1, k).astype(F32), w)
    if row_norms:
        assert not transpose_out and tn // LANES == SUBLANES
        out, nrm = pl.pallas_call(
            functools.partial(_mm_norm_kernel, out_scale=out_scale),
            grid=(n_cols // tn, t // tm),
            in_specs=[pl.BlockSpec((tm, k), lambda n, m: (m, 0)),
                      pl.BlockSpec((k, tn), lambda n, m: (0, n + off))],
            out_specs=[pl.BlockSpec((tm, tn), lambda n, m: (m, n)),
                       pl.BlockSpec((1, 1, SUBLANES, LANES), lambda n, m: (n, m, 0, 0))],
            out_shape=[jax.ShapeDtypeStruct((t, n_cols), BF16),
                       jax.ShapeDtypeStruct((n_cols // tn, t // tm, SUBLANES, LANES), F32)],
            compiler_params=_params(("parallel", "parallel"), 40),
            name="in_proj_norms",
        )(h, w)
        nrm = jnp.sqrt(nrm[:, :, :, 0]).transpose(1, 0, 2).reshape(t // tm, n_cols // LANES)
        return out, nrm
    if transpose_out:
        assert out_scale == 1.0
        body, out_shape, scratch = _mm_t_kernel, (n_cols, t), [pltpu.VMEM((tm, tn), F32)]
        out_spec = pl.BlockSpec((tn, tm), lambda n, m: (n, m))
    else:
        body = functools.partial(_mm_kernel, out_scale=out_scale)
        out_shape, scratch = (t, n_cols), []
        out_spec = pl.BlockSpec((tm, tn), lambda n, m: (m, n))
    return pl.pallas_call(
        body,
        grid=(n_cols // tn, t // tm),
        in_specs=[pl.BlockSpec((tm, k), lambda n, m: (m, 0)),
                  pl.BlockSpec((k, tn), lambda n, m: (0, n + off))],
        out_specs=out_spec,
        out_shape=jax.ShapeDtypeStruct(out_shape, BF16),
        scratch_shapes=scratch,
        compiler_params=_params(("parallel", "parallel"), 40),
        name="in_proj_t" if transpose_out else "in_proj",
    )(h, w)


def _fft_tables(l):
    l1 = l // FFT_L2
    two_pi = 2.0 * math.pi
    k1 = jnp.arange(l1, dtype=jnp.int32)
    a1 = ((k1[:, None] * k1[None, :]) % l1).astype(F32) * (two_pi / l1)
    f1 = jnp.concatenate([jnp.cos(a1), -jnp.sin(a1)], axis=0).astype(BF16)
    k2 = jnp.arange(FFT_L2, dtype=jnp.int32)
    kk = k1[:, None, None] + l1 * k2[None, :, None]
    a2 = ((kk * k2[None, None, :]) % l).astype(F32) * (two_pi / l)
    c2, s2 = jnp.cos(a2), jnp.sin(a2)
    g2 = jnp.concatenate([jnp.concatenate([c2, s2], axis=2),
                          jnp.concatenate([-s2, c2], axis=2)], axis=1).astype(BF16)
    c = jnp.arange(FNET_GROUP_DIM, dtype=jnp.int32)
    a3 = ((c[:, None] * c[None, :]) % FNET_GROUP_DIM).astype(F32) * (two_pi / FNET_GROUP_DIM)
    scale = 1.0 / math.sqrt(l * FNET_GROUP_DIM)
    cs = (jnp.concatenate([jnp.cos(a3), jnp.sin(a3)], axis=0) * scale).astype(BF16)
    return f1, g2, cs


def _fft1_kernel(f_ref, u_ref, t_ref):
    l1 = u_ref.shape[1]
    r = jnp.dot(f_ref[...], u_ref[0], preferred_element_type=F32)
    t_ref[0, 0] = r[:l1].astype(t_ref.dtype)
    t_ref[1, 0] = r[l1:].astype(t_ref.dtype)


def _fft2_kernel(t_ref, g_ref, cs_ref, o_ref, p_scr):
    kb = g_ref.shape[0]
    l2 = FFT_L2
    gd = FNET_GROUP_DIM
    for j in range(kb):
        gm = g_ref[j]
        for g in range(FNET_GROUPS):
            cols = slice(g * gd, (g + 1) * gd)
            rhs = jnp.concatenate([t_ref[0, 0, j, :, cols], t_ref[1, 0, j, :, cols]], axis=0)
            p = jnp.dot(gm, rhs, preferred_element_type=F32)
            r0 = (j * FNET_GROUPS + g) * l2
            p_scr[r0:r0 + l2, 0:gd] = p[:l2].astype(p_scr.dtype)
            p_scr[r0:r0 + l2, gd:2 * gd] = p[l2:].astype(p_scr.dtype)
    y = jnp.dot(p_scr[...], cs_ref[...], preferred_element_type=F32)
    for j in range(kb):
        for g in range(FNET_GROUPS):
            r0 = (j * FNET_GROUPS + g) * l2
            cols = slice(j * D_FNET + g * gd, j * D_FNET + (g + 1) * gd)
            o_ref[0, :, cols] = y[r0:r0 + l2].astype(o_ref.dtype)


def _fnet_mix(u, b, l):
    l1, l2 = l // FFT_L2, FFT_L2
    f1, g2, cs = _fft_tables(l)
    wcols = l2 * D_FNET
    w = min(wcols, (2 * MIB) // (2 * l1))
    t = pl.pallas_call(
        _fft1_kernel,
        grid=(b, wcols // w),
        in_specs=[pl.BlockSpec((2 * l1, l1), lambda i, j: (0, 0)),
                  pl.BlockSpec((1, l1, w), lambda i, j: (i, 0, j))],
        out_specs=pl.BlockSpec((2, 1, l1, w), lambda i, j: (0, i, 0, j)),
        out_shape=jax.ShapeDtypeStruct((2, b, l1, wcols), BF16),
        compiler_params=_params(("parallel", "parallel"), 32),
        name="fnet_stage1",
    )(f1, u.reshape(b, l1, wcols))
    kb = min(FFT_KB, l1)
    y = pl.pallas_call(
        _fft2_kernel,
        grid=(b, l1 // kb),
        in_specs=[pl.BlockSpec((2, 1, kb, l2, D_FNET), lambda i, j: (0, i, j, 0, 0)),
                  pl.BlockSpec((kb, 2 * l2, 2 * l2), lambda i, j: (j, 0, 0)),
                  pl.BlockSpec((2 * FNET_GROUP_DIM, FNET_GROUP_DIM), lambda i, j: (0, 0))],
        out_specs=pl.BlockSpec((1, l2, kb * D_FNET), lambda i, j: (i, 0, j)),
        out_shape=jax.ShapeDtypeStruct((b, l2, l1 * D_FNET), BF16),
        scratch_shapes=[pltpu.VMEM((kb * FNET_GROUPS * l2, 2 * FNET_GROUP_DIM), BF16)],
        compiler_params=_params(("parallel", "parallel"), 48),
        name="fnet_stage2",
    )(t.reshape(2, b, l1, l2, D_FNET), g2, cs)
    return y


def _row_scan(g, forward):
    n = g.shape[0] // SUBLANES
    sub = lax.broadcasted_iota(jnp.int32, (SUBLANES, g.shape[1]), 0)
    outs = [None] * n
    carry = None
    for i in (range(n) if forward else range(n - 1, -1, -1)):
        y = g[i * SUBLANES:(i + 1) * SUBLANES, :]
        for sh in (1, 2, 4):
            if forward:
                y = y + jnp.where(sub >= sh, pltpu.roll(y, sh, 0), 0.0)
            else:
                y = y + jnp.where(sub < SUBLANES - sh, pltpu.roll(y, SUBLANES - sh, 0), 0.0)
        if carry is not None:
            y = y + carry
        edge = y[SUBLANES - 1:SUBLANES, :] if forward else y[0:1, :]
        carry = jnp.broadcast_to(edge, y.shape)
        outs[i] = y
    return jnp.concatenate(outs, axis=0)


def _hgrn_kernel(q_ref, i_ref, ff_ref, fb_ref, gate_ref, lbl_ref, gn_ref, o_ref, acc_ref, qs_ref):
    hb = HGRN_BLOCK
    half = hb // 2
    n_blocks = q_ref.shape[0] // hb
    group = min(HGRN_GROUP, n_blocks)
    dk = HGRN_HEAD_DIM
    row = lax.broadcasted_iota(jnp.int32, (hb, hb), 0)
    col = lax.broadcasted_iota(jnp.int32, (hb, hb), 1)
    row2 = lax.broadcasted_iota(jnp.int32, (hb, 2 * hb), 0)
    col2 = lax.broadcasted_iota(jnp.int32, (hb, 2 * hb), 1) & (hb - 1)

    logits = lbl_ref[...]
    mx = jnp.max(logits, axis=1, keepdims=True)
    ex = jnp.exp(logits - mx)
    lb_all = ex[:, 0, :] / jnp.sum(ex, axis=1)
    gn = gn_ref[...]

    def direction(f_ref, lb, forward):
        keep = (col <= row) if forward else (col >= row)
        tri2 = jnp.where((col2 <= row2) if forward else (col2 >= row2), 1.0, 0.0).astype(BF16)

        def body(step, state_t):
            gi = step if forward else n_blocks // group - 1 - step
            base = gi * (group * hb)
            gates, pieces = [], []
            for u in range(group):
                rows = pl.ds(pl.multiple_of(base + u * hb, hb), hb)
                f = lb + (1.0 - lb) * jax.nn.sigmoid(f_ref[rows, :].astype(F32))
                g = jnp.log2(f)
                hi = g.astype(BF16)
                pieces.append((hi, (g - hi.astype(F32)).astype(BF16)))
                gates.append(1.0 - f)
            stacked = jnp.concatenate([jnp.concatenate([p[0] for p in pieces], axis=1),
                                       jnp.concatenate([p[1] for p in pieces], axis=1)], axis=0)
            a_all = jnp.dot(tri2, stacked, preferred_element_type=F32)
            units = []
            for u in range(group):
                rows = pl.ds(pl.multiple_of(base + u * hb, hb), hb)
                if forward:
                    q = _silu(q_ref[rows, :].astype(F32))
                    qs_ref[rows, :] = q
                else:
                    q = qs_ref[rows, :]
                v = i_ref[rows, :]
                k = gates[u]
                a = a_all[:, u * dk:(u + 1) * dk]
                ref = a[half - 1:half, :] if forward else a[half:half + 1, :]
                end = a[hb - 1:hb, :] if forward else a[0:1, :]
                qt = q * jnp.exp2(a - ref)
                kt = k * jnp.exp2(ref - a)
                scores = jnp.where(keep, _dot_nt(qt.astype(BF16), kt.astype(BF16)), 0.0)
                o_intra = jnp.dot(scores.astype(BF16), v, preferred_element_type=F32)
                q_in = (qt * jnp.exp2(ref)).astype(BF16)
                k_end = (kt * jnp.exp2(end - ref)).astype(BF16)
                kv_t = lax.dot_general(v, k_end, (((0,), (0,)), ((), ())),
                                       preferred_element_type=F32)
                units.append((rows, o_intra, q_in, kv_t, jnp.exp2(end)))
            for rows, o_intra, q_in, kv_t, decay in (units if forward else units[::-1]):
                o = o_intra + _dot_nt(q_in, state_t.astype(BF16))
                state_t = state_t * decay + kv_t
                if forward:
                    acc_ref[rows, :] = o
                else:
                    tot = acc_ref[rows, :] + o
                    inv = lax.rsqrt(jnp.mean(tot * tot, axis=-1, keepdims=True) + RMS_EPS)
                    gate = gate_ref[rows, :].astype(F32)
                    o_ref[rows, :] = (tot * inv * gn * _silu(gate)).astype(o_ref.dtype)
            return state_t

        lax.fori_loop(0, n_blocks // group, body, jnp.zeros((dk, dk), F32))

    direction(ff_ref, lb_all[0:1, :], True)
    direction(fb_ref, lb_all[1:2, :], False)


def _hgrn_mix_gated(hg, lb_logits, g_norm, b, l):
    nh, dk = HGRN_HEADS, HGRN_HEAD_DIM

    def col(block):
        return pl.BlockSpec((l, dk), lambda i, h: (i, block * nh + h))

    return pl.pallas_call(
        _hgrn_kernel,
        grid=(b, nh),
        in_specs=[col(0), col(1), col(2), col(3), col(4),
                  pl.BlockSpec((2, lb_logits.shape[1], dk), lambda i, h: (0, 0, h)),
                  pl.BlockSpec((1, dk), lambda i, h: (0, 0))],
        out_specs=pl.BlockSpec((l, dk), lambda i, h: (i, h)),
        out_shape=jax.ShapeDtypeStruct((b * l, D_HGRN), BF16),
        scratch_shapes=[pltpu.VMEM((l, dk), F32), pltpu.VMEM((l, dk), F32)],
        compiler_params=_params(("parallel", "parallel"), 48),
        name="hgrn2",
    )(hg, hg, hg, hg, hg, lb_logits.astype(F32), g_norm.reshape(1, dk).astype(F32))


ATTN_TQ = 256
ATTN_TK = 512
ATTN_Q_TILES = 8
ATTN_QK_SCALE = math.sqrt(DIFF_HEAD_DIM ** -0.5 * LOG2E)
ATTN_UNROLL_PAIRS = (8, 2, 1)
ATTN_SKIP_MARGIN = 140.0


def _attn_kernel(slopes_ref, nrm_ref, q_ref, k_ref, vt_ref, gate_ref, lq1_ref, lk1_ref, lq2_ref,
                 lk2_ref, subln_ref, o_ref, qbd_ref, d0_ref, ta_ref, tb_ref, m_ref, l_ref, acc_ref,
                 steps_ref, *, lambda_init):
    tq, tk, nqt = ATTN_TQ, ATTN_TK, ATTN_Q_TILES
    n_blocks = k_ref.shape[0] // tk
    n_steps = n_blocks * nqt
    dh = DIFF_HEAD_DIM
    h = pl.program_id(1)
    sup = pl.program_id(2)
    slope2 = slopes_ref[h] * LOG2E
    kk = lax.broadcasted_iota(jnp.int32, (tk, tq), 0)
    qq = lax.broadcasted_iota(jnp.int32, (tk, tq), 1)
    d0 = (kk - qq).astype(F32) * slope2
    d0_ref[0] = d0
    d0_ref[1] = -d0
    for r in range(tk // tq):
        d0_ref[2 + r] = -jnp.abs(d0 - (r * tq) * slope2)

    zeros = jnp.zeros((tq, dh), BF16)
    for t in range(nqt):
        rows = slice(t * tq, (t + 1) * tq)
        qbd_ref[t, 0:tq, 0:dh] = q_ref[rows, 0:dh]
        qbd_ref[t, 0:tq, dh:2 * dh] = zeros
        qbd_ref[t, tq:2 * tq, 0:dh] = zeros
        qbd_ref[t, tq:2 * tq, dh:2 * dh] = q_ref[rows, dh:2 * dh]

    m_ref[...] = jnp.full(m_ref.shape, -1e30, F32)
    l_ref[...] = jnp.zeros(l_ref.shape, F32)
    acc_ref[...] = jnp.zeros(acc_ref.shape, F32)

    seq_tiles = k_ref.shape[0] // NORM_ROWS
    groups = 2 * D_ATTN // LANES
    tile0 = pl.program_id(0) * seq_tiles
    qk_bound = None
    for c in range(2):
        qn = None
        for r in range(nqt * tq // NORM_ROWS):
            v = nrm_ref[(tile0 + sup * (nqt * tq // NORM_ROWS) + r) * groups + 2 * h + c]
            qn = v if qn is None else jnp.maximum(qn, v)
        kn = None
        for r in range(seq_tiles):
            v = nrm_ref[(tile0 + r) * groups + groups // 2 + 2 * h + c]
            kn = v if kn is None else jnp.maximum(kn, v)
        qk_bound = qn * kn if qk_bound is None else jnp.maximum(qk_bound, qn * kn)
    skip_thr = ATTN_SKIP_MARGIN + 2.03125 * qk_bound

    def offsets(s):
        t = s % nqt
        rel = (sup * nqt + t) * tq - (s // nqt) * tk
        return t, rel, rel >= tk, rel <= -tq

    def list_step(s, n):
        t, rel, before, after = offsets(s)
        gap = jnp.where(before, rel - tk + 1, jnp.where(after, -rel - tq + 1, 0))
        steps_ref[n] = s
        return n + (gap.astype(F32) * slope2 < skip_thr).astype(jnp.int32)

    n_listed = lax.fori_loop(0, n_steps, list_step, 0, unroll=8)
    n_pairs = (n_listed + 1) // 2
    for extra in range(2):
        steps_ref[n_listed + extra] = n_steps

    def split(i):
        code = steps_ref[i]
        filler = code >= n_steps
        s = jnp.where(filler, 0, code)
        k0 = pl.multiple_of((s // nqt) * tk, tk)
        t, rel, before, after = offsets(s)
        idx = jnp.where(before, 0, jnp.where(after, 1, 2 + rel // tq))
        lin = rel.astype(F32) * slope2
        const = jnp.where(before, -lin, jnp.where(after, lin, 0.0))
        return k0, t, idx, jnp.where(filler, -3e38, const)

    def scores(s, t_ref):
        k0, t, idx, _ = split(s)
        bias = d0_ref[idx]
        sc = _dot_nt(k_ref[pl.ds(k0, tk), :], qbd_ref[t])
        t_ref[:, 0:tq] = sc[:, 0:tq] + bias
        t_ref[:, tq:2 * tq] = sc[:, tq:2 * tq] + bias

    def accumulate(s, t_ref):
        k0, t, _, const = split(s)
        sc = t_ref[...]
        m_old = m_ref[t]
        m_new = jnp.maximum(m_old, jnp.max(sc, axis=0, keepdims=True) + const)
        alpha = jnp.exp2(m_old - m_new)
        p = jnp.exp2(sc - (m_new - const))
        l_ref[t] = alpha * l_ref[t] + jnp.sum(p, axis=0, keepdims=True)
        acc_ref[t] = alpha * acc_ref[t] + jnp.dot(vt_ref[:, pl.ds(k0, tk)], p.astype(BF16),
                                                  preferred_element_type=F32)
        m_ref[t] = m_new

    scores(0, ta_ref)

    def pair(i):
        scores(2 * i + 1, tb_ref)
        accumulate(2 * i, ta_ref)
        scores(2 * i + 2, ta_ref)
        accumulate(2 * i + 1, tb_ref)

    done = 0
    for width in ATTN_UNROLL_PAIRS:
        def body(c, carry, width=width, done=done):
            for u in range(width):
                pair(done + c * width + u)
            return carry

        trips = (n_pairs - done) // width
        lax.fori_loop(0, trips, body, 0)
        done = done + trips * width

    lam = (jnp.exp(jnp.sum(lq1_ref[...] * lk1_ref[...], keepdims=True))
           - jnp.exp(jnp.sum(lq2_ref[...] * lk2_ref[...], keepdims=True)) + lambda_init)
    gain = subln_ref[...] * (1.0 - lambda_init)
    for t in range(nqt):
        rows = slice(t * tq, (t + 1) * tq)
        r = 1.0 / l_ref[t]
        o_t = (acc_ref[t, :, 0:tq] * r[:, 0:tq]
               - acc_ref[t, :, tq:2 * tq] * (lam * r[:, tq:2 * tq]))
        inv = lax.rsqrt(jnp.mean(o_t * o_t, axis=0, keepdims=True) + RMS_EPS)
        o_ref[rows, :] = ((o_t * inv).T * (gain * _silu(gate_ref[rows, :].astype(F32)))
                          ).astype(o_ref.dtype)


def _diff_attention_gated(qk, qk_norms, v_t, gate, lq1, lk1, lq2, lk2, subln, lambda_init, b, l):
    nh, dv = DIFF_HEADS, DIFF_V_DIM
    tq, tk, nqt = ATTN_TQ, ATTN_TK, ATTN_Q_TILES
    qs = tq * nqt
    ns = l // qs
    assert l % qs == 0 and qs % NORM_ROWS == 0 and l % NORM_ROWS == 0
    slopes = jnp.exp2(-8.0 * (jnp.arange(nh, dtype=F32) + 1.0) / nh)
    vec = lambda a: a.reshape(1, -1).astype(F32)
    small = lambda n: pl.BlockSpec((1, n), lambda i, h, s, *_: (0, 0))
    grid_spec = pltpu.PrefetchScalarGridSpec(
        num_scalar_prefetch=2,
        grid=(b, nh, ns),
        in_specs=[pl.BlockSpec((qs, dv), lambda i, h, s, *_: (i * ns + s, h)),
                  pl.BlockSpec((l, dv), lambda i, h, s, *_: (i, nh + h)),
                  pl.BlockSpec((dv, l), lambda i, h, s, *_: (h, i)),
                  pl.BlockSpec((qs, dv), lambda i, h, s, *_: (i * ns + s, h)),
                  small(DIFF_HEAD_DIM), small(DIFF_HEAD_DIM), small(DIFF_HEAD_DIM),
                  small(DIFF_HEAD_DIM), small(dv)],
        out_specs=pl.BlockSpec((qs, dv), lambda i, h, s, *_: (i * ns + s, h)),
        scratch_shapes=[pltpu.VMEM((nqt, 2 * tq, 2 * DIFF_HEAD_DIM), BF16),
                        pltpu.VMEM((2 + tk // tq, tk, tq), F32),
                        pltpu.VMEM((tk, 2 * tq), F32),
                        pltpu.VMEM((tk, 2 * tq), F32),
                        pltpu.VMEM((nqt, 1, 2 * tq), F32), pltpu.VMEM((nqt, 1, 2 * tq), F32),
                        pltpu.VMEM((nqt, dv, 2 * tq), F32),
                        pltpu.SMEM((l // tk * nqt + 8,), jnp.int32)],
    )
    return pl.pallas_call(
        functools.partial(_attn_kernel, lambda_init=lambda_init),
        grid_spec=grid_spec,
        out_shape=jax.ShapeDtypeStruct((b * l, D_ATTN), BF16),
        compiler_params=_params(("parallel", "parallel", "parallel"), 48),
        name="diff_attention",
    )(slopes, qk_norms.reshape(-1), qk, qk, v_t, gate, vec(lq1), vec(lk1), vec(lq2), vec(lk2), vec(subln))


def _out_kernel(*refs, n_act, gated, with_next, fft_l1):
    refs = list(refs)
    act_refs = [refs.pop(0) for _ in range(n_act)]
    gate_ref = refs.pop(0) if gated else None
    w_refs = [refs.pop(0) for _ in range(n_act)]
    x_ref, g_ref = refs.pop(0), refs.pop(0)
    gn_ref = refs.pop(0) if with_next else None
    o_ref = refs.pop(0)
    h_ref = refs.pop(0) if with_next else None
    tm = x_ref.shape[0]
    if fft_l1:
        rows_ref = refs.pop(0)
        width = act_refs[0].shape[2] // fft_l1
        for k1 in range(fft_l1):
            blk = act_refs[0][0, :, k1 * width:(k1 + 1) * width].astype(F32)
            for t in range(width // LANES):
                rows_ref[t, pl.ds(k1, tm // fft_l1, stride=fft_l1), :] = blk[:, t * LANES:(t + 1) * LANES]
    sub = tm // OUT_ROW_SPLIT
    for part in range(OUT_ROW_SPLIT):
        rows = slice(part * sub, (part + 1) * sub)
        acts = [a_ref[rows, :] for a_ref in act_refs[1 if fft_l1 else 0:]]
        if fft_l1:
            acts.insert(0, jnp.concatenate([rows_ref[t, rows, :] for t in range(rows_ref.shape[0])],
                                           axis=1))
        if gated:
            acts[0] = (acts[0].astype(F32) * _silu(gate_ref[rows, :].astype(F32))).astype(BF16)
        y = jnp.dot(acts[0], w_refs[0][...], preferred_element_type=F32)
        for a, w_ref in zip(acts[1:], w_refs[1:]):
            y = y + jnp.dot(a, w_ref[...], preferred_element_type=F32)
        inv = lax.rsqrt(jnp.mean(y * y, axis=-1, keepdims=True) + RMS_EPS)
        out = x_ref[rows, :] + y * inv * g_ref[...]
        o_ref[rows, :] = out
        if with_next:
            inv_n = lax.rsqrt(jnp.mean(out * out, axis=-1, keepdims=True) + RMS_EPS)
            h_ref[rows, :] = (out * inv_n * gn_ref[...]).astype(BF16)


def _out_proj_residual(acts, ws, x, g, gate=None, g_next=None, fft_l1=0, tm=512):
    t, d = x.shape
    n_act = len(acts)
    row = lambda n: pl.BlockSpec((tm, n), lambda i: (i, 0))
    vec = lambda: pl.BlockSpec((1, d), lambda i: (0, 0))
    args = list(acts)
    in_specs = [row(a.shape[-1]) for a in acts]
    scratch = []
    if fft_l1:
        per_seq = acts[0].shape[1] * fft_l1 // tm
        in_specs[0] = pl.BlockSpec((1, tm // fft_l1, acts[0].shape[2]),
                                   lambda i: (i // per_seq, i % per_seq, 0))
        width = acts[0].shape[2] // fft_l1
        scratch = [pltpu.VMEM((width // LANES, tm, LANES), F32)]
    if gate is not None:
        args.append(gate)
        in_specs.append(row(gate.shape[1]))
    args += list(ws) + [x, g.reshape(1, d).astype(F32)]
    in_specs += [pl.BlockSpec(w.shape, lambda i: (0, 0)) for w in ws] + [row(d), vec()]
    out_specs, out_shape = row(d), jax.ShapeDtypeStruct((t, d), F32)
    if g_next is not None:
        args.append(g_next.reshape(1, d).astype(F32))
        in_specs.append(vec())
        out_specs = [out_specs, row(d)]
        out_shape = [out_shape, jax.ShapeDtypeStruct((t, d), BF16)]
    return pl.pallas_call(
        functools.partial(_out_kernel, n_act=n_act, gated=gate is not None,
                          with_next=g_next is not None, fft_l1=fft_l1),
        grid=(t // tm,),
        in_specs=in_specs,
        out_specs=out_specs,
        out_shape=out_shape,
        scratch_shapes=scratch,
        compiler_params=_params(("parallel",), 52),
        name="out_proj_residual",
    )(*args)


def _trunk(x3, wts):
    b, l, d = x3.shape
    x = x3.reshape(b * l, d)

    w_in, g_pre = wts["ev_w_in"], wts["ev_norm_pre"]
    u = _matmul_cols(x, w_in, 0, D_FNET, prenorm_gain=g_pre)
    gate_a = _matmul_cols(x, w_in, D_FNET, D_FNET, prenorm_gain=g_pre)
    hg = _matmul_cols(x, w_in, 2 * D_FNET, 5 * D_HGRN, prenorm_gain=g_pre)
    y_a = _fnet_mix(u, b, l)
    y_b = _hgrn_mix_gated(hg, wts["hgrn_lb_logits"], wts["hgrn_norm"], b, l)
    w_out = wts["ev_w_out"]
    l1 = l // FFT_L2
    tm_out = 512
    if (tm_out // l1) % 16 == 0:
        fft_l1 = l1
    else:
        fft_l1, y_a = 0, y_a.reshape(b * l, D_FNET)
    x, h = _out_proj_residual([y_a, y_b], [w_out[:D_FNET], w_out[D_FNET:]], x, wts["ev_norm_post"],
                              gate=gate_a, g_next=wts["od_norm_pre"], fft_l1=fft_l1, tm=tm_out)

    lambda_init = 0.8 - 0.6 * math.exp(-0.3 * 1)
    w_in = wts["od_w_in"]
    qk, qk_norms = _matmul_cols(h, w_in, 0, 2 * D_ATTN, out_scale=ATTN_QK_SCALE, row_norms=True,
                                tm=NORM_ROWS)
    v_t = _matmul_cols(h, w_in, 2 * D_ATTN, D_ATTN, transpose_out=True)
    gate = _matmul_cols(h, w_in, 3 * D_ATTN, D_ATTN)
    o = _diff_attention_gated(qk, qk_norms, v_t, gate, wts["lambda_q1"], wts["lambda_k1"], wts["lambda_q2"],
                              wts["lambda_k2"], wts["subln"], lambda_init, b, l)
    x = _out_proj_residual([o], [wts["od_w_out"]], x, wts["od_norm_post"])
    return x.reshape(b, l, d)


def kernel(x_prompt, x_sample, ev_w_in, ev_w_out, ev_norm_pre, ev_norm_post, hgrn_lb_logits,
           hgrn_norm, od_w_in, od_w_out, od_norm_pre, od_norm_post,
           lambda_q1, lambda_k1, lambda_q2, lambda_k2, subln):
    wts = {
        "ev_w_in": ev_w_in[0].astype(BF16), "ev_w_out": ev_w_out[0].astype(BF16),
        "ev_norm_pre": ev_norm_pre[0], "ev_norm_post": ev_norm_post[0],
        "hgrn_lb_logits": hgrn_lb_logits, "hgrn_norm": hgrn_norm[0],
        "od_w_in": od_w_in[0].astype(BF16), "od_w_out": od_w_out[0].astype(BF16),
        "od_norm_pre": od_norm_pre[0], "od_norm_post": od_norm_post[0],
        "lambda_q1": lambda_q1[0], "lambda_k1": lambda_k1[0],
        "lambda_q2": lambda_q2[0], "lambda_k2": lambda_k2[0], "subln": subln[0],
    }
    return (_trunk(x_prompt, wts), _trunk(x_sample, wts))
```

```python
import functools
import math

import jax
import jax.numpy as jnp
from jax import lax
from jax.experimental import pallas as pl
from jax.experimental.pallas import tpu as pltpu

F32 = jnp.float32
BF16 = jnp.bfloat16

D_MODEL = 2048
D_FNET = 1024
FNET_GROUP_DIM = 256
FNET_GROUPS = D_FNET // FNET_GROUP_DIM
D_HGRN = 1024
HGRN_HEAD_DIM = 128
HGRN_HEADS = D_HGRN // HGRN_HEAD_DIM
DIFF_HEADS = 8
DIFF_HEAD_DIM = 128
DIFF_V_DIM = 256
D_ATTN = DIFF_HEADS * DIFF_V_DIM
RMS_EPS = 1e-6
LOG2E = 1.4426950408889634

SUBLANES = 8
LANES = 128
NORM_ROWS = 1024
FFT_L2 = 128
FFT_KB = 8
HGRN_BLOCK = 128
HGRN_GROUP = 16
OUT_ROW_SPLIT = 2
IN_TILE = 1024
OUT_TILE_ROWS = 512
MIB = 1024 * 1024
VMEM_LIMIT_MIB = 52


def _params(semantics):
    return pltpu.CompilerParams(dimension_semantics=semantics,
                                vmem_limit_bytes=VMEM_LIMIT_MIB * MIB)


def _silu(x):
    return x * jax.nn.sigmoid(x)


def _dot_nt(a, b):
    return lax.dot_general(a, b, (((1,), (1,)), ((), ())), preferred_element_type=F32)


def _mm_kernel(h_ref, w_ref, o_ref, *, out_scale):
    r = jnp.dot(h_ref[...], w_ref[...], preferred_element_type=F32)
    if out_scale != 1.0:
        r = r * out_scale
    o_ref[...] = r.astype(o_ref.dtype)


def _mm_prenorm_kernel(x_ref, g_ref, w_ref, o_ref):
    x = x_ref[...]
    inv = lax.rsqrt(jnp.mean(x * x, axis=-1, keepdims=True) + RMS_EPS)
    r = jnp.dot((x * g_ref[...]).astype(BF16), w_ref[...], preferred_element_type=F32)
    o_ref[...] = (r * inv).astype(o_ref.dtype)


def _mm_norm_kernel(h_ref, w_ref, o_ref, n_ref, *, out_scale):
    r = jnp.dot(h_ref[...], w_ref[...], preferred_element_type=F32) * out_scale
    o = r.astype(o_ref.dtype)
    o_ref[...] = o
    of = o.astype(F32)
    sq = of * of
    for g in range(sq.shape[1] // LANES):
        ss = jnp.sum(sq[:, g * LANES:(g + 1) * LANES], axis=1, keepdims=True)
        n_ref[0, 0, g:g + 1, :] = jnp.broadcast_to(jnp.max(ss, axis=0, keepdims=True), (1, LANES))


def _mm_t_kernel(h_ref, w_ref, o_ref, r_ref):
    r_ref[...] = jnp.dot(h_ref[...], w_ref[...], preferred_element_type=F32)
    o_ref[...] = r_ref[...].T.astype(o_ref.dtype)


def _matmul_cols(h, w, col_off, n_cols, transpose_out=False, out_scale=1.0, row_norms=False,
                 prenorm_gain=None, tm=IN_TILE, tn=IN_TILE):
    t, k = h.shape
    tm = min(tm, t)
    off = col_off // tn
    if prenorm_gain is not None:
        assert not transpose_out and not row_norms and out_scale == 1.0
        return pl.pallas_call(
            _mm_prenorm_kernel,
            grid=(n_cols // tn, t // tm),
            in_specs=[pl.BlockSpec((tm, k), lambda n, m: (m, 0)),
                      pl.BlockSpec((1, k), lambda n, m: (0, 0)),
                      pl.BlockSpec((k, tn), lambda n, m: (0, n + off))],
            out_specs=pl.BlockSpec((tm, tn), lambda n, m: (m, n)),
            out_shape=jax.ShapeDtypeStruct((t, n_cols), BF16),
            compiler_params=_params(("parallel", "parallel")),
            name="in_proj_prenorm",
        )(h, prenorm_gain.reshape(1, k).astype(F32), w)
    if row_norms:
        assert not transpose_out and tn // LANES == SUBLANES
        out, nrm = pl.pallas_call(
            functools.partial(_mm_norm_kernel, out_scale=out_scale),
            grid=(n_cols // tn, t // tm),
            in_specs=[pl.BlockSpec((tm, k), lambda n, m: (m, 0)),
                      pl.BlockSpec((k, tn), lambda n, m: (0, n + off))],
            out_specs=[pl.BlockSpec((tm, tn), lambda n, m: (m, n)),
                       pl.BlockSpec((1, 1, SUBLANES, LANES), lambda n, m: (n, m, 0, 0))],
            out_shape=[jax.ShapeDtypeStruct((t, n_cols), BF16),
                       jax.ShapeDtypeStruct((n_cols // tn, t // tm, SUBLANES, LANES), F32)],
            compiler_params=_params(("parallel", "parallel")),
            name="in_proj_norms",
        )(h, w)
        nrm = jnp.sqrt(nrm[:, :, :, 0]).transpose(1, 0, 2).reshape(t // tm, n_cols // LANES)
        return out, nrm
    if transpose_out:
        assert out_scale == 1.0
        body, out_shape, scratch = _mm_t_kernel, (n_cols, t), [pltpu.VMEM((tm, tn), F32)]
        out_spec = pl.BlockSpec((tn, tm), lambda n, m: (n, m))
    else:
        body = functools.partial(_mm_kernel, out_scale=out_scale)
        out_shape, scratch = (t, n_cols), []
        out_spec = pl.BlockSpec((tm, tn), lambda n, m: (m, n))
    return pl.pallas_call(
        body,
        grid=(n_cols // tn, t // tm),
        in_specs=[pl.BlockSpec((tm, k), lambda n, m: (m, 0)),
                  pl.BlockSpec((k, tn), lambda n, m: (0, n + off))],
        out_specs=out_spec,
        out_shape=jax.ShapeDtypeStruct(out_shape, BF16),
        scratch_shapes=scratch,
        compiler_params=_params(("parallel", "parallel")),
        name="in_proj_t" if transpose_out else "in_proj",
    )(h, w)


def _fft_tables(l):
    l1 = l // FFT_L2
    two_pi = 2.0 * math.pi
    k1 = jnp.arange(l1, dtype=jnp.int32)
    a1 = ((k1[:, None] * k1[None, :]) % l1).astype(F32) * (two_pi / l1)
    f1 = jnp.concatenate([jnp.cos(a1), -jnp.sin(a1)], axis=0).astype(BF16)
    k2 = jnp.arange(FFT_L2, dtype=jnp.int32)
    kk = k1[:, None, None] + l1 * k2[None, :, None]
    a2 = ((kk * k2[None, None, :]) % l).astype(F32) * (two_pi / l)
    c2, s2 = jnp.cos(a2), jnp.sin(a2)
    g2 = jnp.concatenate([jnp.concatenate([c2, s2], axis=2),
                          jnp.concatenate([-s2, c2], axis=2)], axis=1).astype(BF16)
    c = jnp.arange(FNET_GROUP_DIM, dtype=jnp.int32)
    a3 = ((c[:, None] * c[None, :]) % FNET_GROUP_DIM).astype(F32) * (two_pi / FNET_GROUP_DIM)
    scale = 1.0 / math.sqrt(l * FNET_GROUP_DIM)
    cs = (jnp.concatenate([jnp.cos(a3), jnp.sin(a3)], axis=0) * scale).astype(BF16)
    return f1, g2, cs


def _fft1_kernel(f_ref, u_ref, t_ref):
    l1 = u_ref.shape[1]
    r = jnp.dot(f_ref[...], u_ref[0], preferred_element_type=F32)
    t_ref[0, 0] = r[:l1].astype(t_ref.dtype)
    t_ref[1, 0] = r[l1:].astype(t_ref.dtype)


def _fft2_kernel(t_ref, g_ref, cs_ref, o_ref, p_scr):
    kb = g_ref.shape[0]
    l2 = FFT_L2
    gd = FNET_GROUP_DIM
    for j in range(kb):
        gm = g_ref[j]
        for g in range(FNET_GROUPS):
            cols = slice(g * gd, (g + 1) * gd)
            rhs = jnp.concatenate([t_ref[0, 0, j, :, cols], t_ref[1, 0, j, :, cols]], axis=0)
            p = jnp.dot(gm, rhs, preferred_element_type=F32)
            r0 = (j * FNET_GROUPS + g) * l2
            p_scr[r0:r0 + l2, 0:gd] = p[:l2].astype(p_scr.dtype)
            p_scr[r0:r0 + l2, gd:2 * gd] = p[l2:].astype(p_scr.dtype)
    y = jnp.dot(p_scr[...], cs_ref[...], preferred_element_type=F32)
    for j in range(kb):
        for g in range(FNET_GROUPS):
            r0 = (j * FNET_GROUPS + g) * l2
            cols = slice(j * D_FNET + g * gd, j * D_FNET + (g + 1) * gd)
            o_ref[0, :, cols] = y[r0:r0 + l2].astype(o_ref.dtype)


def _fnet_mix(u, b, l):
    l1, l2 = l // FFT_L2, FFT_L2
    f1, g2, cs = _fft_tables(l)
    wcols = l2 * D_FNET
    w = min(wcols, (2 * MIB) // (2 * l1))
    t = pl.pallas_call(
        _fft1_kernel,
        grid=(b, wcols // w),
        in_specs=[pl.BlockSpec((2 * l1, l1), lambda i, j: (0, 0)),
                  pl.BlockSpec((1, l1, w), lambda i, j: (i, 0, j))],
        out_specs=pl.BlockSpec((2, 1, l1, w), lambda i, j: (0, i, 0, j)),
        out_shape=jax.ShapeDtypeStruct((2, b, l1, wcols), BF16),
        compiler_params=_params(("parallel", "parallel")),
        name="fnet_stage1",
    )(f1, u.reshape(b, l1, wcols))
    kb = min(FFT_KB, l1)
    y = pl.pallas_call(
        _fft2_kernel,
        grid=(b, l1 // kb),
        in_specs=[pl.BlockSpec((2, 1, kb, l2, D_FNET), lambda i, j: (0, i, j, 0, 0)),
                  pl.BlockSpec((kb, 2 * l2, 2 * l2), lambda i, j: (j, 0, 0)),
                  pl.BlockSpec((2 * FNET_GROUP_DIM, FNET_GROUP_DIM), lambda i, j: (0, 0))],
        out_specs=pl.BlockSpec((1, l2, kb * D_FNET), lambda i, j: (i, 0, j)),
        out_shape=jax.ShapeDtypeStruct((b, l2, l1 * D_FNET), BF16),
        scratch_shapes=[pltpu.VMEM((kb * FNET_GROUPS * l2, 2 * FNET_GROUP_DIM), BF16)],
        compiler_params=_params(("parallel", "parallel")),
        name="fnet_stage2",
    )(t.reshape(2, b, l1, l2, D_FNET), g2, cs)
    return y


def _hgrn_kernel(q_ref, i_ref, ff_ref, fb_ref, gate_ref, lbl_ref, gn_ref, o_ref, acc_ref, qs_ref):
    hb = HGRN_BLOCK
    half = hb // 2
    n_blocks = q_ref.shape[0] // hb
    group = min(HGRN_GROUP, n_blocks)
    dk = HGRN_HEAD_DIM
    row = lax.broadcasted_iota(jnp.int32, (hb, hb), 0)
    col = lax.broadcasted_iota(jnp.int32, (hb, hb), 1)
    row2 = lax.broadcasted_iota(jnp.int32, (hb, 2 * hb), 0)
    col2 = lax.broadcasted_iota(jnp.int32, (hb, 2 * hb), 1) & (hb - 1)

    logits = lbl_ref[...]
    mx = jnp.max(logits, axis=1, keepdims=True)
    ex = jnp.exp(logits - mx)
    lb_all = ex[:, 0, :] / jnp.sum(ex, axis=1)
    gn = gn_ref[...]

    def direction(f_ref, lb, forward):
        keep = (col <= row) if forward else (col >= row)
        tri2 = jnp.where((col2 <= row2) if forward else (col2 >= row2), 1.0, 0.0).astype(BF16)

        def body(step, state_t):
            gi = step if forward else n_blocks // group - 1 - step
            base = gi * (group * hb)
            gates, pieces = [], []
            for u in range(group):
                rows = pl.ds(pl.multiple_of(base + u * hb, hb), hb)
                f = lb + (1.0 - lb) * jax.nn.sigmoid(f_ref[rows, :].astype(F32))
                g = jnp.log2(f)
                hi = g.astype(BF16)
                pieces.append((hi, (g - hi.astype(F32)).astype(BF16)))
                gates.append(1.0 - f)
            stacked = jnp.concatenate([jnp.concatenate([p[0] for p in pieces], axis=1),
                                       jnp.concatenate([p[1] for p in pieces], axis=1)], axis=0)
            a_all = jnp.dot(tri2, stacked, preferred_element_type=F32)
            units = []
            for u in range(group):
                rows = pl.ds(pl.multiple_of(base + u * hb, hb), hb)
                if forward:
                    q = _silu(q_ref[rows, :].astype(F32))
                    qs_ref[rows, :] = q
                else:
                    q = qs_ref[rows, :]
                v = i_ref[rows, :]
                k = gates[u]
                a = a_all[:, u * dk:(u + 1) * dk]
                ref = a[half - 1:half, :] if forward else a[half:half + 1, :]
                end = a[hb - 1:hb, :] if forward else a[0:1, :]
                qt = q * jnp.exp2(a - ref)
                kt = k * jnp.exp2(ref - a)
                scores = jnp.where(keep, _dot_nt(qt.astype(BF16), kt.astype(BF16)), 0.0)
                o_intra = jnp.dot(scores.astype(BF16), v, preferred_element_type=F32)
                q_in = (qt * jnp.exp2(ref)).astype(BF16)
                k_end = (kt * jnp.exp2(end - ref)).astype(BF16)
                kv_t = lax.dot_general(v, k_end, (((0,), (0,)), ((), ())),
                                       preferred_element_type=F32)
                units.append((rows, o_intra, q_in, kv_t, jnp.exp2(end)))
            for rows, o_intra, q_in, kv_t, decay in (units if forward else units[::-1]):
                o = o_intra + _dot_nt(q_in, state_t.astype(BF16))
                state_t = state_t * decay + kv_t
                if forward:
                    acc_ref[rows, :] = o
                else:
                    tot = acc_ref[rows, :] + o
                    inv = lax.rsqrt(jnp.mean(tot * tot, axis=-1, keepdims=True) + RMS_EPS)
                    gate = gate_ref[rows, :].astype(F32)
                    o_ref[rows, :] = (tot * inv * gn * _silu(gate)).astype(o_ref.dtype)
            return state_t

        lax.fori_loop(0, n_blocks // group, body, jnp.zeros((dk, dk), F32))

    direction(ff_ref, lb_all[0:1, :], True)
    direction(fb_ref, lb_all[1:2, :], False)


def _hgrn_mix_gated(hg, lb_logits, g_norm, b, l):
    nh, dk = HGRN_HEADS, HGRN_HEAD_DIM

    def col(block):
        return pl.BlockSpec((l, dk), lambda i, h: (i, block * nh + h))

    return pl.pallas_call(
        _hgrn_kernel,
        grid=(b, nh),
        in_specs=[col(0), col(1), col(2), col(3), col(4),
                  pl.BlockSpec((2, lb_logits.shape[1], dk), lambda i, h: (0, 0, h)),
                  pl.BlockSpec((1, dk), lambda i, h: (0, 0))],
        out_specs=pl.BlockSpec((l, dk), lambda i, h: (i, h)),
        out_shape=jax.ShapeDtypeStruct((b * l, D_HGRN), BF16),
        scratch_shapes=[pltpu.VMEM((l, dk), F32), pltpu.VMEM((l, dk), F32)],
        compiler_params=_params(("parallel", "parallel")),
        name="hgrn2",
    )(hg, hg, hg, hg, hg, lb_logits.astype(F32), g_norm.reshape(1, dk).astype(F32))


ATTN_TQ = 256
ATTN_TK = 512
ATTN_Q_TILES = 8
ATTN_QK_SCALE = math.sqrt(DIFF_HEAD_DIM ** -0.5 * LOG2E)
ATTN_UNROLL_PAIRS = (8, 2, 1)
ATTN_SKIP_MARGIN = 140.0
ATTN_BOUND_SLACK = 1.0 + 2.0 ** -6
ATTN_M_INIT = -1e30
ATTN_FILLER_CONST = -3e38


def _attn_kernel(slopes_ref, nrm_ref, q_ref, k_ref, vt_ref, gate_ref, lq1_ref, lk1_ref, lq2_ref,
                 lk2_ref, subln_ref, o_ref, qbd_ref, d0_ref, ta_ref, tb_ref, m_ref, l_ref, acc_ref,
                 steps_ref, *, lambda_init):
    tq, tk, nqt = ATTN_TQ, ATTN_TK, ATTN_Q_TILES
    n_blocks = k_ref.shape[0] // tk
    n_steps = n_blocks * nqt
    dh = DIFF_HEAD_DIM
    h = pl.program_id(1)
    sup = pl.program_id(2)
    slope2 = slopes_ref[h] * LOG2E
    kk = lax.broadcasted_iota(jnp.int32, (tk, tq), 0)
    qq = lax.broadcasted_iota(jnp.int32, (tk, tq), 1)
    d0 = (kk - qq).astype(F32) * slope2
    d0_ref[0] = d0
    d0_ref[1] = -d0
    for r in range(tk // tq):
        d0_ref[2 + r] = -jnp.abs(d0 - (r * tq) * slope2)

    zeros = jnp.zeros((tq, dh), BF16)
    for t in range(nqt):
        rows = slice(t * tq, (t + 1) * tq)
        qbd_ref[t, 0:tq, 0:dh] = q_ref[rows, 0:dh]
        qbd_ref[t, 0:tq, dh:2 * dh] = zeros
        qbd_ref[t, tq:2 * tq, 0:dh] = zeros
        qbd_ref[t, tq:2 * tq, dh:2 * dh] = q_ref[rows, dh:2 * dh]

    m_ref[...] = jnp.full(m_ref.shape, ATTN_M_INIT, F32)
    l_ref[...] = jnp.zeros(l_ref.shape, F32)
    acc_ref[...] = jnp.zeros(acc_ref.shape, F32)

    seq_tiles = k_ref.shape[0] // NORM_ROWS
    groups = 2 * D_ATTN // LANES
    tile0 = pl.program_id(0) * seq_tiles
    qk_bound = None
    for c in range(2):
        qn = None
        for r in range(nqt * tq // NORM_ROWS):
            v = nrm_ref[(tile0 + sup * (nqt * tq // NORM_ROWS) + r) * groups + 2 * h + c]
            qn = v if qn is None else jnp.maximum(qn, v)
        kn = None
        for r in range(seq_tiles):
            v = nrm_ref[(tile0 + r) * groups + groups // 2 + 2 * h + c]
            kn = v if kn is None else jnp.maximum(kn, v)
        qk_bound = qn * kn if qk_bound is None else jnp.maximum(qk_bound, qn * kn)
    skip_thr = ATTN_SKIP_MARGIN + 2.0 * ATTN_BOUND_SLACK * qk_bound

    def offsets(s):
        t = s % nqt
        rel = (sup * nqt + t) * tq - (s // nqt) * tk
        return t, rel, rel >= tk, rel <= -tq

    def list_step(s, n):
        t, rel, before, after = offsets(s)
        gap = jnp.where(before, rel - tk + 1, jnp.where(after, -rel - tq + 1, 0))
        steps_ref[n] = s
        return n + (gap.astype(F32) * slope2 < skip_thr).astype(jnp.int32)

    n_listed = lax.fori_loop(0, n_steps, list_step, 0, unroll=8)
    n_pairs = (n_listed + 1) // 2
    for extra in range(2):
        steps_ref[n_listed + extra] = n_steps

    def split(i):
        code = steps_ref[i]
        filler = code >= n_steps
        s = jnp.where(filler, 0, code)
        k0 = pl.multiple_of((s // nqt) * tk, tk)
        t, rel, before, after = offsets(s)
        idx = jnp.where(before, 0, jnp.where(after, 1, 2 + rel // tq))
        lin = rel.astype(F32) * slope2
        const = jnp.where(before, -lin, jnp.where(after, lin, 0.0))
        return k0, t, idx, jnp.where(filler, ATTN_FILLER_CONST, const)

    def scores(s, t_ref):
        k0, t, idx, _ = split(s)
        bias = d0_ref[idx]
        sc = _dot_nt(k_ref[pl.ds(k0, tk), :], qbd_ref[t])
        t_ref[:, 0:tq] = sc[:, 0:tq] + bias
        t_ref[:, tq:2 * tq] = sc[:, tq:2 * tq] + bias

    def accumulate(s, t_ref):
        k0, t, _, const = split(s)
        sc = t_ref[...]
        m_old = m_ref[t]
        m_new = jnp.maximum(m_old, jnp.max(sc, axis=0, keepdims=True) + const)
        alpha = jnp.exp2(m_old - m_new)
        p = jnp.exp2(sc - (m_new - const))
        l_ref[t] = alpha * l_ref[t] + jnp.sum(p, axis=0, keepdims=True)
        acc_ref[t] = alpha * acc_ref[t] + jnp.dot(vt_ref[:, pl.ds(k0, tk)], p.astype(BF16),
                                                  preferred_element_type=F32)
        m_ref[t] = m_new

    scores(0, ta_ref)

    def pair(i):
        scores(2 * i + 1, tb_ref)
        accumulate(2 * i, ta_ref)
        scores(2 * i + 2, ta_ref)
        accumulate(2 * i + 1, tb_ref)

    done = 0
    for width in ATTN_UNROLL_PAIRS:
        def body(c, carry, width=width, done=done):
            for u in range(width):
                pair(done + c * width + u)
            return carry

        trips = (n_pairs - done) // width
        lax.fori_loop(0, trips, body, 0)
        done = done + trips * width

    lam = (jnp.exp(jnp.sum(lq1_ref[...] * lk1_ref[...], keepdims=True))
           - jnp.exp(jnp.sum(lq2_ref[...] * lk2_ref[...], keepdims=True)) + lambda_init)
    gain = subln_ref[...] * (1.0 - lambda_init)
    for t in range(nqt):
        rows = slice(t * tq, (t + 1) * tq)
        r = 1.0 / l_ref[t]
        o_t = (acc_ref[t, :, 0:tq] * r[:, 0:tq]
               - acc_ref[t, :, tq:2 * tq] * (lam * r[:, tq:2 * tq]))
        inv = lax.rsqrt(jnp.mean(o_t * o_t, axis=0, keepdims=True) + RMS_EPS)
        o_ref[rows, :] = ((o_t * inv).T * (gain * _silu(gate_ref[rows, :].astype(F32)))
                          ).astype(o_ref.dtype)


def _diff_attention_gated(qk, qk_norms, v_t, gate, lq1, lk1, lq2, lk2, subln, lambda_init, b, l):
    nh, dv = DIFF_HEADS, DIFF_V_DIM
    tq, tk, nqt = ATTN_TQ, ATTN_TK, ATTN_Q_TILES
    qs = tq * nqt
    ns = l // qs
    assert l % qs == 0 and qs % NORM_ROWS == 0 and l % NORM_ROWS == 0
    slopes = jnp.exp2(-8.0 * (jnp.arange(nh, dtype=F32) + 1.0) / nh)
    vec = lambda a: a.reshape(1, -1).astype(F32)
    small = lambda n: pl.BlockSpec((1, n), lambda i, h, s, *_: (0, 0))
    grid_spec = pltpu.PrefetchScalarGridSpec(
        num_scalar_prefetch=2,
        grid=(b, nh, ns),
        in_specs=[pl.BlockSpec((qs, dv), lambda i, h, s, *_: (i * ns + s, h)),
                  pl.BlockSpec((l, dv), lambda i, h, s, *_: (i, nh + h)),
                  pl.BlockSpec((dv, l), lambda i, h, s, *_: (h, i)),
                  pl.BlockSpec((qs, dv), lambda i, h, s, *_: (i * ns + s, h)),
                  small(DIFF_HEAD_DIM), small(DIFF_HEAD_DIM), small(DIFF_HEAD_DIM),
                  small(DIFF_HEAD_DIM), small(dv)],
        out_specs=pl.BlockSpec((qs, dv), lambda i, h, s, *_: (i * ns + s, h)),
        scratch_shapes=[pltpu.VMEM((nqt, 2 * tq, 2 * DIFF_HEAD_DIM), BF16),
                        pltpu.VMEM((2 + tk // tq, tk, tq), F32),
                        pltpu.VMEM((tk, 2 * tq), F32),
                        pltpu.VMEM((tk, 2 * tq), F32),
                        pltpu.VMEM((nqt, 1, 2 * tq), F32), pltpu.VMEM((nqt, 1, 2 * tq), F32),
                        pltpu.VMEM((nqt, dv, 2 * tq), F32),
                        pltpu.SMEM((l // tk * nqt + 8,), jnp.int32)],
    )
    return pl.pallas_call(
        functools.partial(_attn_kernel, lambda_init=lambda_init),
        grid_spec=grid_spec,
        out_shape=jax.ShapeDtypeStruct((b * l, D_ATTN), BF16),
        compiler_params=_params(("parallel", "parallel", "parallel")),
        name="diff_attention",
    )(slopes, qk_norms.reshape(-1), qk, qk, v_t, gate, vec(lq1), vec(lk1), vec(lq2), vec(lk2), vec(subln))


def _out_kernel(*refs, n_act, gated, with_next, fft_l1):
    refs = list(refs)
    act_refs = [refs.pop(0) for _ in range(n_act)]
    gate_ref = refs.pop(0) if gated else None
    w_refs = [refs.pop(0) for _ in range(n_act)]
    x_ref, g_ref = refs.pop(0), refs.pop(0)
    gn_ref = refs.pop(0) if with_next else None
    o_ref = refs.pop(0)
    h_ref = refs.pop(0) if with_next else None
    tm = x_ref.shape[0]
    if fft_l1:
        rows_ref = refs.pop(0)
        width = act_refs[0].shape[2] // fft_l1
        for k1 in range(fft_l1):
            blk = act_refs[0][0, :, k1 * width:(k1 + 1) * width].astype(F32)
            for t in range(width // LANES):
                rows_ref[t, pl.ds(k1, tm // fft_l1, stride=fft_l1), :] = blk[:, t * LANES:(t + 1) * LANES]
    sub = tm // OUT_ROW_SPLIT
    for part in range(OUT_ROW_SPLIT):
        rows = slice(part * sub, (part + 1) * sub)
        acts = [a_ref[rows, :] for a_ref in act_refs[1 if fft_l1 else 0:]]
        if fft_l1:
            acts.insert(0, jnp.concatenate([rows_ref[t, rows, :] for t in range(rows_ref.shape[0])],
                                           axis=1))
        if gated:
            acts[0] = (acts[0].astype(F32) * _silu(gate_ref[rows, :].astype(F32))).astype(BF16)
        y = jnp.dot(acts[0], w_refs[0][...], preferred_element_type=F32)
        for a, w_ref in zip(acts[1:], w_refs[1:]):
            y = y + jnp.dot(a, w_ref[...], preferred_element_type=F32)
        inv = lax.rsqrt(jnp.mean(y * y, axis=-1, keepdims=True) + RMS_EPS)
        out = x_ref[rows, :] + y * inv * g_ref[...]
        o_ref[rows, :] = out
        if with_next:
            inv_n = lax.rsqrt(jnp.mean(out * out, axis=-1, keepdims=True) + RMS_EPS)
            h_ref[rows, :] = (out * inv_n * gn_ref[...]).astype(BF16)


def _out_proj_residual(acts, ws, x, g, gate=None, g_next=None, fft_l1=0, tm=OUT_TILE_ROWS):
    t, d = x.shape
    n_act = len(acts)
    row = lambda n: pl.BlockSpec((tm, n), lambda i: (i, 0))
    vec = lambda: pl.BlockSpec((1, d), lambda i: (0, 0))
    args = list(acts)
    in_specs = [row(a.shape[-1]) for a in acts]
    scratch = []
    if fft_l1:
        per_seq = acts[0].shape[1] * fft_l1 // tm
        in_specs[0] = pl.BlockSpec((1, tm // fft_l1, acts[0].shape[2]),
                                   lambda i: (i // per_seq, i % per_seq, 0))
        width = acts[0].shape[2] // fft_l1
        scratch = [pltpu.VMEM((width // LANES, tm, LANES), F32)]
    if gate is not None:
        args.append(gate)
        in_specs.append(row(gate.shape[1]))
    args += list(ws) + [x, g.reshape(1, d).astype(F32)]
    in_specs += [pl.BlockSpec(w.shape, lambda i: (0, 0)) for w in ws] + [row(d), vec()]
    out_specs, out_shape = row(d), jax.ShapeDtypeStruct((t, d), F32)
    if g_next is not None:
        args.append(g_next.reshape(1, d).astype(F32))
        in_specs.append(vec())
        out_specs = [out_specs, row(d)]
        out_shape = [out_shape, jax.ShapeDtypeStruct((t, d), BF16)]
    return pl.pallas_call(
        functools.partial(_out_kernel, n_act=n_act, gated=gate is not None,
                          with_next=g_next is not None, fft_l1=fft_l1),
        grid=(t // tm,),
        in_specs=in_specs,
        out_specs=out_specs,
        out_shape=out_shape,
        scratch_shapes=scratch,
        compiler_params=_params(("parallel",)),
        name="out_proj_residual",
    )(*args)


def _trunk(x3, wts):
    b, l, d = x3.shape
    x = x3.reshape(b * l, d)

    w_in, g_pre = wts["ev_w_in"], wts["ev_norm_pre"]
    u = _matmul_cols(x, w_in, 0, D_FNET, prenorm_gain=g_pre)
    gate_a = _matmul_cols(x, w_in, D_FNET, D_FNET, prenorm_gain=g_pre)
    hg = _matmul_cols(x, w_in, 2 * D_FNET, 5 * D_HGRN, prenorm_gain=g_pre)
    y_a = _fnet_mix(u, b, l)
    y_b = _hgrn_mix_gated(hg, wts["hgrn_lb_logits"], wts["hgrn_norm"], b, l)
    w_out = wts["ev_w_out"]
    l1 = l // FFT_L2
    tm_out = OUT_TILE_ROWS
    if (tm_out // l1) % (2 * SUBLANES) == 0:
        fft_l1 = l1
    else:
        fft_l1, y_a = 0, y_a.reshape(b * l, D_FNET)
    x, h = _out_proj_residual([y_a, y_b], [w_out[:D_FNET], w_out[D_FNET:]], x, wts["ev_norm_post"],
                              gate=gate_a, g_next=wts["od_norm_pre"], fft_l1=fft_l1, tm=tm_out)

    lambda_init = 0.8 - 0.6 * math.exp(-0.3 * 1)
    w_in = wts["od_w_in"]
    qk, qk_norms = _matmul_cols(h, w_in, 0, 2 * D_ATTN, out_scale=ATTN_QK_SCALE, row_norms=True,
                                tm=NORM_ROWS)
    v_t = _matmul_cols(h, w_in, 2 * D_ATTN, D_ATTN, transpose_out=True)
    gate = _matmul_cols(h, w_in, 3 * D_ATTN, D_ATTN)
    o = _diff_attention_gated(qk, qk_norms, v_t, gate, wts["lambda_q1"], wts["lambda_k1"], wts["lambda_q2"],
                              wts["lambda_k2"], wts["subln"], lambda_init, b, l)
    x = _out_proj_residual([o], [wts["od_w_out"]], x, wts["od_norm_post"])
    return x.reshape(b, l, d)


def kernel(x_prompt, x_sample, ev_w_in, ev_w_out, ev_norm_pre, ev_norm_post, hgrn_lb_logits,
           hgrn_norm, od_w_in, od_w_out, od_norm_pre, od_norm_post,
           lambda_q1, lambda_k1, lambda_q2, lambda_k2, subln):
    wts = {
        "ev_w_in": ev_w_in[0].astype(BF16), "ev_w_out": ev_w_out[0].astype(BF16),
        "ev_norm_pre": ev_norm_pre[0], "ev_norm_post": ev_norm_post[0],
        "hgrn_lb_logits": hgrn_lb_logits, "hgrn_norm": hgrn_norm[0],
        "od_w_in": od_w_in[0].astype(BF16), "od_w_out": od_w_out[0].astype(BF16),
        "od_norm_pre": od_norm_pre[0], "od_norm_post": od_norm_post[0],
        "lambda_q1": lambda_q1[0], "lambda_k1": lambda_k1[0],
        "lambda_q2": lambda_q2[0], "lambda_k2": lambda_k2[0], "subln": subln[0],
    }
    return (_trunk(x_prompt, wts), _trunk(x_sample, wts))
```

```python
import functools
import math

import jax
import jax.numpy as jnp
from jax import lax
from jax.experimental import pallas as pl
from jax.experimental.pallas import tpu as pltpu

F32 = jnp.float32
BF16 = jnp.bfloat16

D_MODEL = 2048
D_FNET = 1024
FNET_GROUP_DIM = 256
FNET_GROUPS = D_FNET // FNET_GROUP_DIM
D_HGRN = 1024
HGRN_HEAD_DIM = 128
HGRN_HEADS = D_HGRN // HGRN_HEAD_DIM
DIFF_HEADS = 8
DIFF_HEAD_DIM = 128
DIFF_V_DIM = 256
D_ATTN = DIFF_HEADS * DIFF_V_DIM
RMS_EPS = 1e-6
LOG2E = 1.4426950408889634

SUBLANES = 8
LANES = 128
NORM_ROWS = 1024
FFT_L2 = 128
FFT_KB = 8
HGRN_BLOCK = 128
HGRN_GROUP = 16
OUT_ROW_SPLIT = 2
IN_TILE = 1024
OUT_TILE_ROWS = 512
MIB = 1024 * 1024
VMEM_LIMIT_MIB = 52


def _params(semantics):
    return pltpu.CompilerParams(dimension_semantics=semantics,
                                vmem_limit_bytes=VMEM_LIMIT_MIB * MIB)


def _silu(x):
    return x * jax.nn.sigmoid(x)


def _dot_nt(a, b):
    return lax.dot_general(a, b, (((1,), (1,)), ((), ())), preferred_element_type=F32)


def _mm_kernel(h_ref, w_ref, o_ref, *, out_scale):
    r = jnp.dot(h_ref[...], w_ref[...].astype(BF16), preferred_element_type=F32)
    if out_scale != 1.0:
        r = r * out_scale
    o_ref[...] = r.astype(o_ref.dtype)


def _mm_prenorm_kernel(x_ref, g_ref, w_ref, o_ref):
    x = x_ref[...]
    inv = lax.rsqrt(jnp.mean(x * x, axis=-1, keepdims=True) + RMS_EPS)
    r = jnp.dot((x * g_ref[...]).astype(BF16), w_ref[...].astype(BF16), preferred_element_type=F32)
    o_ref[...] = (r * inv).astype(o_ref.dtype)


def _mm_norm_kernel(h_ref, w_ref, o_ref, n_ref, *, out_scale):
    r = jnp.dot(h_ref[...], w_ref[...].astype(BF16), preferred_element_type=F32) * out_scale
    o = r.astype(o_ref.dtype)
    o_ref[...] = o
    of = o.astype(F32)
    sq = of * of
    for g in range(sq.shape[1] // LANES):
        ss = jnp.sum(sq[:, g * LANES:(g + 1) * LANES], axis=1, keepdims=True)
        n_ref[0, 0, g:g + 1, :] = jnp.broadcast_to(jnp.max(ss, axis=0, keepdims=True), (1, LANES))


def _mm_t_kernel(h_ref, w_ref, o_ref, r_ref):
    r_ref[...] = jnp.dot(h_ref[...], w_ref[...].astype(BF16), preferred_element_type=F32)
    o_ref[...] = r_ref[...].T.astype(o_ref.dtype)


def _matmul_cols(h, w, col_off, n_cols, transpose_out=False, out_scale=1.0, row_norms=False,
                 prenorm_gain=None, tm=IN_TILE, tn=IN_TILE):
    t, k = h.shape
    tm = min(tm, t)
    off = col_off // tn
    if prenorm_gain is not None:
        assert not transpose_out and not row_norms and out_scale == 1.0
        return pl.pallas_call(
            _mm_prenorm_kernel,
            grid=(n_cols // tn, t // tm),
            in_specs=[pl.BlockSpec((tm, k), lambda n, m: (m, 0)),
                      pl.BlockSpec((1, k), lambda n, m: (0, 0)),
                      pl.BlockSpec((k, tn), lambda n, m: (0, n + off))],
            out_specs=pl.BlockSpec((tm, tn), lambda n, m: (m, n)),
            out_shape=jax.ShapeDtypeStruct((t, n_cols), BF16),
            compiler_params=_params(("parallel", "parallel")),
            name="in_proj_prenorm",
        )(h, prenorm_gain.reshape(1, k).astype(F32), w)
    if row_norms:
        assert not transpose_out and tn // LANES == SUBLANES
        out, nrm = pl.pallas_call(
            functools.partial(_mm_norm_kernel, out_scale=out_scale),
            grid=(n_cols // tn, t // tm),
            in_specs=[pl.BlockSpec((tm, k), lambda n, m: (m, 0)),
                      pl.BlockSpec((k, tn), lambda n, m: (0, n + off))],
            out_specs=[pl.BlockSpec((tm, tn), lambda n, m: (m, n)),
                       pl.BlockSpec((1, 1, SUBLANES, LANES), lambda n, m: (n, m, 0, 0))],
            out_shape=[jax.ShapeDtypeStruct((t, n_cols), BF16),
                       jax.ShapeDtypeStruct((n_cols // tn, t // tm, SUBLANES, LANES), F32)],
            compiler_params=_params(("parallel", "parallel")),
            name="in_proj_norms",
        )(h, w)
        nrm = jnp.sqrt(nrm[:, :, :, 0]).transpose(1, 0, 2).reshape(t // tm, n_cols // LANES)
        return out, nrm
    if transpose_out:
        assert out_scale == 1.0
        body, out_shape, scratch = _mm_t_kernel, (n_cols, t), [pltpu.VMEM((tm, tn), F32)]
        out_spec = pl.BlockSpec((tn, tm), lambda n, m: (n, m))
    else:
        body = functools.partial(_mm_kernel, out_scale=out_scale)
        out_shape, scratch = (t, n_cols), []
        out_spec = pl.BlockSpec((tm, tn), lambda n, m: (m, n))
    return pl.pallas_call(
        body,
        grid=(n_cols // tn, t // tm),
        in_specs=[pl.BlockSpec((tm, k), lambda n, m: (m, 0)),
                  pl.BlockSpec((k, tn), lambda n, m: (0, n + off))],
        out_specs=out_spec,
        out_shape=jax.ShapeDtypeStruct(out_shape, BF16),
        scratch_shapes=scratch,
        compiler_params=_params(("parallel", "parallel")),
        name="in_proj_t" if transpose_out else "in_proj",
    )(h, w)


def _fft_tables(l):
    l1 = l // FFT_L2
    two_pi = 2.0 * math.pi
    k1 = jnp.arange(l1, dtype=jnp.int32)
    a1 = ((k1[:, None] * k1[None, :]) % l1).astype(F32) * (two_pi / l1)
    f1 = jnp.concatenate([jnp.cos(a1), -jnp.sin(a1)], axis=0).astype(BF16)
    k2 = jnp.arange(FFT_L2, dtype=jnp.int32)
    kk = k1[:, None, None] + l1 * k2[None, :, None]
    a2 = ((kk * k2[None, None, :]) % l).astype(F32) * (two_pi / l)
    c2, s2 = jnp.cos(a2), jnp.sin(a2)
    g2 = jnp.concatenate([jnp.concatenate([c2, s2], axis=2),
                          jnp.concatenate([-s2, c2], axis=2)], axis=1).astype(BF16)
    c = jnp.arange(FNET_GROUP_DIM, dtype=jnp.int32)
    a3 = ((c[:, None] * c[None, :]) % FNET_GROUP_DIM).astype(F32) * (two_pi / FNET_GROUP_DIM)
    scale = 1.0 / math.sqrt(l * FNET_GROUP_DIM)
    cs = (jnp.concatenate([jnp.cos(a3), jnp.sin(a3)], axis=0) * scale).astype(BF16)
    return f1, g2, cs


def _fft1_kernel(f_ref, u_ref, t_ref):
    l1 = u_ref.shape[1]
    r = jnp.dot(f_ref[...], u_ref[0], preferred_element_type=F32)
    t_ref[0, 0] = r[:l1].astype(t_ref.dtype)
    t_ref[1, 0] = r[l1:].astype(t_ref.dtype)


def _fft2_kernel(t_ref, g_ref, cs_ref, o_ref, p_scr):
    kb = g_ref.shape[0]
    l2 = FFT_L2
    gd = FNET_GROUP_DIM
    for j in range(kb):
        gm = g_ref[j]
        for g in range(FNET_GROUPS):
            cols = slice(g * gd, (g + 1) * gd)
            rhs = jnp.concatenate([t_ref[0, 0, j, :, cols], t_ref[1, 0, j, :, cols]], axis=0)
            p = jnp.dot(gm, rhs, preferred_element_type=F32)
            r0 = (j * FNET_GROUPS + g) * l2
            p_scr[r0:r0 + l2, 0:gd] = p[:l2].astype(p_scr.dtype)
            p_scr[r0:r0 + l2, gd:2 * gd] = p[l2:].astype(p_scr.dtype)
    y = jnp.dot(p_scr[...], cs_ref[...], preferred_element_type=F32)
    for j in range(kb):
        for g in range(FNET_GROUPS):
            r0 = (j * FNET_GROUPS + g) * l2
            cols = slice(j * D_FNET + g * gd, j * D_FNET + (g + 1) * gd)
            o_ref[0, :, cols] = y[r0:r0 + l2].astype(o_ref.dtype)


def _fnet_mix(u, b, l):
    l1, l2 = l // FFT_L2, FFT_L2
    f1, g2, cs = _fft_tables(l)
    wcols = l2 * D_FNET
    w = min(wcols, (2 * MIB) // (2 * l1))
    t = pl.pallas_call(
        _fft1_kernel,
        grid=(b, wcols // w),
        in_specs=[pl.BlockSpec((2 * l1, l1), lambda i, j: (0, 0)),
                  pl.BlockSpec((1, l1, w), lambda i, j: (i, 0, j))],
        out_specs=pl.BlockSpec((2, 1, l1, w), lambda i, j: (0, i, 0, j)),
        out_shape=jax.ShapeDtypeStruct((2, b, l1, wcols), BF16),
        compiler_params=_params(("parallel", "parallel")),
        name="fnet_stage1",
    )(f1, u.reshape(b, l1, wcols))
    kb = min(FFT_KB, l1)
    y = pl.pallas_call(
        _fft2_kernel,
        grid=(b, l1 // kb),
        in_specs=[pl.BlockSpec((2, 1, kb, l2, D_FNET), lambda i, j: (0, i, j, 0, 0)),
                  pl.BlockSpec((kb, 2 * l2, 2 * l2), lambda i, j: (j, 0, 0)),
                  pl.BlockSpec((2 * FNET_GROUP_DIM, FNET_GROUP_DIM), lambda i, j: (0, 0))],
        out_specs=pl.BlockSpec((1, l2, kb * D_FNET), lambda i, j: (i, 0, j)),
        out_shape=jax.ShapeDtypeStruct((b, l2, l1 * D_FNET), BF16),
        scratch_shapes=[pltpu.VMEM((kb * FNET_GROUPS * l2, 2 * FNET_GROUP_DIM), BF16)],
        compiler_params=_params(("parallel", "parallel")),
        name="fnet_stage2",
    )(t.reshape(2, b, l1, l2, D_FNET), g2, cs)
    return y


def _hgrn_kernel(q_ref, i_ref, ff_ref, fb_ref, gate_ref, lbl_ref, gn_ref, o_ref, acc_ref, qs_ref):
    hb = HGRN_BLOCK
    half = hb // 2
    n_blocks = q_ref.shape[0] // hb
    group = min(HGRN_GROUP, n_blocks)
    dk = HGRN_HEAD_DIM
    row = lax.broadcasted_iota(jnp.int32, (hb, hb), 0)
    col = lax.broadcasted_iota(jnp.int32, (hb, hb), 1)
    row2 = lax.broadcasted_iota(jnp.int32, (hb, 2 * hb), 0)
    col2 = lax.broadcasted_iota(jnp.int32, (hb, 2 * hb), 1) & (hb - 1)

    logits = lbl_ref[...]
    mx = jnp.max(logits, axis=1, keepdims=True)
    ex = jnp.exp(logits - mx)
    lb_all = ex[:, 0, :] / jnp.sum(ex, axis=1)
    gn = gn_ref[...]

    def direction(f_ref, lb, forward):
        keep = (col <= row) if forward else (col >= row)
        tri2 = jnp.where((col2 <= row2) if forward else (col2 >= row2), 1.0, 0.0).astype(BF16)

        def body(step, state_t):
            gi = step if forward else n_blocks // group - 1 - step
            base = gi * (group * hb)
            gates, pieces = [], []
            for u in range(group):
                rows = pl.ds(pl.multiple_of(base + u * hb, hb), hb)
                f = lb + (1.0 - lb) * jax.nn.sigmoid(f_ref[rows, :].astype(F32))
                g = jnp.log2(f)
                hi = g.astype(BF16)
                pieces.append((hi, (g - hi.astype(F32)).astype(BF16)))
                gates.append(1.0 - f)
            stacked = jnp.concatenate([jnp.concatenate([p[0] for p in pieces], axis=1),
                                       jnp.concatenate([p[1] for p in pieces], axis=1)], axis=0)
            a_all = jnp.dot(tri2, stacked, preferred_element_type=F32)
            units = []
            for u in range(group):
                rows = pl.ds(pl.multiple_of(base + u * hb, hb), hb)
                if forward:
                    q = _silu(q_ref[rows, :].astype(F32))
                    qs_ref[rows, :] = q
                else:
                    q = qs_ref[rows, :]
                v = i_ref[rows, :]
                k = gates[u]
                a = a_all[:, u * dk:(u + 1) * dk]
                ref = a[half - 1:half, :] if forward else a[half:half + 1, :]
                end = a[hb - 1:hb, :] if forward else a[0:1, :]
                qt = q * jnp.exp2(a - ref)
                kt = k * jnp.exp2(ref - a)
                scores = jnp.where(keep, _dot_nt(qt.astype(BF16), kt.astype(BF16)), 0.0)
                o_intra = jnp.dot(scores.astype(BF16), v, preferred_element_type=F32)
                q_in = (qt * jnp.exp2(ref)).astype(BF16)
                k_end = (kt * jnp.exp2(end - ref)).astype(BF16)
                kv_t = lax.dot_general(v, k_end, (((0,), (0,)), ((), ())),
                                       preferred_element_type=F32)
                units.append((rows, o_intra, q_in, kv_t, jnp.exp2(end)))
            for rows, o_intra, q_in, kv_t, decay in (units if forward else units[::-1]):
                o = o_intra + _dot_nt(q_in, state_t.astype(BF16))
                state_t = state_t * decay + kv_t
                if forward:
                    acc_ref[rows, :] = o
                else:
                    tot = acc_ref[rows, :] + o
                    inv = lax.rsqrt(jnp.mean(tot * tot, axis=-1, keepdims=True) + RMS_EPS)
                    gate = gate_ref[rows, :].astype(F32)
                    o_ref[rows, :] = (tot * inv * gn * _silu(gate)).astype(o_ref.dtype)
            return state_t

        lax.fori_loop(0, n_blocks // group, body, jnp.zeros((dk, dk), F32))

    direction(ff_ref, lb_all[0:1, :], True)
    direction(fb_ref, lb_all[1:2, :], False)


def _hgrn_mix_gated(hg, lb_logits, g_norm, b, l):
    nh, dk = HGRN_HEADS, HGRN_HEAD_DIM

    def col(block):
        return pl.BlockSpec((l, dk), lambda i, h: (i, block * nh + h))

    return pl.pallas_call(
        _hgrn_kernel,
        grid=(b, nh),
        in_specs=[col(0), col(1), col(2), col(3), col(4),
                  pl.BlockSpec((2, lb_logits.shape[1], dk), lambda i, h: (0, 0, h)),
                  pl.BlockSpec((1, dk), lambda i, h: (0, 0))],
        out_specs=pl.BlockSpec((l, dk), lambda i, h: (i, h)),
        out_shape=jax.ShapeDtypeStruct((b * l, D_HGRN), BF16),
        scratch_shapes=[pltpu.VMEM((l, dk), F32), pltpu.VMEM((l, dk), F32)],
        compiler_params=_params(("parallel", "parallel")),
        name="hgrn2",
    )(hg, hg, hg, hg, hg, lb_logits.astype(F32), g_norm.reshape(1, dk).astype(F32))


ATTN_TQ = 256
ATTN_TK = 512
ATTN_Q_TILES = 8
ATTN_QK_SCALE = math.sqrt(DIFF_HEAD_DIM ** -0.5 * LOG2E)
ATTN_UNROLL_PAIRS = (8, 2, 1)
ATTN_SKIP_MARGIN = 140.0
ATTN_BOUND_SLACK = 1.0 + 2.0 ** -6
ATTN_M_INIT = -1e30
ATTN_FILLER_CONST = -3e38


def _attn_kernel(slopes_ref, nrm_ref, q_ref, k_ref, vt_ref, gate_ref, lq1_ref, lk1_ref, lq2_ref,
                 lk2_ref, subln_ref, o_ref, qbd_ref, d0_ref, ta_ref, tb_ref, m_ref, l_ref, acc_ref,
                 steps_ref, *, lambda_init):
    tq, tk, nqt = ATTN_TQ, ATTN_TK, ATTN_Q_TILES
    n_blocks = k_ref.shape[0] // tk
    n_steps = n_blocks * nqt
    dh = DIFF_HEAD_DIM
    h = pl.program_id(1)
    sup = pl.program_id(2)
    slope2 = slopes_ref[h] * LOG2E
    kk = lax.broadcasted_iota(jnp.int32, (tk, tq), 0)
    qq = lax.broadcasted_iota(jnp.int32, (tk, tq), 1)
    d0 = (kk - qq).astype(F32) * slope2
    d0_ref[0] = d0
    d0_ref[1] = -d0
    for r in range(tk // tq):
        d0_ref[2 + r] = -jnp.abs(d0 - (r * tq) * slope2)

    zeros = jnp.zeros((tq, dh), BF16)
    for t in range(nqt):
        rows = slice(t * tq, (t + 1) * tq)
        qbd_ref[t, 0:tq, 0:dh] = q_ref[rows, 0:dh]
        qbd_ref[t, 0:tq, dh:2 * dh] = zeros
        qbd_ref[t, tq:2 * tq, 0:dh] = zeros
        qbd_ref[t, tq:2 * tq, dh:2 * dh] = q_ref[rows, dh:2 * dh]

    m_ref[...] = jnp.full(m_ref.shape, ATTN_M_INIT, F32)
    l_ref[...] = jnp.zeros(l_ref.shape, F32)
    acc_ref[...] = jnp.zeros(acc_ref.shape, F32)

    seq_tiles = k_ref.shape[0] // NORM_ROWS
    groups = 2 * D_ATTN // LANES
    tile0 = pl.program_id(0) * seq_tiles
    qk_bound = None
    for c in range(2):
        qn = None
        for r in range(nqt * tq // NORM_ROWS):
            v = nrm_ref[(tile0 + sup * (nqt * tq // NORM_ROWS) + r) * groups + 2 * h + c]
            qn = v if qn is None else jnp.maximum(qn, v)
        kn = None
        for r in range(seq_tiles):
            v = nrm_ref[(tile0 + r) * groups + groups // 2 + 2 * h + c]
            kn = v if kn is None else jnp.maximum(kn, v)
        qk_bound = qn * kn if qk_bound is None else jnp.maximum(qk_bound, qn * kn)
    skip_thr = ATTN_SKIP_MARGIN + 2.0 * ATTN_BOUND_SLACK * qk_bound

    def offsets(s):
        t = s % nqt
        rel = (sup * nqt + t) * tq - (s // nqt) * tk
        return t, rel, rel >= tk, rel <= -tq

    def list_step(s, n):
        t, rel, before, after = offsets(s)
        gap = jnp.where(before, rel - tk + 1, jnp.where(after, -rel - tq + 1, 0))
        steps_ref[n] = s
        return n + (gap.astype(F32) * slope2 < skip_thr).astype(jnp.int32)

    n_listed = lax.fori_loop(0, n_steps, list_step, 0, unroll=8)
    n_pairs = (n_listed + 1) // 2
    for extra in range(2):
        steps_ref[n_listed + extra] = n_steps

    def split(i):
        code = steps_ref[i]
        filler = code >= n_steps
        s = jnp.where(filler, 0, code)
        k0 = pl.multiple_of((s // nqt) * tk, tk)
        t, rel, before, after = offsets(s)
        idx = jnp.where(before, 0, jnp.where(after, 1, 2 + rel // tq))
        lin = rel.astype(F32) * slope2
        const = jnp.where(before, -lin, jnp.where(after, lin, 0.0))
        return k0, t, idx, jnp.where(filler, ATTN_FILLER_CONST, const)

    def scores(s, t_ref):
        k0, t, idx, _ = split(s)
        bias = d0_ref[idx]
        sc = _dot_nt(k_ref[pl.ds(k0, tk), :], qbd_ref[t])
        t_ref[:, 0:tq] = sc[:, 0:tq] + bias
        t_ref[:, tq:2 * tq] = sc[:, tq:2 * tq] + bias

    def accumulate(s, t_ref):
        k0, t, _, const = split(s)
        sc = t_ref[...]
        m_old = m_ref[t]
        m_new = jnp.maximum(m_old, jnp.max(sc, axis=0, keepdims=True) + const)
        alpha = jnp.exp2(m_old - m_new)
        p = jnp.exp2(sc - (m_new - const))
        l_ref[t] = alpha * l_ref[t] + jnp.sum(p, axis=0, keepdims=True)
        acc_ref[t] = alpha * acc_ref[t] + jnp.dot(vt_ref[:, pl.ds(k0, tk)], p.astype(BF16),
                                                  preferred_element_type=F32)
        m_ref[t] = m_new

    scores(0, ta_ref)

    def pair(i):
        scores(2 * i + 1, tb_ref)
        accumulate(2 * i, ta_ref)
        scores(2 * i + 2, ta_ref)
        accumulate(2 * i + 1, tb_ref)

    done = 0
    for width in ATTN_UNROLL_PAIRS:
        def body(c, carry, width=width, done=done):
            for u in range(width):
                pair(done + c * width + u)
            return carry

        trips = (n_pairs - done) // width
        lax.fori_loop(0, trips, body, 0)
        done = done + trips * width

    lam = (jnp.exp(jnp.sum(lq1_ref[...] * lk1_ref[...], keepdims=True))
           - jnp.exp(jnp.sum(lq2_ref[...] * lk2_ref[...], keepdims=True)) + lambda_init)
    gain = subln_ref[...] * (1.0 - lambda_init)
    for t in range(nqt):
        rows = slice(t * tq, (t + 1) * tq)
        r = 1.0 / l_ref[t]
        o_t = (acc_ref[t, :, 0:tq] * r[:, 0:tq]
               - acc_ref[t, :, tq:2 * tq] * (lam * r[:, tq:2 * tq]))
        inv = lax.rsqrt(jnp.mean(o_t * o_t, axis=0, keepdims=True) + RMS_EPS)
        o_ref[rows, :] = ((o_t * inv).T * (gain * _silu(gate_ref[rows, :].astype(F32)))
                          ).astype(o_ref.dtype)


def _diff_attention_gated(qk, qk_norms, v_t, gate, lq1, lk1, lq2, lk2, subln, lambda_init, b, l):
    nh, dv = DIFF_HEADS, DIFF_V_DIM
    tq, tk, nqt = ATTN_TQ, ATTN_TK, ATTN_Q_TILES
    qs = tq * nqt
    ns = l // qs
    assert l % qs == 0 and qs % NORM_ROWS == 0 and l % NORM_ROWS == 0
    slopes = jnp.exp2(-8.0 * (jnp.arange(nh, dtype=F32) + 1.0) / nh)
    vec = lambda a: a.reshape(1, -1).astype(F32)
    small = lambda n: pl.BlockSpec((1, n), lambda i, h, s, *_: (0, 0))
    grid_spec = pltpu.PrefetchScalarGridSpec(
        num_scalar_prefetch=2,
        grid=(b, nh, ns),
        in_specs=[pl.BlockSpec((qs, dv), lambda i, h, s, *_: (i * ns + s, h)),
                  pl.BlockSpec((l, dv), lambda i, h, s, *_: (i, nh + h)),
                  pl.BlockSpec((dv, l), lambda i, h, s, *_: (h, i)),
                  pl.BlockSpec((qs, dv), lambda i, h, s, *_: (i * ns + s, h)),
                  small(DIFF_HEAD_DIM), small(DIFF_HEAD_DIM), small(DIFF_HEAD_DIM),
                  small(DIFF_HEAD_DIM), small(dv)],
        out_specs=pl.BlockSpec((qs, dv), lambda i, h, s, *_: (i * ns + s, h)),
        scratch_shapes=[pltpu.VMEM((nqt, 2 * tq, 2 * DIFF_HEAD_DIM), BF16),
                        pltpu.VMEM((2 + tk // tq, tk, tq), F32),
                        pltpu.VMEM((tk, 2 * tq), F32),
                        pltpu.VMEM((tk, 2 * tq), F32),
                        pltpu.VMEM((nqt, 1, 2 * tq), F32), pltpu.VMEM((nqt, 1, 2 * tq), F32),
                        pltpu.VMEM((nqt, dv, 2 * tq), F32),
                        pltpu.SMEM((l // tk * nqt + 8,), jnp.int32)],
    )
    return pl.pallas_call(
        functools.partial(_attn_kernel, lambda_init=lambda_init),
        grid_spec=grid_spec,
        out_shape=jax.ShapeDtypeStruct((b * l, D_ATTN), BF16),
        compiler_params=_params(("parallel", "parallel", "parallel")),
        name="diff_attention",
    )(slopes, qk_norms.reshape(-1), qk, qk, v_t, gate, vec(lq1), vec(lk1), vec(lq2), vec(lk2), vec(subln))


def _out_kernel(*refs, n_act, gated, with_next, fft_l1):
    refs = list(refs)
    act_refs = [refs.pop(0) for _ in range(n_act)]
    gate_ref = refs.pop(0) if gated else None
    w_refs = [refs.pop(0) for _ in range(n_act)]
    x_ref, g_ref = refs.pop(0), refs.pop(0)
    gn_ref = refs.pop(0) if with_next else None
    o_ref = refs.pop(0)
    h_ref = refs.pop(0) if with_next else None
    tm = x_ref.shape[0]
    if fft_l1:
        rows_ref = refs.pop(0)
        width = act_refs[0].shape[2] // fft_l1
        for k1 in range(fft_l1):
            blk = act_refs[0][0, :, k1 * width:(k1 + 1) * width].astype(F32)
            for t in range(width // LANES):
                rows_ref[t, pl.ds(k1, tm // fft_l1, stride=fft_l1), :] = blk[:, t * LANES:(t + 1) * LANES]
    sub = tm // OUT_ROW_SPLIT
    for part in range(OUT_ROW_SPLIT):
        rows = slice(part * sub, (part + 1) * sub)
        acts = [a_ref[rows, :] for a_ref in act_refs[1 if fft_l1 else 0:]]
        if fft_l1:
            acts.insert(0, jnp.concatenate([rows_ref[t, rows, :] for t in range(rows_ref.shape[0])],
                                           axis=1))
        if gated:
            acts[0] = (acts[0].astype(F32) * _silu(gate_ref[rows, :].astype(F32))).astype(BF16)
        y = jnp.dot(acts[0], w_refs[0][...], preferred_element_type=F32)
        for a, w_ref in zip(acts[1:], w_refs[1:]):
            y = y + jnp.dot(a, w_ref[...], preferred_element_type=F32)
        inv = lax.rsqrt(jnp.mean(y * y, axis=-1, keepdims=True) + RMS_EPS)
        out = x_ref[rows, :] + y * inv * g_ref[...]
        o_ref[rows, :] = out
        if with_next:
            inv_n = lax.rsqrt(jnp.mean(out * out, axis=-1, keepdims=True) + RMS_EPS)
            h_ref[rows, :] = (out * inv_n * gn_ref[...]).astype(BF16)


def _out_proj_residual(acts, ws, x, g, gate=None, g_next=None, fft_l1=0, tm=OUT_TILE_ROWS):
    t, d = x.shape
    n_act = len(acts)
    row = lambda n: pl.BlockSpec((tm, n), lambda i: (i, 0))
    vec = lambda: pl.BlockSpec((1, d), lambda i: (0, 0))
    args = list(acts)
    in_specs = [row(a.shape[-1]) for a in acts]
    scratch = []
    if fft_l1:
        per_seq = acts[0].shape[1] * fft_l1 // tm
        in_specs[0] = pl.BlockSpec((1, tm // fft_l1, acts[0].shape[2]),
                                   lambda i: (i // per_seq, i % per_seq, 0))
        width = acts[0].shape[2] // fft_l1
        scratch = [pltpu.VMEM((width // LANES, tm, LANES), F32)]
    if gate is not None:
        args.append(gate)
        in_specs.append(row(gate.shape[1]))
    args += list(ws) + [x, g.reshape(1, d).astype(F32)]
    in_specs += [pl.BlockSpec(w.shape, lambda i: (0, 0)) for w in ws] + [row(d), vec()]
    out_specs, out_shape = row(d), jax.ShapeDtypeStruct((t, d), F32)
    if g_next is not None:
        args.append(g_next.reshape(1, d).astype(F32))
        in_specs.append(vec())
        out_specs = [out_specs, row(d)]
        out_shape = [out_shape, jax.ShapeDtypeStruct((t, d), BF16)]
    return pl.pallas_call(
        functools.partial(_out_kernel, n_act=n_act, gated=gate is not None,
                          with_next=g_next is not None, fft_l1=fft_l1),
        grid=(t // tm,),
        in_specs=in_specs,
        out_specs=out_specs,
        out_shape=out_shape,
        scratch_shapes=scratch,
        compiler_params=_params(("parallel",)),
        name="out_proj_residual",
    )(*args)


def _trunk(x3, wts):
    b, l, d = x3.shape
    x = x3.reshape(b * l, d)

    w_in, g_pre = wts["ev_w_in"], wts["ev_norm_pre"]
    u = _matmul_cols(x, w_in, 0, D_FNET, prenorm_gain=g_pre)
    gate_a = _matmul_cols(x, w_in, D_FNET, D_FNET, prenorm_gain=g_pre)
    hg = _matmul_cols(x, w_in, 2 * D_FNET, 5 * D_HGRN, prenorm_gain=g_pre)
    y_a = _fnet_mix(u, b, l)
    y_b = _hgrn_mix_gated(hg, wts["hgrn_lb_logits"], wts["hgrn_norm"], b, l)
    w_out = wts["ev_w_out"]
    l1 = l // FFT_L2
    tm_out = OUT_TILE_ROWS
    if (tm_out // l1) % (2 * SUBLANES) == 0:
        fft_l1 = l1
    else:
        fft_l1, y_a = 0, y_a.reshape(b * l, D_FNET)
    x, h = _out_proj_residual([y_a, y_b], [w_out[:D_FNET], w_out[D_FNET:]], x, wts["ev_norm_post"],
                              gate=gate_a, g_next=wts["od_norm_pre"], fft_l1=fft_l1, tm=tm_out)

    lambda_init = 0.8 - 0.6 * math.exp(-0.3 * 1)
    w_in = wts["od_w_in"]
    qk, qk_norms = _matmul_cols(h, w_in, 0, 2 * D_ATTN, out_scale=ATTN_QK_SCALE, row_norms=True,
                                tm=NORM_ROWS)
    v_t = _matmul_cols(h, w_in, 2 * D_ATTN, D_ATTN, transpose_out=True)
    gate = _matmul_cols(h, w_in, 3 * D_ATTN, D_ATTN)
    o = _diff_attention_gated(qk, qk_norms, v_t, gate, wts["lambda_q1"], wts["lambda_k1"], wts["lambda_q2"],
                              wts["lambda_k2"], wts["subln"], lambda_init, b, l)
    x = _out_proj_residual([o], [wts["od_w_out"]], x, wts["od_norm_post"])
    return x.reshape(b, l, d)


def kernel(x_prompt, x_sample, ev_w_in, ev_w_out, ev_norm_pre, ev_norm_post, hgrn_lb_logits,
           hgrn_norm, od_w_in, od_w_out, od_norm_pre, od_norm_post,
           lambda_q1, lambda_k1, lambda_q2, lambda_k2, subln):
    wts = {
        "ev_w_in": ev_w_in[0], "ev_w_out": ev_w_out[0].astype(BF16),
        "ev_norm_pre": ev_norm_pre[0], "ev_norm_post": ev_norm_post[0],
        "hgrn_lb_logits": hgrn_lb_logits, "hgrn_norm": hgrn_norm[0],
        "od_w_in": od_w_in[0], "od_w_out": od_w_out[0].astype(BF16),
        "od_norm_pre": od_norm_pre[0], "od_norm_post": od_norm_post[0],
        "lambda_q1": lambda_q1[0], "lambda_k1": lambda_k1[0],
        "lambda_q2": lambda_q2[0], "lambda_k2": lambda_k2[0], "subln": subln[0],
    }
    return (_trunk(x_prompt, wts), _trunk(x_sample, wts))
```

```python
import functools
import math

import jax
import jax.numpy as jnp
from jax import lax
from jax.experimental import pallas as pl
from jax.experimental.pallas import tpu as pltpu

F32 = jnp.float32
BF16 = jnp.bfloat16

D_MODEL = 2048
D_FNET = 1024
FNET_GROUP_DIM = 256
FNET_GROUPS = D_FNET // FNET_GROUP_DIM
D_HGRN = 1024
HGRN_HEAD_DIM = 128
HGRN_HEADS = D_HGRN // HGRN_HEAD_DIM
DIFF_HEADS = 8
DIFF_HEAD_DIM = 128
DIFF_V_DIM = 256
D_ATTN = DIFF_HEADS * DIFF_V_DIM
RMS_EPS = 1e-6
LOG2E = 1.4426950408889634

SUBLANES = 8
LANES = 128
NORM_ROWS = 1024
FFT_L2 = 128
FFT_KB = 8
HGRN_BLOCK = 128
HGRN_GROUP = 16
OUT_ROW_SPLIT = 2
IN_TILE = 1024
OUT_TILE_ROWS = 512
MIB = 1024 * 1024
VMEM_LIMIT_MIB = 52


def _params(semantics):
    return pltpu.CompilerParams(dimension_semantics=semantics,
                                vmem_limit_bytes=VMEM_LIMIT_MIB * MIB)


def _silu(x):
    return x * jax.nn.sigmoid(x)


def _dot_nt(a, b):
    return lax.dot_general(a, b, (((1,), (1,)), ((), ())), preferred_element_type=F32)


def _mm_kernel(h_ref, w_ref, o_ref, *, out_scale):
    r = jnp.dot(h_ref[...], w_ref[...].astype(BF16), preferred_element_type=F32)
    if out_scale != 1.0:
        r = r * out_scale
    o_ref[...] = r.astype(o_ref.dtype)


def _mm_prenorm_kernel(x_ref, g_ref, w_ref, o_ref):
    x = x_ref[...]
    inv = lax.rsqrt(jnp.mean(x * x, axis=-1, keepdims=True) + RMS_EPS)
    r = jnp.dot((x * g_ref[...]).astype(BF16), w_ref[...].astype(BF16), preferred_element_type=F32)
    o_ref[...] = (r * inv).astype(o_ref.dtype)


def _mm_norm_kernel(h_ref, w_ref, o_ref, n_ref, *, out_scale):
    r = jnp.dot(h_ref[...], w_ref[...].astype(BF16), preferred_element_type=F32) * out_scale
    o = r.astype(o_ref.dtype)
    o_ref[...] = o
    of = o.astype(F32)
    sq = of * of
    for g in range(sq.shape[1] // LANES):
        ss = jnp.sum(sq[:, g * LANES:(g + 1) * LANES], axis=1, keepdims=True)
        n_ref[0, 0, g:g + 1, :] = jnp.broadcast_to(jnp.max(ss, axis=0, keepdims=True), (1, LANES))


def _mm_t_kernel(h_ref, w_ref, o_ref, r_ref):
    r_ref[...] = jnp.dot(h_ref[...], w_ref[...].astype(BF16), preferred_element_type=F32)
    o_ref[...] = r_ref[...].T.astype(o_ref.dtype)


def _matmul_cols(h, w, col_off, n_cols, transpose_out=False, out_scale=1.0, row_norms=False,
                 prenorm_gain=None, tm=IN_TILE, tn=IN_TILE):
    t, k = h.shape
    tm = min(tm, t)
    off = col_off // tn
    if prenorm_gain is not None:
        assert not transpose_out and not row_norms and out_scale == 1.0
        return pl.pallas_call(
            _mm_prenorm_kernel,
            grid=(n_cols // tn, t // tm),
            in_specs=[pl.BlockSpec((tm, k), lambda n, m: (m, 0)),
                      pl.BlockSpec((1, k), lambda n, m: (0, 0)),
                      pl.BlockSpec((k, tn), lambda n, m: (0, n + off))],
            out_specs=pl.BlockSpec((tm, tn), lambda n, m: (m, n)),
            out_shape=jax.ShapeDtypeStruct((t, n_cols), BF16),
            compiler_params=_params(("parallel", "parallel")),
            name="in_proj_prenorm",
        )(h, prenorm_gain.reshape(1, k).astype(F32), w)
    if row_norms:
        assert not transpose_out and tn // LANES == SUBLANES
        out, nrm = pl.pallas_call(
            functools.partial(_mm_norm_kernel, out_scale=out_scale),
            grid=(n_cols // tn, t // tm),
            in_specs=[pl.BlockSpec((tm, k), lambda n, m: (m, 0)),
                      pl.BlockSpec((k, tn), lambda n, m: (0, n + off))],
            out_specs=[pl.BlockSpec((tm, tn), lambda n, m: (m, n)),
                       pl.BlockSpec((1, 1, SUBLANES, LANES), lambda n, m: (n, m, 0, 0))],
            out_shape=[jax.ShapeDtypeStruct((t, n_cols), BF16),
                       jax.ShapeDtypeStruct((n_cols // tn, t // tm, SUBLANES, LANES), F32)],
            compiler_params=_params(("parallel", "parallel")),
            name="in_proj_norms",
        )(h, w)
        nrm = jnp.sqrt(nrm[:, :, :, 0]).transpose(1, 0, 2).reshape(t // tm, n_cols // LANES)
        return out, nrm
    if transpose_out:
        assert out_scale == 1.0
        body, out_shape, scratch = _mm_t_kernel, (n_cols, t), [pltpu.VMEM((tm, tn), F32)]
        out_spec = pl.BlockSpec((tn, tm), lambda n, m: (n, m))
    else:
        body = functools.partial(_mm_kernel, out_scale=out_scale)
        out_shape, scratch = (t, n_cols), []
        out_spec = pl.BlockSpec((tm, tn), lambda n, m: (m, n))
    return pl.pallas_call(
        body,
        grid=(n_cols // tn, t // tm),
        in_specs=[pl.BlockSpec((tm, k), lambda n, m: (m, 0)),
                  pl.BlockSpec((k, tn), lambda n, m: (0, n + off))],
        out_specs=out_spec,
        out_shape=jax.ShapeDtypeStruct(out_shape, BF16),
        scratch_shapes=scratch,
        compiler_params=_params(("parallel", "parallel")),
        name="in_proj_t" if transpose_out else "in_proj",
    )(h, w)


def _fft_tables(l):
    l1 = l // FFT_L2
    two_pi = 2.0 * math.pi
    k1 = jnp.arange(l1, dtype=jnp.int32)
    a1 = ((k1[:, None] * k1[None, :]) % l1).astype(F32) * (two_pi / l1)
    f1 = jnp.concatenate([jnp.cos(a1), -jnp.sin(a1)], axis=0).astype(BF16)
    k2 = jnp.arange(FFT_L2, dtype=jnp.int32)
    kk = k1[:, None, None] + l1 * k2[None, :, None]
    a2 = ((kk * k2[None, None, :]) % l).astype(F32) * (two_pi / l)
    c2, s2 = jnp.cos(a2), jnp.sin(a2)
    g2 = jnp.concatenate([jnp.concatenate([c2, s2], axis=2),
                          jnp.concatenate([-s2, c2], axis=2)], axis=1).astype(BF16)
    c = jnp.arange(FNET_GROUP_DIM, dtype=jnp.int32)
    a3 = ((c[:, None] * c[None, :]) % FNET_GROUP_DIM).astype(F32) * (two_pi / FNET_GROUP_DIM)
    scale = 1.0 / math.sqrt(l * FNET_GROUP_DIM)
    cs = (jnp.concatenate([jnp.cos(a3), jnp.sin(a3)], axis=0) * scale).astype(BF16)
    return f1, g2, cs


def _fft1_kernel(f_ref, u_ref, t_ref):
    l1 = u_ref.shape[1]
    r = jnp.dot(f_ref[...], u_ref[0], preferred_element_type=F32)
    t_ref[0, 0] = r[:l1].astype(t_ref.dtype)
    t_ref[1, 0] = r[l1:].astype(t_ref.dtype)


def _fft2_kernel(t_ref, g_ref, cs_ref, o_ref, p_scr):
    kb = g_ref.shape[0]
    l2 = FFT_L2
    gd = FNET_GROUP_DIM
    for j in range(kb):
        gm = g_ref[j]
        for g in range(FNET_GROUPS):
            cols = slice(g * gd, (g + 1) * gd)
            rhs = jnp.concatenate([t_ref[0, 0, j, :, cols], t_ref[1, 0, j, :, cols]], axis=0)
            p = jnp.dot(gm, rhs, preferred_element_type=F32)
            r0 = (j * FNET_GROUPS + g) * l2
            p_scr[r0:r0 + l2, 0:gd] = p[:l2].astype(p_scr.dtype)
            p_scr[r0:r0 + l2, gd:2 * gd] = p[l2:].astype(p_scr.dtype)
    y = jnp.dot(p_scr[...], cs_ref[...], preferred_element_type=F32)
    for j in range(kb):
        for g in range(FNET_GROUPS):
            r0 = (j * FNET_GROUPS + g) * l2
            cols = slice(j * D_FNET + g * gd, j * D_FNET + (g + 1) * gd)
            o_ref[0, :, cols] = y[r0:r0 + l2].astype(o_ref.dtype)


def _fnet_mix(u, b, l):
    l1, l2 = l // FFT_L2, FFT_L2
    f1, g2, cs = _fft_tables(l)
    wcols = l2 * D_FNET
    w = min(wcols, (2 * MIB) // (2 * l1))
    t = pl.pallas_call(
        _fft1_kernel,
        grid=(b, wcols // w),
        in_specs=[pl.BlockSpec((2 * l1, l1), lambda i, j: (0, 0)),
                  pl.BlockSpec((1, l1, w), lambda i, j: (i, 0, j))],
        out_specs=pl.BlockSpec((2, 1, l1, w), lambda i, j: (0, i, 0, j)),
        out_shape=jax.ShapeDtypeStruct((2, b, l1, wcols), BF16),
        compiler_params=_params(("parallel", "parallel")),
        name="fnet_stage1",
    )(f1, u.reshape(b, l1, wcols))
    kb = min(FFT_KB, l1)
    y = pl.pallas_call(
        _fft2_kernel,
        grid=(b, l1 // kb),
        in_specs=[pl.BlockSpec((2, 1, kb, l2, D_FNET), lambda i, j: (0, i, j, 0, 0)),
                  pl.BlockSpec((kb, 2 * l2, 2 * l2), lambda i, j: (j, 0, 0)),
                  pl.BlockSpec((2 * FNET_GROUP_DIM, FNET_GROUP_DIM), lambda i, j: (0, 0))],
        out_specs=pl.BlockSpec((1, l2, kb * D_FNET), lambda i, j: (i, 0, j)),
        out_shape=jax.ShapeDtypeStruct((b, l2, l1 * D_FNET), BF16),
        scratch_shapes=[pltpu.VMEM((kb * FNET_GROUPS * l2, 2 * FNET_GROUP_DIM), BF16)],
        compiler_params=_params(("parallel", "parallel")),
        name="fnet_stage2",
    )(t.reshape(2, b, l1, l2, D_FNET), g2, cs)
    return y


def _hgrn_kernel(q_ref, i_ref, ff_ref, fb_ref, gate_ref, lbl_ref, gn_ref, o_ref, acc_ref, qs_ref):
    hb = HGRN_BLOCK
    half = hb // 2
    n_blocks = q_ref.shape[0] // hb
    group = min(HGRN_GROUP, n_blocks)
    dk = HGRN_HEAD_DIM
    row = lax.broadcasted_iota(jnp.int32, (hb, hb), 0)
    col = lax.broadcasted_iota(jnp.int32, (hb, hb), 1)
    row2 = lax.broadcasted_iota(jnp.int32, (hb, 2 * hb), 0)
    col2 = lax.broadcasted_iota(jnp.int32, (hb, 2 * hb), 1) & (hb - 1)

    logits = lbl_ref[...]
    mx = jnp.max(logits, axis=1, keepdims=True)
    ex = jnp.exp(logits - mx)
    lb_all = ex[:, 0, :] / jnp.sum(ex, axis=1)
    gn = gn_ref[...]

    def direction(f_ref, lb, forward):
        keep = (col <= row) if forward else (col >= row)
        tri2 = jnp.where((col2 <= row2) if forward else (col2 >= row2), 1.0, 0.0).astype(BF16)

        def body(step, state_t):
            gi = step if forward else n_blocks // group - 1 - step
            base = gi * (group * hb)
            gates, pieces = [], []
            for u in range(group):
                rows = pl.ds(pl.multiple_of(base + u * hb, hb), hb)
                f = lb + (1.0 - lb) * jax.nn.sigmoid(f_ref[rows, :].astype(F32))
                g = jnp.log2(f)
                hi = g.astype(BF16)
                pieces.append((hi, (g - hi.astype(F32)).astype(BF16)))
                gates.append(1.0 - f)
            stacked = jnp.concatenate([jnp.concatenate([p[0] for p in pieces], axis=1),
                                       jnp.concatenate([p[1] for p in pieces], axis=1)], axis=0)
            a_all = jnp.dot(tri2, stacked, preferred_element_type=F32)
            units = []
            for u in range(group):
                rows = pl.ds(pl.multiple_of(base + u * hb, hb), hb)
                if forward:
                    q = _silu(q_ref[rows, :].astype(F32))
                    qs_ref[rows, :] = q
                else:
                    q = qs_ref[rows, :]
                v = i_ref[rows, :]
                k = gates[u]
                a = a_all[:, u * dk:(u + 1) * dk]
                ref = a[half - 1:half, :] if forward else a[half:half + 1, :]
                end = a[hb - 1:hb, :] if forward else a[0:1, :]
                qt = q * jnp.exp2(a - ref)
                kt = k * jnp.exp2(ref - a)
                scores = jnp.where(keep, _dot_nt(qt.astype(BF16), kt.astype(BF16)), 0.0)
                o_intra = jnp.dot(scores.astype(BF16), v, preferred_element_type=F32)
                q_in = (qt * jnp.exp2(ref)).astype(BF16)
                k_end = (kt * jnp.exp2(end - ref)).astype(BF16)
                kv_t = lax.dot_general(v, k_end, (((0,), (0,)), ((), ())),
                                       preferred_element_type=F32)
                units.append((rows, o_intra, q_in, kv_t, jnp.exp2(end)))
            for rows, o_intra, q_in, kv_t, decay in (units if forward else units[::-1]):
                o = o_intra + _dot_nt(q_in, state_t.astype(BF16))
                state_t = state_t * decay + kv_t
                if forward:
                    acc_ref[rows, :] = o
                else:
                    tot = acc_ref[rows, :] + o
                    inv = lax.rsqrt(jnp.mean(tot * tot, axis=-1, keepdims=True) + RMS_EPS)
                    gate = gate_ref[rows, :].astype(F32)
                    o_ref[rows, :] = (tot * inv * gn * _silu(gate)).astype(o_ref.dtype)
            return state_t

        lax.fori_loop(0, n_blocks // group, body, jnp.zeros((dk, dk), F32))

    direction(ff_ref, lb_all[0:1, :], True)
    direction(fb_ref, lb_all[1:2, :], False)


def _hgrn_mix_gated(hg, lb_logits, g_norm, b, l):
    nh, dk = HGRN_HEADS, HGRN_HEAD_DIM

    def col(block):
        return pl.BlockSpec((l, dk), lambda i, h: (i, block * nh + h))

    return pl.pallas_call(
        _hgrn_kernel,
        grid=(b, nh),
        in_specs=[col(0), col(1), col(2), col(3), col(4),
                  pl.BlockSpec((2, lb_logits.shape[1], dk), lambda i, h: (0, 0, h)),
                  pl.BlockSpec((1, dk), lambda i, h: (0, 0))],
        out_specs=pl.BlockSpec((l, dk), lambda i, h: (i, h)),
        out_shape=jax.ShapeDtypeStruct((b * l, D_HGRN), BF16),
        scratch_shapes=[pltpu.VMEM((l, dk), F32), pltpu.VMEM((l, dk), F32)],
        compiler_params=_params(("parallel", "parallel")),
        name="hgrn2",
    )(hg, hg, hg, hg, hg, lb_logits.astype(F32), g_norm.reshape(1, dk).astype(F32))


ATTN_TQ = 256
ATTN_TK = 512
ATTN_Q_TILES = 8
ATTN_QK_SCALE = math.sqrt(DIFF_HEAD_DIM ** -0.5 * LOG2E)
ATTN_UNROLL_PAIRS = (8, 2, 1)
ATTN_SKIP_MARGIN = 128.0
ATTN_BOUND_SLACK = 1.0 + 2.0 ** -6
ATTN_M_INIT = -1e30
ATTN_FILLER_CONST = -3e38


def _attn_kernel(slopes_ref, nrm_ref, q_ref, k_ref, vt_ref, gate_ref, lq1_ref, lk1_ref, lq2_ref,
                 lk2_ref, subln_ref, o_ref, qbd_ref, d0_ref, ta_ref, tb_ref, m_ref, l_ref, acc_ref,
                 steps_ref, *, lambda_init):
    tq, tk, nqt = ATTN_TQ, ATTN_TK, ATTN_Q_TILES
    n_blocks = k_ref.shape[0] // tk
    n_steps = n_blocks * nqt
    dh = DIFF_HEAD_DIM
    h = pl.program_id(1)
    sup = pl.program_id(2)
    slope2 = slopes_ref[h] * LOG2E
    kk = lax.broadcasted_iota(jnp.int32, (tk, tq), 0)
    qq = lax.broadcasted_iota(jnp.int32, (tk, tq), 1)
    d0 = (kk - qq).astype(F32) * slope2
    d0_ref[0] = d0
    d0_ref[1] = -d0
    for r in range(tk // tq):
        d0_ref[2 + r] = -jnp.abs(d0 - (r * tq) * slope2)

    zeros = jnp.zeros((tq, dh), BF16)
    for t in range(nqt):
        rows = slice(t * tq, (t + 1) * tq)
        qbd_ref[t, 0:tq, 0:dh] = q_ref[rows, 0:dh]
        qbd_ref[t, 0:tq, dh:2 * dh] = zeros
        qbd_ref[t, tq:2 * tq, 0:dh] = zeros
        qbd_ref[t, tq:2 * tq, dh:2 * dh] = q_ref[rows, dh:2 * dh]

    m_ref[...] = jnp.full(m_ref.shape, ATTN_M_INIT, F32)
    l_ref[...] = jnp.zeros(l_ref.shape, F32)
    acc_ref[...] = jnp.zeros(acc_ref.shape, F32)

    seq_tiles = k_ref.shape[0] // NORM_ROWS
    groups = 2 * D_ATTN // LANES
    tile0 = pl.program_id(0) * seq_tiles
    qk_bound = None
    for c in range(2):
        qn = None
        for r in range(nqt * tq // NORM_ROWS):
            v = nrm_ref[(tile0 + sup * (nqt * tq // NORM_ROWS) + r) * groups + 2 * h + c]
            qn = v if qn is None else jnp.maximum(qn, v)
        kn = None
        for r in range(seq_tiles):
            v = nrm_ref[(tile0 + r) * groups + groups // 2 + 2 * h + c]
            kn = v if kn is None else jnp.maximum(kn, v)
        qk_bound = qn * kn if qk_bound is None else jnp.maximum(qk_bound, qn * kn)
    skip_thr = ATTN_SKIP_MARGIN + 2.0 * ATTN_BOUND_SLACK * qk_bound

    def offsets(s):
        t = s % nqt
        rel = (sup * nqt + t) * tq - (s // nqt) * tk
        return t, rel, rel >= tk, rel <= -tq

    def list_step(s, n):
        t, rel, before, after = offsets(s)
        gap = jnp.where(before, rel - tk + 1, jnp.where(after, -rel - tq + 1, 0))
        steps_ref[n] = s
        return n + (gap.astype(F32) * slope2 < skip_thr).astype(jnp.int32)

    n_listed = lax.fori_loop(0, n_steps, list_step, 0, unroll=8)
    n_pairs = (n_listed + 1) // 2
    for extra in range(2):
        steps_ref[n_listed + extra] = n_steps

    def split(i):
        code = steps_ref[i]
        filler = code >= n_steps
        s = jnp.where(filler, 0, code)
        k0 = pl.multiple_of((s // nqt) * tk, tk)
        t, rel, before, after = offsets(s)
        idx = jnp.where(before, 0, jnp.where(after, 1, 2 + rel // tq))
        lin = rel.astype(F32) * slope2
        const = jnp.where(before, -lin, jnp.where(after, lin, 0.0))
        return k0, t, idx, jnp.where(filler, ATTN_FILLER_CONST, const)

    def scores(s, t_ref):
        k0, t, idx, _ = split(s)
        bias = d0_ref[idx]
        sc = _dot_nt(k_ref[pl.ds(k0, tk), :], qbd_ref[t])
        t_ref[:, 0:tq] = sc[:, 0:tq] + bias
        t_ref[:, tq:2 * tq] = sc[:, tq:2 * tq] + bias

    def accumulate(s, t_ref):
        k0, t, _, const = split(s)
        sc = t_ref[...]
        m_old = m_ref[t]
        m_new = jnp.maximum(m_old, jnp.max(sc, axis=0, keepdims=True) + const)
        alpha = jnp.exp2(m_old - m_new)
        p = jnp.exp2(sc - (m_new - const))
        l_ref[t] = alpha * l_ref[t] + jnp.sum(p, axis=0, keepdims=True)
        acc_ref[t] = alpha * acc_ref[t] + jnp.dot(vt_ref[:, pl.ds(k0, tk)], p.astype(BF16),
                                                  preferred_element_type=F32)
        m_ref[t] = m_new

    scores(0, ta_ref)

    def pair(i):
        scores(2 * i + 1, tb_ref)
        accumulate(2 * i, ta_ref)
        scores(2 * i + 2, ta_ref)
        accumulate(2 * i + 1, tb_ref)

    done = 0
    for width in ATTN_UNROLL_PAIRS:
        def body(c, carry, width=width, done=done):
            for u in range(width):
                pair(done + c * width + u)
            return carry

        trips = (n_pairs - done) // width
        lax.fori_loop(0, trips, body, 0)
        done = done + trips * width

    lam = (jnp.exp(jnp.sum(lq1_ref[...] * lk1_ref[...], keepdims=True))
           - jnp.exp(jnp.sum(lq2_ref[...] * lk2_ref[...], keepdims=True)) + lambda_init)
    gain = subln_ref[...] * (1.0 - lambda_init)
    for t in range(nqt):
        rows = slice(t * tq, (t + 1) * tq)
        r = 1.0 / l_ref[t]
        o_t = (acc_ref[t, :, 0:tq] * r[:, 0:tq]
               - acc_ref[t, :, tq:2 * tq] * (lam * r[:, tq:2 * tq]))
        inv = lax.rsqrt(jnp.mean(o_t * o_t, axis=0, keepdims=True) + RMS_EPS)
        o_ref[rows, :] = ((o_t * inv).T * (gain * _silu(gate_ref[rows, :].astype(F32)))
                          ).astype(o_ref.dtype)


def _diff_attention_gated(qk, qk_norms, v_t, gate, lq1, lk1, lq2, lk2, subln, lambda_init, b, l):
    nh, dv = DIFF_HEADS, DIFF_V_DIM
    tq, tk, nqt = ATTN_TQ, ATTN_TK, ATTN_Q_TILES
    qs = tq * nqt
    ns = l // qs
    assert l % qs == 0 and qs % NORM_ROWS == 0 and l % NORM_ROWS == 0
    slopes = jnp.exp2(-8.0 * (jnp.arange(nh, dtype=F32) + 1.0) / nh)
    vec = lambda a: a.reshape(1, -1).astype(F32)
    small = lambda n: pl.BlockSpec((1, n), lambda i, h, s, *_: (0, 0))
    grid_spec = pltpu.PrefetchScalarGridSpec(
        num_scalar_prefetch=2,
        grid=(b, nh, ns),
        in_specs=[pl.BlockSpec((qs, dv), lambda i, h, s, *_: (i * ns + s, h)),
                  pl.BlockSpec((l, dv), lambda i, h, s, *_: (i, nh + h)),
                  pl.BlockSpec((dv, l), lambda i, h, s, *_: (h, i)),
                  pl.BlockSpec((qs, dv), lambda i, h, s, *_: (i * ns + s, h)),
                  small(DIFF_HEAD_DIM), small(DIFF_HEAD_DIM), small(DIFF_HEAD_DIM),
                  small(DIFF_HEAD_DIM), small(dv)],
        out_specs=pl.BlockSpec((qs, dv), lambda i, h, s, *_: (i * ns + s, h)),
        scratch_shapes=[pltpu.VMEM((nqt, 2 * tq, 2 * DIFF_HEAD_DIM), BF16),
                        pltpu.VMEM((2 + tk // tq, tk, tq), F32),
                        pltpu.VMEM((tk, 2 * tq), F32),
                        pltpu.VMEM((tk, 2 * tq), F32),
                        pltpu.VMEM((nqt, 1, 2 * tq), F32), pltpu.VMEM((nqt, 1, 2 * tq), F32),
                        pltpu.VMEM((nqt, dv, 2 * tq), F32),
                        pltpu.SMEM((l // tk * nqt + 8,), jnp.int32)],
    )
    return pl.pallas_call(
        functools.partial(_attn_kernel, lambda_init=lambda_init),
        grid_spec=grid_spec,
        out_shape=jax.ShapeDtypeStruct((b * l, D_ATTN), BF16),
        compiler_params=_params(("parallel", "parallel", "parallel")),
        name="diff_attention",
    )(slopes, qk_norms.reshape(-1), qk, qk, v_t, gate, vec(lq1), vec(lk1), vec(lq2), vec(lk2), vec(subln))


def _out_kernel(*refs, n_act, gated, with_next, fft_l1):
    refs = list(refs)
    act_refs = [refs.pop(0) for _ in range(n_act)]
    gate_ref = refs.pop(0) if gated else None
    w_refs = [refs.pop(0) for _ in range(n_act)]
    x_ref, g_ref = refs.pop(0), refs.pop(0)
    gn_ref = refs.pop(0) if with_next else None
    o_ref = refs.pop(0)
    h_ref = refs.pop(0) if with_next else None
    tm = x_ref.shape[0]
    if fft_l1:
        rows_ref = refs.pop(0)
        width = act_refs[0].shape[2] // fft_l1
        for k1 in range(fft_l1):
            blk = act_refs[0][0, :, k1 * width:(k1 + 1) * width].astype(F32)
            for t in range(width // LANES):
                rows_ref[t, pl.ds(k1, tm // fft_l1, stride=fft_l1), :] = blk[:, t * LANES:(t + 1) * LANES]
    sub = tm // OUT_ROW_SPLIT
    for part in range(OUT_ROW_SPLIT):
        rows = slice(part * sub, (part + 1) * sub)
        acts = [a_ref[rows, :] for a_ref in act_refs[1 if fft_l1 else 0:]]
        if fft_l1:
            acts.insert(0, jnp.concatenate([rows_ref[t, rows, :] for t in range(rows_ref.shape[0])],
                                           axis=1))
        if gated:
            acts[0] = (acts[0].astype(F32) * _silu(gate_ref[rows, :].astype(F32))).astype(BF16)
        y = jnp.dot(acts[0], w_refs[0][...], preferred_element_type=F32)
        for a, w_ref in zip(acts[1:], w_refs[1:]):
            y = y + jnp.dot(a, w_ref[...], preferred_element_type=F32)
        inv = lax.rsqrt(jnp.mean(y * y, axis=-1, keepdims=True) + RMS_EPS)
        out = x_ref[rows, :] + y * inv * g_ref[...]
        o_ref[rows, :] = out
        if with_next:
            inv_n = lax.rsqrt(jnp.mean(out * out, axis=-1, keepdims=True) + RMS_EPS)
            h_ref[rows, :] = (out * inv_n * gn_ref[...]).astype(BF16)


def _out_proj_residual(acts, ws, x, g, gate=None, g_next=None, fft_l1=0, tm=OUT_TILE_ROWS):
    t, d = x.shape
    n_act = len(acts)
    row = lambda n: pl.BlockSpec((tm, n), lambda i: (i, 0))
    vec = lambda: pl.BlockSpec((1, d), lambda i: (0, 0))
    args = list(acts)
    in_specs = [row(a.shape[-1]) for a in acts]
    scratch = []
    if fft_l1:
        per_seq = acts[0].shape[1] * fft_l1 // tm
        in_specs[0] = pl.BlockSpec((1, tm // fft_l1, acts[0].shape[2]),
                                   lambda i: (i // per_seq, i % per_seq, 0))
        width = acts[0].shape[2] // fft_l1
        scratch = [pltpu.VMEM((width // LANES, tm, LANES), F32)]
    if gate is not None:
        args.append(gate)
        in_specs.append(row(gate.shape[1]))
    args += list(ws) + [x, g.reshape(1, d).astype(F32)]
    in_specs += [pl.BlockSpec(w.shape, lambda i: (0, 0)) for w in ws] + [row(d), vec()]
    out_specs, out_shape = row(d), jax.ShapeDtypeStruct((t, d), F32)
    if g_next is not None:
        args.append(g_next.reshape(1, d).astype(F32))
        in_specs.append(vec())
        out_specs = [out_specs, row(d)]
        out_shape = [out_shape, jax.ShapeDtypeStruct((t, d), BF16)]
    return pl.pallas_call(
        functools.partial(_out_kernel, n_act=n_act, gated=gate is not None,
                          with_next=g_next is not None, fft_l1=fft_l1),
        grid=(t // tm,),
        in_specs=in_specs,
        out_specs=out_specs,
        out_shape=out_shape,
        scratch_shapes=scratch,
        compiler_params=_params(("parallel",)),
        name="out_proj_residual",
    )(*args)


def _trunk(x3, wts):
    b, l, d = x3.shape
    x = x3.reshape(b * l, d)

    w_in, g_pre = wts["ev_w_in"], wts["ev_norm_pre"]
    u = _matmul_cols(x, w_in, 0, D_FNET, prenorm_gain=g_pre)
    gate_a = _matmul_cols(x, w_in, D_FNET, D_FNET, prenorm_gain=g_pre)
    hg = _matmul_cols(x, w_in, 2 * D_FNET, 5 * D_HGRN, prenorm_gain=g_pre)
    y_a = _fnet_mix(u, b, l)
    y_b = _hgrn_mix_gated(hg, wts["hgrn_lb_logits"], wts["hgrn_norm"], b, l)
    w_out = wts["ev_w_out"]
    l1 = l // FFT_L2
    tm_out = OUT_TILE_ROWS
    if (tm_out // l1) % (2 * SUBLANES) == 0:
        fft_l1 = l1
    else:
        fft_l1, y_a = 0, y_a.reshape(b * l, D_FNET)
    x, h = _out_proj_residual([y_a, y_b], [w_out[:D_FNET], w_out[D_FNET:]], x, wts["ev_norm_post"],
                              gate=gate_a, g_next=wts["od_norm_pre"], fft_l1=fft_l1, tm=tm_out)

    lambda_init = 0.8 - 0.6 * math.exp(-0.3 * 1)
    w_in = wts["od_w_in"]
    qk, qk_norms = _matmul_cols(h, w_in, 0, 2 * D_ATTN, out_scale=ATTN_QK_SCALE, row_norms=True,
                                tm=NORM_ROWS)
    v_t = _matmul_cols(h, w_in, 2 * D_ATTN, D_ATTN, transpose_out=True)
    gate = _matmul_cols(h, w_in, 3 * D_ATTN, D_ATTN)
    o = _diff_attention_gated(qk, qk_norms, v_t, gate, wts["lambda_q1"], wts["lambda_k1"], wts["lambda_q2"],
                              wts["lambda_k2"], wts["subln"], lambda_init, b, l)
    x = _out_proj_residual([o], [wts["od_w_out"]], x, wts["od_norm_post"])
    return x.reshape(b, l, d)


def kernel(x_prompt, x_sample, ev_w_in, ev_w_out, ev_norm_pre, ev_norm_post, hgrn_lb_logits,
           hgrn_norm, od_w_in, od_w_out, od_norm_pre, od_norm_post,
           lambda_q1, lambda_k1, lambda_q2, lambda_k2, subln):
    wts = {
        "ev_w_in": ev_w_in[0], "ev_w_out": ev_w_out[0].astype(BF16),
        "ev_norm_pre": ev_norm_pre[0], "ev_norm_post": ev_norm_post[0],
        "hgrn_lb_logits": hgrn_lb_logits, "hgrn_norm": hgrn_norm[0],
        "od_w_in": od_w_in[0], "od_w_out": od_w_out[0].astype(BF16),
        "od_norm_pre": od_norm_pre[0], "od_norm_post": od_norm_post[0],
        "lambda_q1": lambda_q1[0], "lambda_k1": lambda_k1[0],
        "lambda_q2": lambda_q2[0], "lambda_k2": lambda_k2[0], "subln": subln[0],
    }
    return (_trunk(x_prompt, wts), _trunk(x_sample, wts))
```

```python
import functools
import math

import jax
import jax.numpy as jnp
from jax import lax
from jax.experimental import pallas as pl
from jax.experimental.pallas import tpu as pltpu

F32 = jnp.float32
BF16 = jnp.bfloat16

D_MODEL = 2048
D_FNET = 1024
FNET_GROUP_DIM = 256
FNET_GROUPS = D_FNET // FNET_GROUP_DIM
D_HGRN = 1024
HGRN_HEAD_DIM = 128
HGRN_HEADS = D_HGRN // HGRN_HEAD_DIM
DIFF_HEADS = 8
DIFF_HEAD_DIM = 128
DIFF_V_DIM = 256
D_ATTN = DIFF_HEADS * DIFF_V_DIM
RMS_EPS = 1e-6
LOG2E = 1.4426950408889634

SUBLANES = 8
LANES = 128
NORM_ROWS = 1024
FFT_L2 = 128
FFT_KB = 8
HGRN_BLOCK = 128
HGRN_GROUP = 16
OUT_ROW_SPLIT = 2
IN_TILE = 1024
OUT_TILE_ROWS = 512
MIB = 1024 * 1024
VMEM_LIMIT_MIB = 52


def _params(semantics):
    return pltpu.CompilerParams(dimension_semantics=semantics,
                                vmem_limit_bytes=VMEM_LIMIT_MIB * MIB)


def _silu(x):
    return x * jax.nn.sigmoid(x)


def _dot_nt(a, b):
    return lax.dot_general(a, b, (((1,), (1,)), ((), ())), preferred_element_type=F32)


def _mm_kernel(h_ref, w_ref, o_ref, *, out_scale):
    r = jnp.dot(h_ref[...], w_ref[...].astype(BF16), preferred_element_type=F32)
    if out_scale != 1.0:
        r = r * out_scale
    o_ref[...] = r.astype(o_ref.dtype)


def _mm_prenorm_kernel(x_ref, g_ref, w_ref, o_ref):
    x = x_ref[...]
    inv = lax.rsqrt(jnp.mean(x * x, axis=-1, keepdims=True) + RMS_EPS)
    r = jnp.dot((x * g_ref[...]).astype(BF16), w_ref[...].astype(BF16), preferred_element_type=F32)
    o_ref[...] = (r * inv).astype(o_ref.dtype)


def _mm_norm_kernel(h_ref, w_ref, o_ref, n_ref, *, out_scale):
    r = jnp.dot(h_ref[...], w_ref[...].astype(BF16), preferred_element_type=F32) * out_scale
    o = r.astype(o_ref.dtype)
    o_ref[...] = o
    of = o.astype(F32)
    sq = of * of
    for g in range(sq.shape[1] // LANES):
        ss = jnp.sum(sq[:, g * LANES:(g + 1) * LANES], axis=1, keepdims=True)
        n_ref[0, 0, g:g + 1, :] = jnp.broadcast_to(jnp.max(ss, axis=0, keepdims=True), (1, LANES))


def _mm_t_kernel(h_ref, w_ref, o_ref, r_ref):
    r_ref[...] = jnp.dot(h_ref[...], w_ref[...].astype(BF16), preferred_element_type=F32)
    o_ref[...] = r_ref[...].T.astype(o_ref.dtype)


def _matmul_cols(h, w, col_off, n_cols, transpose_out=False, out_scale=1.0, row_norms=False,
                 prenorm_gain=None, tm=IN_TILE, tn=IN_TILE):
    t, k = h.shape
    tm = min(tm, t)
    off = col_off // tn
    if prenorm_gain is not None:
        assert not transpose_out and not row_norms and out_scale == 1.0
        return pl.pallas_call(
            _mm_prenorm_kernel,
            grid=(n_cols // tn, t // tm),
            in_specs=[pl.BlockSpec((tm, k), lambda n, m: (m, 0)),
                      pl.BlockSpec((1, k), lambda n, m: (0, 0)),
                      pl.BlockSpec((k, tn), lambda n, m: (0, n + off))],
            out_specs=pl.BlockSpec((tm, tn), lambda n, m: (m, n)),
            out_shape=jax.ShapeDtypeStruct((t, n_cols), BF16),
            compiler_params=_params(("parallel", "parallel")),
            name="in_proj_prenorm",
        )(h, prenorm_gain.reshape(1, k).astype(F32), w)
    if row_norms:
        assert not transpose_out and tn // LANES == SUBLANES
        out, nrm = pl.pallas_call(
            functools.partial(_mm_norm_kernel, out_scale=out_scale),
            grid=(n_cols // tn, t // tm),
            in_specs=[pl.BlockSpec((tm, k), lambda n, m: (m, 0)),
                      pl.BlockSpec((k, tn), lambda n, m: (0, n + off))],
            out_specs=[pl.BlockSpec((tm, tn), lambda n, m: (m, n)),
                       pl.BlockSpec((1, 1, SUBLANES, LANES), lambda n, m: (n, m, 0, 0))],
            out_shape=[jax.ShapeDtypeStruct((t, n_cols), BF16),
                       jax.ShapeDtypeStruct((n_cols // tn, t // tm, SUBLANES, LANES), F32)],
            compiler_params=_params(("parallel", "parallel")),
            name="in_proj_norms",
        )(h, w)
        nrm = jnp.sqrt(nrm[:, :, :, 0]).transpose(1, 0, 2).reshape(t // tm, n_cols // LANES)
        return out, nrm
    if transpose_out:
        assert out_scale == 1.0
        body, out_shape, scratch = _mm_t_kernel, (n_cols, t), [pltpu.VMEM((tm, tn), F32)]
        out_spec = pl.BlockSpec((tn, tm), lambda n, m: (n, m))
    else:
        body = functools.partial(_mm_kernel, out_scale=out_scale)
        out_shape, scratch = (t, n_cols), []
        out_spec = pl.BlockSpec((tm, tn), lambda n, m: (m, n))
    return pl.pallas_call(
        body,
        grid=(n_cols // tn, t // tm),
        in_specs=[pl.BlockSpec((tm, k), lambda n, m: (m, 0)),
                  pl.BlockSpec((k, tn), lambda n, m: (0, n + off))],
        out_specs=out_spec,
        out_shape=jax.ShapeDtypeStruct(out_shape, BF16),
        scratch_shapes=scratch,
        compiler_params=_params(("parallel", "parallel")),
        name="in_proj_t" if transpose_out else "in_proj",
    )(h, w)


def _fft_tables(l):
    l1 = l // FFT_L2
    two_pi = 2.0 * math.pi
    k1 = jnp.arange(l1, dtype=jnp.int32)
    a1 = ((k1[:, None] * k1[None, :]) % l1).astype(F32) * (two_pi / l1)
    f1 = jnp.concatenate([jnp.cos(a1), -jnp.sin(a1)], axis=0).astype(BF16)
    k2 = jnp.arange(FFT_L2, dtype=jnp.int32)
    kk = k1[:, None, None] + l1 * k2[None, :, None]
    a2 = ((kk * k2[None, None, :]) % l).astype(F32) * (two_pi / l)
    c2, s2 = jnp.cos(a2), jnp.sin(a2)
    g2 = jnp.concatenate([jnp.concatenate([c2, s2], axis=2),
                          jnp.concatenate([-s2, c2], axis=2)], axis=1).astype(BF16)
    c = jnp.arange(FNET_GROUP_DIM, dtype=jnp.int32)
    a3 = ((c[:, None] * c[None, :]) % FNET_GROUP_DIM).astype(F32) * (two_pi / FNET_GROUP_DIM)
    scale = 1.0 / math.sqrt(l * FNET_GROUP_DIM)
    cs = (jnp.concatenate([jnp.cos(a3), jnp.sin(a3)], axis=0) * scale).astype(BF16)
    return f1, g2, cs


def _fft1_kernel(f_ref, u_ref, t_ref):
    l1 = u_ref.shape[1]
    r = jnp.dot(f_ref[...], u_ref[0], preferred_element_type=F32)
    t_ref[0, 0] = r[:l1].astype(t_ref.dtype)
    t_ref[1, 0] = r[l1:].astype(t_ref.dtype)


def _fft2_kernel(t_ref, g_ref, cs_ref, o_ref, p_scr):
    kb = g_ref.shape[0]
    l2 = FFT_L2
    gd = FNET_GROUP_DIM
    for j in range(kb):
        gm = g_ref[j]
        for g in range(FNET_GROUPS):
            cols = slice(g * gd, (g + 1) * gd)
            rhs = jnp.concatenate([t_ref[0, 0, j, :, cols], t_ref[1, 0, j, :, cols]], axis=0)
            p = jnp.dot(gm, rhs, preferred_element_type=F32)
            r0 = (j * FNET_GROUPS + g) * l2
            p_scr[r0:r0 + l2, 0:gd] = p[:l2].astype(p_scr.dtype)
            p_scr[r0:r0 + l2, gd:2 * gd] = p[l2:].astype(p_scr.dtype)
    y = jnp.dot(p_scr[...], cs_ref[...], preferred_element_type=F32)
    for j in range(kb):
        for g in range(FNET_GROUPS):
            r0 = (j * FNET_GROUPS + g) * l2
            cols = slice(j * D_FNET + g * gd, j * D_FNET + (g + 1) * gd)
            o_ref[0, :, cols] = y[r0:r0 + l2].astype(o_ref.dtype)


def _fnet_mix(u, b, l):
    l1, l2 = l // FFT_L2, FFT_L2
    f1, g2, cs = _fft_tables(l)
    wcols = l2 * D_FNET
    w = min(wcols, (2 * MIB) // (2 * l1))
    t = pl.pallas_call(
        _fft1_kernel,
        grid=(b, wcols // w),
        in_specs=[pl.BlockSpec((2 * l1, l1), lambda i, j: (0, 0)),
                  pl.BlockSpec((1, l1, w), lambda i, j: (i, 0, j))],
        out_specs=pl.BlockSpec((2, 1, l1, w), lambda i, j: (0, i, 0, j)),
        out_shape=jax.ShapeDtypeStruct((2, b, l1, wcols), BF16),
        compiler_params=_params(("parallel", "parallel")),
        name="fnet_stage1",
    )(f1, u.reshape(b, l1, wcols))
    kb = min(FFT_KB, l1)
    y = pl.pallas_call(
        _fft2_kernel,
        grid=(b, l1 // kb),
        in_specs=[pl.BlockSpec((2, 1, kb, l2, D_FNET), lambda i, j: (0, i, j, 0, 0)),
                  pl.BlockSpec((kb, 2 * l2, 2 * l2), lambda i, j: (j, 0, 0)),
                  pl.BlockSpec((2 * FNET_GROUP_DIM, FNET_GROUP_DIM), lambda i, j: (0, 0))],
        out_specs=pl.BlockSpec((1, l2, kb * D_FNET), lambda i, j: (i, 0, j)),
        out_shape=jax.ShapeDtypeStruct((b, l2, l1 * D_FNET), BF16),
        scratch_shapes=[pltpu.VMEM((kb * FNET_GROUPS * l2, 2 * FNET_GROUP_DIM), BF16)],
        compiler_params=_params(("parallel", "parallel")),
        name="fnet_stage2",
    )(t.reshape(2, b, l1, l2, D_FNET), g2, cs)
    return y


def _hgrn_kernel(q_ref, i_ref, ff_ref, fb_ref, gate_ref, lbl_ref, gn_ref, o_ref, acc_ref, qs_ref):
    hb = HGRN_BLOCK
    half = hb // 2
    n_blocks = q_ref.shape[0] // hb
    group = min(HGRN_GROUP, n_blocks)
    dk = HGRN_HEAD_DIM
    row = lax.broadcasted_iota(jnp.int32, (hb, hb), 0)
    col = lax.broadcasted_iota(jnp.int32, (hb, hb), 1)
    row2 = lax.broadcasted_iota(jnp.int32, (hb, 2 * hb), 0)
    col2 = lax.broadcasted_iota(jnp.int32, (hb, 2 * hb), 1) & (hb - 1)

    logits = lbl_ref[...]
    mx = jnp.max(logits, axis=1, keepdims=True)
    ex = jnp.exp(logits - mx)
    lb_all = ex[:, 0, :] / jnp.sum(ex, axis=1)
    gn = gn_ref[...]

    def direction(f_ref, lb, forward):
        keep = (col <= row) if forward else (col >= row)
        tri2 = jnp.where((col2 <= row2) if forward else (col2 >= row2), 1.0, 0.0).astype(BF16)

        def body(step, state_t):
            gi = step if forward else n_blocks // group - 1 - step
            base = gi * (group * hb)
            gates, pieces = [], []
            for u in range(group):
                rows = pl.ds(pl.multiple_of(base + u * hb, hb), hb)
                f = lb + (1.0 - lb) * jax.nn.sigmoid(f_ref[rows, :].astype(F32))
                g = jnp.log2(f)
                hi = g.astype(BF16)
                pieces.append((hi, (g - hi.astype(F32)).astype(BF16)))
                gates.append(1.0 - f)
            stacked = jnp.concatenate([jnp.concatenate([p[0] for p in pieces], axis=1),
                                       jnp.concatenate([p[1] for p in pieces], axis=1)], axis=0)
            a_all = jnp.dot(tri2, stacked, preferred_element_type=F32)
            units = []
            for u in range(group):
                rows = pl.ds(pl.multiple_of(base + u * hb, hb), hb)
                if forward:
                    q = _silu(q_ref[rows, :].astype(F32))
                    qs_ref[rows, :] = q
                else:
                    q = qs_ref[rows, :]
                v = i_ref[rows, :]
                k = gates[u]
                a = a_all[:, u * dk:(u + 1) * dk]
                ref = a[half - 1:half, :] if forward else a[half:half + 1, :]
                end = a[hb - 1:hb, :] if forward else a[0:1, :]
                qt = q * jnp.exp2(a - ref)
                kt = k * jnp.exp2(ref - a)
                scores = jnp.where(keep, _dot_nt(qt.astype(BF16), kt.astype(BF16)), 0.0)
                o_intra = jnp.dot(scores.astype(BF16), v, preferred_element_type=F32)
                q_in = (qt * jnp.exp2(ref)).astype(BF16)
                k_end = (kt * jnp.exp2(end - ref)).astype(BF16)
                kv_t = lax.dot_general(v, k_end, (((0,), (0,)), ((), ())),
                                       preferred_element_type=F32)
                units.append((rows, o_intra, q_in, kv_t, jnp.exp2(end)))
            for rows, o_intra, q_in, kv_t, decay in (units if forward else units[::-1]):
                o = o_intra + _dot_nt(q_in, state_t.astype(BF16))
                state_t = state_t * decay + kv_t
                if forward:
                    acc_ref[rows, :] = o
                else:
                    tot = acc_ref[rows, :] + o
                    inv = lax.rsqrt(jnp.mean(tot * tot, axis=-1, keepdims=True) + RMS_EPS)
                    gate = gate_ref[rows, :].astype(F32)
                    o_ref[rows, :] = (tot * inv * gn * _silu(gate)).astype(o_ref.dtype)
            return state_t

        lax.fori_loop(0, n_blocks // group, body, jnp.zeros((dk, dk), F32))

    direction(ff_ref, lb_all[0:1, :], True)
    direction(fb_ref, lb_all[1:2, :], False)


def _hgrn_mix_gated(hg, lb_logits, g_norm, b, l):
    nh, dk = HGRN_HEADS, HGRN_HEAD_DIM

    def col(block):
        return pl.BlockSpec((l, dk), lambda i, h: (i, block * nh + h))

    return pl.pallas_call(
        _hgrn_kernel,
        grid=(b, nh),
        in_specs=[col(0), col(1), col(2), col(3), col(4),
                  pl.BlockSpec((2, lb_logits.shape[1], dk), lambda i, h: (0, 0, h)),
                  pl.BlockSpec((1, dk), lambda i, h: (0, 0))],
        out_specs=pl.BlockSpec((l, dk), lambda i, h: (i, h)),
        out_shape=jax.ShapeDtypeStruct((b * l, D_HGRN), BF16),
        scratch_shapes=[pltpu.VMEM((l, dk), F32), pltpu.VMEM((l, dk), F32)],
        compiler_params=_params(("parallel", "parallel")),
        name="hgrn2",
    )(hg, hg, hg, hg, hg, lb_logits.astype(F32), g_norm.reshape(1, dk).astype(F32))


ATTN_TQ = 256
ATTN_TK = 512
ATTN_Q_TILES = 8
ATTN_QK_SCALE = math.sqrt(DIFF_HEAD_DIM ** -0.5 * LOG2E)
ATTN_UNROLL_PAIRS = (8, 2, 1)
ATTN_SKIP_MARGIN = 128.0
ATTN_BOUND_SLACK = 1.0 + 2.0 ** -6
ATTN_M_INIT = -1e30
ATTN_FILLER_CONST = -3e38


def _attn_kernel(slopes_ref, nrm_ref, q_ref, k_ref, vt_ref, gate_ref, lq1_ref, lk1_ref, lq2_ref,
                 lk2_ref, subln_ref, o_ref, qbd_ref, d0_ref, ta_ref, tb_ref, m_ref, l_ref, acc_ref,
                 steps_ref, *, lambda_init):
    tq, tk, nqt = ATTN_TQ, ATTN_TK, ATTN_Q_TILES
    n_blocks = k_ref.shape[0] // tk
    n_steps = n_blocks * nqt
    dh = DIFF_HEAD_DIM
    h = pl.program_id(1)
    sup = pl.program_id(2)
    slope2 = slopes_ref[h] * LOG2E
    kk = lax.broadcasted_iota(jnp.int32, (tk, tq), 0)
    qq = lax.broadcasted_iota(jnp.int32, (tk, tq), 1)
    d0 = (kk - qq).astype(F32) * slope2
    d0_ref[0] = d0
    d0_ref[1] = -d0
    for r in range(tk // tq):
        d0_ref[2 + r] = -jnp.abs(d0 - (r * tq) * slope2)

    zeros = jnp.zeros((tq, dh), BF16)
    for t in range(nqt):
        rows = slice(t * tq, (t + 1) * tq)
        qbd_ref[t, 0:tq, 0:dh] = q_ref[rows, 0:dh]
        qbd_ref[t, 0:tq, dh:2 * dh] = zeros
        qbd_ref[t, tq:2 * tq, 0:dh] = zeros
        qbd_ref[t, tq:2 * tq, dh:2 * dh] = q_ref[rows, dh:2 * dh]

    m_ref[...] = jnp.full(m_ref.shape, ATTN_M_INIT, F32)
    l_ref[...] = jnp.zeros(l_ref.shape, F32)
    acc_ref[...] = jnp.zeros(acc_ref.shape, F32)

    seq_tiles = k_ref.shape[0] // NORM_ROWS
    groups = 2 * D_ATTN // LANES
    tile0 = pl.program_id(0) * seq_tiles
    qk_bound = None
    for c in range(2):
        qn = None
        for r in range(nqt * tq // NORM_ROWS):
            v = nrm_ref[(tile0 + sup * (nqt * tq // NORM_ROWS) + r) * groups + 2 * h + c]
            qn = v if qn is None else jnp.maximum(qn, v)
        kn = None
        for r in range(seq_tiles):
            v = nrm_ref[(tile0 + r) * groups + groups // 2 + 2 * h + c]
            kn = v if kn is None else jnp.maximum(kn, v)
        qk_bound = qn * kn if qk_bound is None else jnp.maximum(qk_bound, qn * kn)
    skip_thr = ATTN_SKIP_MARGIN + 2.0 * ATTN_BOUND_SLACK * qk_bound

    def offsets(s):
        t = s & (nqt - 1)
        k0 = lax.shift_right_logical(s, nqt.bit_length() - 1) * tk
        rel = (sup * nqt + t) * tq - k0
        return t, k0, rel, rel >= tk, rel <= -tq

    def list_step(s, n):
        t, _, rel, before, after = offsets(s)
        gap = jnp.where(before, rel - tk + 1, jnp.where(after, -rel - tq + 1, 0))
        steps_ref[n] = s
        return n + (gap.astype(F32) * slope2 < skip_thr).astype(jnp.int32)

    n_listed = lax.fori_loop(0, n_steps, list_step, 0, unroll=8)
    n_pairs = (n_listed + 1) // 2
    for extra in range(2):
        steps_ref[n_listed + extra] = n_steps

    def split(i):
        code = steps_ref[i]
        filler = code >= n_steps
        s = jnp.where(filler, 0, code)
        t, k0, rel, before, after = offsets(s)
        diag = lax.shift_right_logical(rel, tq.bit_length() - 1)
        idx = jnp.where(before, 0, jnp.where(after, 1, 2 + diag))
        lin = rel.astype(F32) * slope2
        const = jnp.where(before, -lin, jnp.where(after, lin, 0.0))
        return pl.multiple_of(k0, tk), t, idx, jnp.where(filler, ATTN_FILLER_CONST, const)

    def scores(s, t_ref):
        k0, t, idx, _ = split(s)
        bias = d0_ref[idx]
        sc = _dot_nt(k_ref[pl.ds(k0, tk), :], qbd_ref[t])
        t_ref[:, 0:tq] = sc[:, 0:tq] + bias
        t_ref[:, tq:2 * tq] = sc[:, tq:2 * tq] + bias

    def accumulate(s, t_ref):
        k0, t, _, const = split(s)
        sc = t_ref[...]
        m_old = m_ref[t]
        m_new = jnp.maximum(m_old, jnp.max(sc, axis=0, keepdims=True) + const)
        alpha = jnp.exp2(m_old - m_new)
        p = jnp.exp2(sc - (m_new - const))
        l_ref[t] = alpha * l_ref[t] + jnp.sum(p, axis=0, keepdims=True)
        acc_ref[t] = alpha * acc_ref[t] + jnp.dot(vt_ref[:, pl.ds(k0, tk)], p.astype(BF16),
                                                  preferred_element_type=F32)
        m_ref[t] = m_new

    scores(0, ta_ref)

    def pair(i):
        scores(2 * i + 1, tb_ref)
        accumulate(2 * i, ta_ref)
        scores(2 * i + 2, ta_ref)
        accumulate(2 * i + 1, tb_ref)

    done = 0
    for width in ATTN_UNROLL_PAIRS:
        def body(c, carry, width=width, done=done):
            for u in range(width):
                pair(done + c * width + u)
            return carry

        trips = (n_pairs - done) // width
        lax.fori_loop(0, trips, body, 0)
        done = done + trips * width

    lam = (jnp.exp(jnp.sum(lq1_ref[...] * lk1_ref[...], keepdims=True))
           - jnp.exp(jnp.sum(lq2_ref[...] * lk2_ref[...], keepdims=True)) + lambda_init)
    gain = subln_ref[...] * (1.0 - lambda_init)
    for t in range(nqt):
        rows = slice(t * tq, (t + 1) * tq)
        r = 1.0 / l_ref[t]
        o_t = (acc_ref[t, :, 0:tq] * r[:, 0:tq]
               - acc_ref[t, :, tq:2 * tq] * (lam * r[:, tq:2 * tq]))
        inv = lax.rsqrt(jnp.mean(o_t * o_t, axis=0, keepdims=True) + RMS_EPS)
        o_ref[rows, :] = ((o_t * inv).T * (gain * _silu(gate_ref[rows, :].astype(F32)))
                          ).astype(o_ref.dtype)


def _diff_attention_gated(qk, qk_norms, v_t, gate, lq1, lk1, lq2, lk2, subln, lambda_init, b, l):
    nh, dv = DIFF_HEADS, DIFF_V_DIM
    tq, tk, nqt = ATTN_TQ, ATTN_TK, ATTN_Q_TILES
    qs = tq * nqt
    ns = l // qs
    assert l % qs == 0 and qs % NORM_ROWS == 0 and l % NORM_ROWS == 0
    assert nqt & (nqt - 1) == 0 and tq & (tq - 1) == 0
    slopes = jnp.exp2(-8.0 * (jnp.arange(nh, dtype=F32) + 1.0) / nh)
    vec = lambda a: a.reshape(1, -1).astype(F32)
    small = lambda n: pl.BlockSpec((1, n), lambda i, h, s, *_: (0, 0))
    grid_spec = pltpu.PrefetchScalarGridSpec(
        num_scalar_prefetch=2,
        grid=(b, nh, ns),
        in_specs=[pl.BlockSpec((qs, dv), lambda i, h, s, *_: (i * ns + s, h)),
                  pl.BlockSpec((l, dv), lambda i, h, s, *_: (i, nh + h)),
                  pl.BlockSpec((dv, l), lambda i, h, s, *_: (h, i)),
                  pl.BlockSpec((qs, dv), lambda i, h, s, *_: (i * ns + s, h)),
                  small(DIFF_HEAD_DIM), small(DIFF_HEAD_DIM), small(DIFF_HEAD_DIM),
                  small(DIFF_HEAD_DIM), small(dv)],
        out_specs=pl.BlockSpec((qs, dv), lambda i, h, s, *_: (i * ns + s, h)),
        scratch_shapes=[pltpu.VMEM((nqt, 2 * tq, 2 * DIFF_HEAD_DIM), BF16),
                        pltpu.VMEM((2 + tk // tq, tk, tq), F32),
                        pltpu.VMEM((tk, 2 * tq), F32),
                        pltpu.VMEM((tk, 2 * tq), F32),
                        pltpu.VMEM((nqt, 1, 2 * tq), F32), pltpu.VMEM((nqt, 1, 2 * tq), F32),
                        pltpu.VMEM((nqt, dv, 2 * tq), F32),
                        pltpu.SMEM((l // tk * nqt + 8,), jnp.int32)],
    )
    return pl.pallas_call(
        functools.partial(_attn_kernel, lambda_init=lambda_init),
        grid_spec=grid_spec,
        out_shape=jax.ShapeDtypeStruct((b * l, D_ATTN), BF16),
        compiler_params=_params(("parallel", "parallel", "parallel")),
        name="diff_attention",
    )(slopes, qk_norms.reshape(-1), qk, qk, v_t, gate, vec(lq1), vec(lk1), vec(lq2), vec(lk2), vec(subln))


def _out_kernel(*refs, n_act, gated, with_next, fft_l1):
    refs = list(refs)
    act_refs = [refs.pop(0) for _ in range(n_act)]
    gate_ref = refs.pop(0) if gated else None
    w_refs = [refs.pop(0) for _ in range(n_act)]
    x_ref, g_ref = refs.pop(0), refs.pop(0)
    gn_ref = refs.pop(0) if with_next else None
    o_ref = refs.pop(0)
    h_ref = refs.pop(0) if with_next else None
    tm = x_ref.shape[0]
    if fft_l1:
        rows_ref = refs.pop(0)
        width = act_refs[0].shape[2] // fft_l1
        for k1 in range(fft_l1):
            blk = act_refs[0][0, :, k1 * width:(k1 + 1) * width].astype(F32)
            for t in range(width // LANES):
                rows_ref[t, pl.ds(k1, tm // fft_l1, stride=fft_l1), :] = blk[:, t * LANES:(t + 1) * LANES]
    sub = tm // OUT_ROW_SPLIT
    for part in range(OUT_ROW_SPLIT):
        rows = slice(part * sub, (part + 1) * sub)
        acts = [a_ref[rows, :] for a_ref in act_refs[1 if fft_l1 else 0:]]
        if fft_l1:
            acts.insert(0, jnp.concatenate([rows_ref[t, rows, :] for t in range(rows_ref.shape[0])],
                                           axis=1))
        if gated:
            acts[0] = (acts[0].astype(F32) * _silu(gate_ref[rows, :].astype(F32))).astype(BF16)
        y = jnp.dot(acts[0], w_refs[0][...], preferred_element_type=F32)
        for a, w_ref in zip(acts[1:], w_refs[1:]):
            y = y + jnp.dot(a, w_ref[...], preferred_element_type=F32)
        inv = lax.rsqrt(jnp.mean(y * y, axis=-1, keepdims=True) + RMS_EPS)
        out = x_ref[rows, :] + y * inv * g_ref[...]
        o_ref[rows, :] = out
        if with_next:
            inv_n = lax.rsqrt(jnp.mean(out * out, axis=-1, keepdims=True) + RMS_EPS)
            h_ref[rows, :] = (out * inv_n * gn_ref[...]).astype(BF16)


def _out_proj_residual(acts, ws, x, g, gate=None, g_next=None, fft_l1=0, tm=OUT_TILE_ROWS):
    t, d = x.shape
    n_act = len(acts)
    row = lambda n: pl.BlockSpec((tm, n), lambda i: (i, 0))
    vec = lambda: pl.BlockSpec((1, d), lambda i: (0, 0))
    args = list(acts)
    in_specs = [row(a.shape[-1]) for a in acts]
    scratch = []
    if fft_l1:
        per_seq = acts[0].shape[1] * fft_l1 // tm
        in_specs[0] = pl.BlockSpec((1, tm // fft_l1, acts[0].shape[2]),
                                   lambda i: (i // per_seq, i % per_seq, 0))
        width = acts[0].shape[2] // fft_l1
        scratch = [pltpu.VMEM((width // LANES, tm, LANES), F32)]
    if gate is not None:
        args.append(gate)
        in_specs.append(row(gate.shape[1]))
    args += list(ws) + [x, g.reshape(1, d).astype(F32)]
    in_specs += [pl.BlockSpec(w.shape, lambda i: (0, 0)) for w in ws] + [row(d), vec()]
    out_specs, out_shape = row(d), jax.ShapeDtypeStruct((t, d), F32)
    if g_next is not None:
        args.append(g_next.reshape(1, d).astype(F32))
        in_specs.append(vec())
        out_specs = [out_specs, row(d)]
        out_shape = [out_shape, jax.ShapeDtypeStruct((t, d), BF16)]
    return pl.pallas_call(
        functools.partial(_out_kernel, n_act=n_act, gated=gate is not None,
                          with_next=g_next is not None, fft_l1=fft_l1),
        grid=(t // tm,),
        in_specs=in_specs,
        out_specs=out_specs,
        out_shape=out_shape,
        scratch_shapes=scratch,
        compiler_params=_params(("parallel",)),
        name="out_proj_residual",
    )(*args)


def _trunk(x3, wts):
    b, l, d = x3.shape
    x = x3.reshape(b * l, d)

    w_in, g_pre = wts["ev_w_in"], wts["ev_norm_pre"]
    u = _matmul_cols(x, w_in, 0, D_FNET, prenorm_gain=g_pre)
    gate_a = _matmul_cols(x, w_in, D_FNET, D_FNET, prenorm_gain=g_pre)
    hg = _matmul_cols(x, w_in, 2 * D_FNET, 5 * D_HGRN, prenorm_gain=g_pre)
    y_a = _fnet_mix(u, b, l)
    y_b = _hgrn_mix_gated(hg, wts["hgrn_lb_logits"], wts["hgrn_norm"], b, l)
    w_out = wts["ev_w_out"]
    l1 = l // FFT_L2
    tm_out = OUT_TILE_ROWS
    if (tm_out // l1) % (2 * SUBLANES) == 0:
        fft_l1 = l1
    else:
        fft_l1, y_a = 0, y_a.reshape(b * l, D_FNET)
    x, h = _out_proj_residual([y_a, y_b], [w_out[:D_FNET], w_out[D_FNET:]], x, wts["ev_norm_post"],
                              gate=gate_a, g_next=wts["od_norm_pre"], fft_l1=fft_l1, tm=tm_out)

    lambda_init = 0.8 - 0.6 * math.exp(-0.3 * 1)
    w_in = wts["od_w_in"]
    qk, qk_norms = _matmul_cols(h, w_in, 0, 2 * D_ATTN, out_scale=ATTN_QK_SCALE, row_norms=True,
                                tm=NORM_ROWS)
    v_t = _matmul_cols(h, w_in, 2 * D_ATTN, D_ATTN, transpose_out=True)
    gate = _matmul_cols(h, w_in, 3 * D_ATTN, D_ATTN)
    o = _diff_attention_gated(qk, qk_norms, v_t, gate, wts["lambda_q1"], wts["lambda_k1"], wts["lambda_q2"],
                              wts["lambda_k2"], wts["subln"], lambda_init, b, l)
    x = _out_proj_residual([o], [wts["od_w_out"]], x, wts["od_norm_post"])
    return x.reshape(b, l, d)


def kernel(x_prompt, x_sample, ev_w_in, ev_w_out, ev_norm_pre, ev_norm_post, hgrn_lb_logits,
           hgrn_norm, od_w_in, od_w_out, od_norm_pre, od_norm_post,
           lambda_q1, lambda_k1, lambda_q2, lambda_k2, subln):
    wts = {
        "ev_w_in": ev_w_in[0], "ev_w_out": ev_w_out[0].astype(BF16),
        "ev_norm_pre": ev_norm_pre[0], "ev_norm_post": ev_norm_post[0],
        "hgrn_lb_logits": hgrn_lb_logits, "hgrn_norm": hgrn_norm[0],
        "od_w_in": od_w_in[0], "od_w_out": od_w_out[0].astype(BF16),
        "od_norm_pre": od_norm_pre[0], "od_norm_post": od_norm_post[0],
        "lambda_q1": lambda_q1[0], "lambda_k1": lambda_k1[0],
        "lambda_q2": lambda_q2[0], "lambda_k2": lambda_k2[0], "subln": subln[0],
    }
    return (_trunk(x_prompt, wts), _trunk(x_sample, wts))
```

```python
import functools
import math

import jax
import jax.numpy as jnp
from jax import lax
from jax.experimental import pallas as pl
from jax.experimental.pallas import tpu as pltpu

F32 = jnp.float32
BF16 = jnp.bfloat16

D_MODEL = 2048
D_FNET = 1024
FNET_GROUP_DIM = 256
FNET_GROUPS = D_FNET // FNET_GROUP_DIM
D_HGRN = 1024
HGRN_HEAD_DIM = 128
HGRN_HEADS = D_HGRN // HGRN_HEAD_DIM
DIFF_HEADS = 8
DIFF_HEAD_DIM = 128
DIFF_V_DIM = 256
D_ATTN = DIFF_HEADS * DIFF_V_DIM
RMS_EPS = 1e-6
LOG2E = 1.4426950408889634

SUBLANES = 8
LANES = 128
NORM_ROWS = 1024
FFT_L2 = 128
FFT_KB = 8
HGRN_BLOCK = 128
HGRN_GROUP = 16
OUT_ROW_SPLIT = 2
IN_TILE = 1024
OUT_TILE_ROWS = 512
MIB = 1024 * 1024
VMEM_LIMIT_MIB = 52


def _params(semantics):
    return pltpu.CompilerParams(dimension_semantics=semantics,
                                vmem_limit_bytes=VMEM_LIMIT_MIB * MIB)


def _silu(x):
    return x * jax.nn.sigmoid(x)


def _dot_nt(a, b):
    return lax.dot_general(a, b, (((1,), (1,)), ((), ())), preferred_element_type=F32)


def _mm_kernel(h_ref, w_ref, o_ref, *, out_scale):
    r = jnp.dot(h_ref[...], w_ref[...].astype(BF16), preferred_element_type=F32)
    if out_scale != 1.0:
        r = r * out_scale
    o_ref[...] = r.astype(o_ref.dtype)


def _mm_prenorm_kernel(x_ref, g_ref, w_ref, o_ref):
    x = x_ref[...]
    inv = lax.rsqrt(jnp.mean(x * x, axis=-1, keepdims=True) + RMS_EPS)
    r = jnp.dot((x * g_ref[...]).astype(BF16), w_ref[...].astype(BF16), preferred_element_type=F32)
    o_ref[...] = (r * inv).astype(o_ref.dtype)


def _mm_norm_kernel(h_ref, w_ref, o_ref, n_ref, *, out_scale):
    r = jnp.dot(h_ref[...], w_ref[...].astype(BF16), preferred_element_type=F32) * out_scale
    o = r.astype(o_ref.dtype)
    o_ref[...] = o
    of = o.astype(F32)
    sq = of * of
    for g in range(sq.shape[1] // LANES):
        ss = jnp.sum(sq[:, g * LANES:(g + 1) * LANES], axis=1, keepdims=True)
        n_ref[0, 0, g:g + 1, :] = jnp.broadcast_to(jnp.max(ss, axis=0, keepdims=True), (1, LANES))


def _mm_t_kernel(h_ref, w_ref, o_ref, r_ref):
    r_ref[...] = jnp.dot(h_ref[...], w_ref[...].astype(BF16), preferred_element_type=F32)
    o_ref[...] = r_ref[...].T.astype(o_ref.dtype)


def _matmul_cols(h, w, col_off, n_cols, transpose_out=False, out_scale=1.0, row_norms=False,
                 prenorm_gain=None, tm=IN_TILE, tn=IN_TILE):
    t, k = h.shape
    tm = min(tm, t)
    off = col_off // tn
    if prenorm_gain is not None:
        assert not transpose_out and not row_norms and out_scale == 1.0
        return pl.pallas_call(
            _mm_prenorm_kernel,
            grid=(n_cols // tn, t // tm),
            in_specs=[pl.BlockSpec((tm, k), lambda n, m: (m, 0)),
                      pl.BlockSpec((1, k), lambda n, m: (0, 0)),
                      pl.BlockSpec((k, tn), lambda n, m: (0, n + off))],
            out_specs=pl.BlockSpec((tm, tn), lambda n, m: (m, n)),
            out_shape=jax.ShapeDtypeStruct((t, n_cols), BF16),
            compiler_params=_params(("parallel", "parallel")),
            name="in_proj_prenorm",
        )(h, prenorm_gain.reshape(1, k).astype(F32), w)
    if row_norms:
        assert not transpose_out and tn // LANES == SUBLANES
        out, nrm = pl.pallas_call(
            functools.partial(_mm_norm_kernel, out_scale=out_scale),
            grid=(n_cols // tn, t // tm),
            in_specs=[pl.BlockSpec((tm, k), lambda n, m: (m, 0)),
                      pl.BlockSpec((k, tn), lambda n, m: (0, n + off))],
            out_specs=[pl.BlockSpec((tm, tn), lambda n, m: (m, n)),
                       pl.BlockSpec((1, 1, SUBLANES, LANES), lambda n, m: (n, m, 0, 0))],
            out_shape=[jax.ShapeDtypeStruct((t, n_cols), BF16),
                       jax.ShapeDtypeStruct((n_cols // tn, t // tm, SUBLANES, LANES), F32)],
            compiler_params=_params(("parallel", "parallel")),
            name="in_proj_norms",
        )(h, w)
        nrm = jnp.sqrt(nrm[:, :, :, 0]).transpose(1, 0, 2).reshape(t // tm, n_cols // LANES)
        return out, nrm
    if transpose_out:
        assert out_scale == 1.0
        body, out_shape, scratch = _mm_t_kernel, (n_cols, t), [pltpu.VMEM((tm, tn), F32)]
        out_spec = pl.BlockSpec((tn, tm), lambda n, m: (n, m))
    else:
        body = functools.partial(_mm_kernel, out_scale=out_scale)
        out_shape, scratch = (t, n_cols), []
        out_spec = pl.BlockSpec((tm, tn), lambda n, m: (m, n))
    return pl.pallas_call(
        body,
        grid=(n_cols // tn, t // tm),
        in_specs=[pl.BlockSpec((tm, k), lambda n, m: (m, 0)),
                  pl.BlockSpec((k, tn), lambda n, m: (0, n + off))],
        out_specs=out_spec,
        out_shape=jax.ShapeDtypeStruct(out_shape, BF16),
        scratch_shapes=scratch,
        compiler_params=_params(("parallel", "parallel")),
        name="in_proj_t" if transpose_out else "in_proj",
    )(h, w)


def _fft_tables(l):
    l1 = l // FFT_L2
    two_pi = 2.0 * math.pi
    k1 = jnp.arange(l1, dtype=jnp.int32)
    a1 = ((k1[:, None] * k1[None, :]) % l1).astype(F32) * (two_pi / l1)
    f1 = jnp.concatenate([jnp.cos(a1), -jnp.sin(a1)], axis=0).astype(BF16)
    k2 = jnp.arange(FFT_L2, dtype=jnp.int32)
    kk = k1[:, None, None] + l1 * k2[None, :, None]
    a2 = ((kk * k2[None, None, :]) % l).astype(F32) * (two_pi / l)
    c2, s2 = jnp.cos(a2), jnp.sin(a2)
    g2 = jnp.concatenate([jnp.concatenate([c2, s2], axis=2),
                          jnp.concatenate([-s2, c2], axis=2)], axis=1).astype(BF16)
    c = jnp.arange(FNET_GROUP_DIM, dtype=jnp.int32)
    a3 = ((c[:, None] * c[None, :]) % FNET_GROUP_DIM).astype(F32) * (two_pi / FNET_GROUP_DIM)
    scale = 1.0 / math.sqrt(l * FNET_GROUP_DIM)
    cs = (jnp.concatenate([jnp.cos(a3), jnp.sin(a3)], axis=0) * scale).astype(BF16)
    return f1, g2, cs


def _fft1_kernel(f_ref, u_ref, t_ref):
    l1 = u_ref.shape[1]
    r = jnp.dot(f_ref[...], u_ref[0], preferred_element_type=F32)
    t_ref[0, 0] = r[:l1].astype(t_ref.dtype)
    t_ref[1, 0] = r[l1:].astype(t_ref.dtype)


def _fft2_kernel(t_ref, g_ref, cs_ref, o_ref, p_scr):
    kb = g_ref.shape[0]
    l2 = FFT_L2
    gd = FNET_GROUP_DIM
    for j in range(kb):
        gm = g_ref[j]
        for g in range(FNET_GROUPS):
            cols = slice(g * gd, (g + 1) * gd)
            rhs = jnp.concatenate([t_ref[0, 0, j, :, cols], t_ref[1, 0, j, :, cols]], axis=0)
            p = jnp.dot(gm, rhs, preferred_element_type=F32)
            r0 = (j * FNET_GROUPS + g) * l2
            p_scr[r0:r0 + l2, 0:gd] = p[:l2].astype(p_scr.dtype)
            p_scr[r0:r0 + l2, gd:2 * gd] = p[l2:].astype(p_scr.dtype)
    y = jnp.dot(p_scr[...], cs_ref[...], preferred_element_type=F32)
    for j in range(kb):
        for g in range(FNET_GROUPS):
            r0 = (j * FNET_GROUPS + g) * l2
            cols = slice(j * D_FNET + g * gd, j * D_FNET + (g + 1) * gd)
            o_ref[0, :, cols] = y[r0:r0 + l2].astype(o_ref.dtype)


def _fnet_mix(u, b, l):
    l1, l2 = l // FFT_L2, FFT_L2
    f1, g2, cs = _fft_tables(l)
    wcols = l2 * D_FNET
    w = min(wcols, (2 * MIB) // (2 * l1))
    t = pl.pallas_call(
        _fft1_kernel,
        grid=(b, wcols // w),
        in_specs=[pl.BlockSpec((2 * l1, l1), lambda i, j: (0, 0)),
                  pl.BlockSpec((1, l1, w), lambda i, j: (i, 0, j))],
        out_specs=pl.BlockSpec((2, 1, l1, w), lambda i, j: (0, i, 0, j)),
        out_shape=jax.ShapeDtypeStruct((2, b, l1, wcols), BF16),
        compiler_params=_params(("parallel", "parallel")),
        name="fnet_stage1",
    )(f1, u.reshape(b, l1, wcols))
    kb = min(FFT_KB, l1)
    y = pl.pallas_call(
        _fft2_kernel,
        grid=(b, l1 // kb),
        in_specs=[pl.BlockSpec((2, 1, kb, l2, D_FNET), lambda i, j: (0, i, j, 0, 0)),
                  pl.BlockSpec((kb, 2 * l2, 2 * l2), lambda i, j: (j, 0, 0)),
                  pl.BlockSpec((2 * FNET_GROUP_DIM, FNET_GROUP_DIM), lambda i, j: (0, 0))],
        out_specs=pl.BlockSpec((1, l2, kb * D_FNET), lambda i, j: (i, 0, j)),
        out_shape=jax.ShapeDtypeStruct((b, l2, l1 * D_FNET), BF16),
        scratch_shapes=[pltpu.VMEM((kb * FNET_GROUPS * l2, 2 * FNET_GROUP_DIM), BF16)],
        compiler_params=_params(("parallel", "parallel")),
        name="fnet_stage2",
    )(t.reshape(2, b, l1, l2, D_FNET), g2, cs)
    return y


def _hgrn_kernel(q_ref, i_ref, ff_ref, fb_ref, gate_ref, lbl_ref, gn_ref, o_ref, acc_ref, qs_ref):
    hb = HGRN_BLOCK
    half = hb // 2
    n_blocks = q_ref.shape[0] // hb
    group = min(HGRN_GROUP, n_blocks)
    dk = HGRN_HEAD_DIM
    row = lax.broadcasted_iota(jnp.int32, (hb, hb), 0)
    col = lax.broadcasted_iota(jnp.int32, (hb, hb), 1)
    row2 = lax.broadcasted_iota(jnp.int32, (hb, 2 * hb), 0)
    col2 = lax.broadcasted_iota(jnp.int32, (hb, 2 * hb), 1) & (hb - 1)

    logits = lbl_ref[...]
    mx = jnp.max(logits, axis=1, keepdims=True)
    ex = jnp.exp(logits - mx)
    lb_all = ex[:, 0, :] / jnp.sum(ex, axis=1)
    gn = gn_ref[...]

    def direction(f_ref, lb, forward):
        keep = (col <= row) if forward else (col >= row)
        tri2 = jnp.where((col2 <= row2) if forward else (col2 >= row2), 1.0, 0.0).astype(BF16)

        def body(step, state_t):
            gi = step if forward else n_blocks // group - 1 - step
            base = gi * (group * hb)
            gates, pieces = [], []
            for u in range(group):
                rows = pl.ds(pl.multiple_of(base + u * hb, hb), hb)
                f = lb + (1.0 - lb) * jax.nn.sigmoid(f_ref[rows, :].astype(F32))
                g = jnp.log2(f)
                hi = g.astype(BF16)
                pieces.append((hi, (g - hi.astype(F32)).astype(BF16)))
                gates.append(1.0 - f)
            stacked = jnp.concatenate([jnp.concatenate([p[0] for p in pieces], axis=1),
                                       jnp.concatenate([p[1] for p in pieces], axis=1)], axis=0)
            a_all = jnp.dot(tri2, stacked, preferred_element_type=F32)
            units = []
            for u in range(group):
                rows = pl.ds(pl.multiple_of(base + u * hb, hb), hb)
                if forward:
                    q = _silu(q_ref[rows, :].astype(F32))
                    qs_ref[rows, :] = q
                else:
                    q = qs_ref[rows, :]
                v = i_ref[rows, :]
                k = gates[u]
                a = a_all[:, u * dk:(u + 1) * dk]
                ref = a[half - 1:half, :] if forward else a[half:half + 1, :]
                end = a[hb - 1:hb, :] if forward else a[0:1, :]
                qt = q * jnp.exp2(a - ref)
                kt = k * jnp.exp2(ref - a)
                scores = jnp.where(keep, _dot_nt(qt.astype(BF16), kt.astype(BF16)), 0.0)
                o_intra = jnp.dot(scores.astype(BF16), v, preferred_element_type=F32)
                q_in = (qt * jnp.exp2(ref)).astype(BF16)
                k_end = (kt * jnp.exp2(end - ref)).astype(BF16)
                kv_t = lax.dot_general(v, k_end, (((0,), (0,)), ((), ())),
                                       preferred_element_type=F32)
                units.append((rows, o_intra, q_in, kv_t, jnp.exp2(end)))
            for rows, o_intra, q_in, kv_t, decay in (units if forward else units[::-1]):
                o = o_intra + _dot_nt(q_in, state_t.astype(BF16))
                state_t = state_t * decay + kv_t
                if forward:
                    acc_ref[rows, :] = o
                else:
                    tot = acc_ref[rows, :] + o
                    inv = lax.rsqrt(jnp.mean(tot * tot, axis=-1, keepdims=True) + RMS_EPS)
                    gate = gate_ref[rows, :].astype(F32)
                    o_ref[rows, :] = (tot * inv * gn * _silu(gate)).astype(o_ref.dtype)
            return state_t

        lax.fori_loop(0, n_blocks // group, body, jnp.zeros((dk, dk), F32))

    direction(ff_ref, lb_all[0:1, :], True)
    direction(fb_ref, lb_all[1:2, :], False)


def _hgrn_mix_gated(hg, lb_logits, g_norm, b, l):
    nh, dk = HGRN_HEADS, HGRN_HEAD_DIM

    def col(block):
        return pl.BlockSpec((l, dk), lambda i, h: (i, block * nh + h))

    return pl.pallas_call(
        _hgrn_kernel,
        grid=(b, nh),
        in_specs=[col(0), col(1), col(2), col(3), col(4),
                  pl.BlockSpec((2, lb_logits.shape[1], dk), lambda i, h: (0, 0, h)),
                  pl.BlockSpec((1, dk), lambda i, h: (0, 0))],
        out_specs=pl.BlockSpec((l, dk), lambda i, h: (i, h)),
        out_shape=jax.ShapeDtypeStruct((b * l, D_HGRN), BF16),
        scratch_shapes=[pltpu.VMEM((l, dk), F32), pltpu.VMEM((l, dk), F32)],
        compiler_params=_params(("parallel", "parallel")),
        name="hgrn2",
    )(hg, hg, hg, hg, hg, lb_logits.astype(F32), g_norm.reshape(1, dk).astype(F32))


ATTN_TQ = 256
ATTN_TK = 512
ATTN_Q_TILES = 8
ATTN_QK_SCALE = math.sqrt(DIFF_HEAD_DIM ** -0.5 * LOG2E)
ATTN_UNROLL_PAIRS = (6, 2, 1)
ATTN_SKIP_MARGIN = 128.0
ATTN_BOUND_SLACK = 1.0 + 2.0 ** -6
ATTN_FILLER_CONST = -3e38


def _attn_kernel(slopes_ref, nrm_ref, q_ref, k_ref, vt_ref, gate_ref, lq1_ref, lk1_ref, lq2_ref,
                 lk2_ref, subln_ref, o_ref, qbd_ref, d0_ref, ta_ref, tb_ref, m_ref, l_ref, acc_ref,
                 steps_ref, *, lambda_init):
    tq, tk, nqt = ATTN_TQ, ATTN_TK, ATTN_Q_TILES
    n_blocks = k_ref.shape[0] // tk
    n_steps = n_blocks * nqt
    dh = DIFF_HEAD_DIM
    h = pl.program_id(1)
    sup = pl.program_id(2)
    slope2 = slopes_ref[h] * LOG2E
    kk = lax.broadcasted_iota(jnp.int32, (tk, tq), 0)
    qq = lax.broadcasted_iota(jnp.int32, (tk, tq), 1)
    d0 = (kk - qq).astype(F32) * slope2
    d0_ref[0] = d0
    d0_ref[1] = -d0
    for r in range(tk // tq):
        d0_ref[2 + r] = -jnp.abs(d0 - (r * tq) * slope2)

    zeros = jnp.zeros((tq, dh), BF16)
    for t in range(nqt):
        rows = slice(t * tq, (t + 1) * tq)
        qbd_ref[t, 0:tq, 0:dh] = q_ref[rows, 0:dh]
        qbd_ref[t, 0:tq, dh:2 * dh] = zeros
        qbd_ref[t, tq:2 * tq, 0:dh] = zeros
        qbd_ref[t, tq:2 * tq, dh:2 * dh] = q_ref[rows, dh:2 * dh]

    seq_tiles = k_ref.shape[0] // NORM_ROWS
    groups = 2 * D_ATTN // LANES
    tile0 = pl.program_id(0) * seq_tiles
    qk_bound = None
    for c in range(2):
        qn = None
        for r in range(nqt * tq // NORM_ROWS):
            v = nrm_ref[(tile0 + sup * (nqt * tq // NORM_ROWS) + r) * groups + 2 * h + c]
            qn = v if qn is None else jnp.maximum(qn, v)
        kn = None
        for r in range(seq_tiles):
            v = nrm_ref[(tile0 + r) * groups + groups // 2 + 2 * h + c]
            kn = v if kn is None else jnp.maximum(kn, v)
        qk_bound = qn * kn if qk_bound is None else jnp.maximum(qk_bound, qn * kn)
    skip_thr = ATTN_SKIP_MARGIN + 2.0 * ATTN_BOUND_SLACK * qk_bound

    def offsets(s):
        t = s & (nqt - 1)
        k0 = lax.shift_right_logical(s, nqt.bit_length() - 1) * tk
        rel = (sup * nqt + t) * tq - k0
        return t, k0, rel, rel >= tk, rel <= -tq

    def list_step(s, n):
        t, _, rel, before, after = offsets(s)
        gap = jnp.where(before, rel - tk + 1, jnp.where(after, -rel - tq + 1, 0))
        steps_ref[n] = s
        return n + ((gap > 0) & (gap.astype(F32) * slope2 < skip_thr)).astype(jnp.int32)

    n_listed = lax.fori_loop(0, n_steps, list_step, 0, unroll=8)
    n_pairs = (n_listed + 1) // 2
    for extra in range(2):
        steps_ref[n_listed + extra] = n_steps

    def split(i):
        code = steps_ref[i]
        filler = code >= n_steps
        s = jnp.where(filler, 0, code)
        t, k0, rel, before, after = offsets(s)
        diag = lax.shift_right_logical(rel, tq.bit_length() - 1)
        idx = jnp.where(before, 0, jnp.where(after, 1, 2 + diag))
        lin = rel.astype(F32) * slope2
        const = jnp.where(before, -lin, jnp.where(after, lin, 0.0))
        return pl.multiple_of(k0, tk), t, idx, jnp.where(filler, ATTN_FILLER_CONST, const)

    def scores(s, t_ref):
        k0, t, idx, _ = split(s)
        bias = d0_ref[idx]
        sc = _dot_nt(k_ref[pl.ds(k0, tk), :], qbd_ref[t])
        t_ref[:, 0:tq] = sc[:, 0:tq] + bias
        t_ref[:, tq:2 * tq] = sc[:, tq:2 * tq] + bias

    def accumulate(s, t_ref):
        k0, t, _, const = split(s)
        sc = t_ref[...]
        m_old = m_ref[t]
        m_new = jnp.maximum(m_old, jnp.max(sc, axis=0, keepdims=True) + const)
        alpha = jnp.exp2(m_old - m_new)
        p = jnp.exp2(sc - (m_new - const))
        l_ref[t] = alpha * l_ref[t] + jnp.sum(p, axis=0, keepdims=True)
        acc_ref[t] = alpha * acc_ref[t] + jnp.dot(vt_ref[:, pl.ds(k0, tk)], p.astype(BF16),
                                                  preferred_element_type=F32)
        m_ref[t] = m_new

    def own_block(t):
        q0 = (sup * nqt + t) * tq
        k0 = lax.shift_left(lax.shift_right_logical(q0, tk.bit_length() - 1), tk.bit_length() - 1)
        return pl.multiple_of(k0, tk)

    def own_scores(t, t_ref):
        bias = d0_ref[2 + t % (tk // tq)]
        sc = _dot_nt(k_ref[pl.ds(own_block(t), tk), :], qbd_ref[t])
        t_ref[:, 0:tq] = sc[:, 0:tq] + bias
        t_ref[:, tq:2 * tq] = sc[:, tq:2 * tq] + bias

    def own_assign(t, t_ref):
        sc = t_ref[...]
        m_new = jnp.max(sc, axis=0, keepdims=True)
        p = jnp.exp2(sc - m_new)
        l_ref[t] = jnp.sum(p, axis=0, keepdims=True)
        acc_ref[t] = jnp.dot(vt_ref[:, pl.ds(own_block(t), tk)], p.astype(BF16),
                             preferred_element_type=F32)
        m_ref[t] = m_new

    buffers = (ta_ref, tb_ref)
    own_scores(0, ta_ref)
    for t in range(nqt):
        if t + 1 < nqt:
            own_scores(t + 1, buffers[(t + 1) % 2])
        else:
            scores(0, buffers[(t + 1) % 2])
        own_assign(t, buffers[t % 2])

    def pair(i):
        scores(2 * i + 1, tb_ref)
        accumulate(2 * i, ta_ref)
        scores(2 * i + 2, ta_ref)
        accumulate(2 * i + 1, tb_ref)

    done = 0
    for width in ATTN_UNROLL_PAIRS:
        def body(c, carry, width=width, done=done):
            for u in range(width):
                pair(done + c * width + u)
            return carry

        trips = (n_pairs - done) // width
        lax.fori_loop(0, trips, body, 0)
        done = done + trips * width

    lam = (jnp.exp(jnp.sum(lq1_ref[...] * lk1_ref[...], keepdims=True))
           - jnp.exp(jnp.sum(lq2_ref[...] * lk2_ref[...], keepdims=True)) + lambda_init)
    gain = subln_ref[...] * (1.0 - lambda_init)
    for t in range(nqt):
        rows = slice(t * tq, (t + 1) * tq)
        r = 1.0 / l_ref[t]
        o_t = (acc_ref[t, :, 0:tq] * r[:, 0:tq]
               - acc_ref[t, :, tq:2 * tq] * (lam * r[:, tq:2 * tq]))
        inv = lax.rsqrt(jnp.mean(o_t * o_t, axis=0, keepdims=True) + RMS_EPS)
        o_ref[rows, :] = ((o_t * inv).T * (gain * _silu(gate_ref[rows, :].astype(F32)))
                          ).astype(o_ref.dtype)


def _diff_attention_gated(qk, qk_norms, v_t, gate, lq1, lk1, lq2, lk2, subln, lambda_init, b, l):
    nh, dv = DIFF_HEADS, DIFF_V_DIM
    tq, tk, nqt = ATTN_TQ, ATTN_TK, ATTN_Q_TILES
    qs = tq * nqt
    ns = l // qs
    assert l % qs == 0 and qs % NORM_ROWS == 0 and l % NORM_ROWS == 0
    assert nqt & (nqt - 1) == 0 and tq & (tq - 1) == 0
    assert nqt % 2 == 0 and tk & (tk - 1) == 0 and tk % tq == 0
    slopes = jnp.exp2(-8.0 * (jnp.arange(nh, dtype=F32) + 1.0) / nh)
    vec = lambda a: a.reshape(1, -1).astype(F32)
    small = lambda n: pl.BlockSpec((1, n), lambda i, h, s, *_: (0, 0))
    grid_spec = pltpu.PrefetchScalarGridSpec(
        num_scalar_prefetch=2,
        grid=(b, nh, ns),
        in_specs=[pl.BlockSpec((qs, dv), lambda i, h, s, *_: (i * ns + s, h)),
                  pl.BlockSpec((l, dv), lambda i, h, s, *_: (i, nh + h)),
                  pl.BlockSpec((dv, l), lambda i, h, s, *_: (h, i)),
                  pl.BlockSpec((qs, dv), lambda i, h, s, *_: (i * ns + s, h)),
                  small(DIFF_HEAD_DIM), small(DIFF_HEAD_DIM), small(DIFF_HEAD_DIM),
                  small(DIFF_HEAD_DIM), small(dv)],
        out_specs=pl.BlockSpec((qs, dv), lambda i, h, s, *_: (i * ns + s, h)),
        scratch_shapes=[pltpu.VMEM((nqt, 2 * tq, 2 * DIFF_HEAD_DIM), BF16),
                        pltpu.VMEM((2 + tk // tq, tk, tq), F32),
                        pltpu.VMEM((tk, 2 * tq), F32),
                        pltpu.VMEM((tk, 2 * tq), F32),
                        pltpu.VMEM((nqt, 1, 2 * tq), F32), pltpu.VMEM((nqt, 1, 2 * tq), F32),
                        pltpu.VMEM((nqt, dv, 2 * tq), F32),
                        pltpu.SMEM((l // tk * nqt + 8,), jnp.int32)],
    )
    return pl.pallas_call(
        functools.partial(_attn_kernel, lambda_init=lambda_init),
        grid_spec=grid_spec,
        out_shape=jax.ShapeDtypeStruct((b * l, D_ATTN), BF16),
        compiler_params=_params(("parallel", "parallel", "parallel")),
        name="diff_attention",
    )(slopes, qk_norms.reshape(-1), qk, qk, v_t, gate, vec(lq1), vec(lk1), vec(lq2), vec(lk2), vec(subln))


def _out_kernel(*refs, n_act, gated, with_next, fft_l1):
    refs = list(refs)
    act_refs = [refs.pop(0) for _ in range(n_act)]
    gate_ref = refs.pop(0) if gated else None
    w_refs = [refs.pop(0) for _ in range(n_act)]
    x_ref, g_ref = refs.pop(0), refs.pop(0)
    gn_ref = refs.pop(0) if with_next else None
    o_ref = refs.pop(0)
    h_ref = refs.pop(0) if with_next else None
    tm = x_ref.shape[0]
    if fft_l1:
        rows_ref = refs.pop(0)
        width = act_refs[0].shape[2] // fft_l1
        for k1 in range(fft_l1):
            blk = act_refs[0][0, :, k1 * width:(k1 + 1) * width].astype(F32)
            for t in range(width // LANES):
                rows_ref[t, pl.ds(k1, tm // fft_l1, stride=fft_l1), :] = blk[:, t * LANES:(t + 1) * LANES]
    sub = tm // OUT_ROW_SPLIT
    for part in range(OUT_ROW_SPLIT):
        rows = slice(part * sub, (part + 1) * sub)
        acts = [a_ref[rows, :] for a_ref in act_refs[1 if fft_l1 else 0:]]
        if fft_l1:
            acts.insert(0, jnp.concatenate([rows_ref[t, rows, :] for t in range(rows_ref.shape[0])],
                                           axis=1))
        if gated:
            acts[0] = (acts[0].astype(F32) * _silu(gate_ref[rows, :].astype(F32))).astype(BF16)
        y = jnp.dot(acts[0], w_refs[0][...], preferred_element_type=F32)
        for a, w_ref in zip(acts[1:], w_refs[1:]):
            y = y + jnp.dot(a, w_ref[...], preferred_element_type=F32)
        inv = lax.rsqrt(jnp.mean(y * y, axis=-1, keepdims=True) + RMS_EPS)
        out = x_ref[rows, :] + y * inv * g_ref[...]
        o_ref[rows, :] = out
        if with_next:
            inv_n = lax.rsqrt(jnp.mean(out * out, axis=-1, keepdims=True) + RMS_EPS)
            h_ref[rows, :] = (out * inv_n * gn_ref[...]).astype(BF16)


def _out_proj_residual(acts, ws, x, g, gate=None, g_next=None, fft_l1=0, tm=OUT_TILE_ROWS):
    t, d = x.shape
    n_act = len(acts)
    row = lambda n: pl.BlockSpec((tm, n), lambda i: (i, 0))
    vec = lambda: pl.BlockSpec((1, d), lambda i: (0, 0))
    args = list(acts)
    in_specs = [row(a.shape[-1]) for a in acts]
    scratch = []
    if fft_l1:
        per_seq = acts[0].shape[1] * fft_l1 // tm
        in_specs[0] = pl.BlockSpec((1, tm // fft_l1, acts[0].shape[2]),
                                   lambda i: (i // per_seq, i % per_seq, 0))
        width = acts[0].shape[2] // fft_l1
        scratch = [pltpu.VMEM((width // LANES, tm, LANES), F32)]
    if gate is not None:
        args.append(gate)
        in_specs.append(row(gate.shape[1]))
    args += list(ws) + [x, g.reshape(1, d).astype(F32)]
    in_specs += [pl.BlockSpec(w.shape, lambda i: (0, 0)) for w in ws] + [row(d), vec()]
    out_specs, out_shape = row(d), jax.ShapeDtypeStruct((t, d), F32)
    if g_next is not None:
        args.append(g_next.reshape(1, d).astype(F32))
        in_specs.append(vec())
        out_specs = [out_specs, row(d)]
        out_shape = [out_shape, jax.ShapeDtypeStruct((t, d), BF16)]
    return pl.pallas_call(
        functools.partial(_out_kernel, n_act=n_act, gated=gate is not None,
                          with_next=g_next is not None, fft_l1=fft_l1),
        grid=(t // tm,),
        in_specs=in_specs,
        out_specs=out_specs,
        out_shape=out_shape,
        scratch_shapes=scratch,
        compiler_params=_params(("parallel",)),
        name="out_proj_residual",
    )(*args)


def _trunk(x3, wts):
    b, l, d = x3.shape
    x = x3.reshape(b * l, d)

    w_in, g_pre = wts["ev_w_in"], wts["ev_norm_pre"]
    u = _matmul_cols(x, w_in, 0, D_FNET, prenorm_gain=g_pre)
    gate_a = _matmul_cols(x, w_in, D_FNET, D_FNET, prenorm_gain=g_pre)
    hg = _matmul_cols(x, w_in, 2 * D_FNET, 5 * D_HGRN, prenorm_gain=g_pre)
    y_a = _fnet_mix(u, b, l)
    y_b = _hgrn_mix_gated(hg, wts["hgrn_lb_logits"], wts["hgrn_norm"], b, l)
    w_out = wts["ev_w_out"]
    l1 = l // FFT_L2
    tm_out = OUT_TILE_ROWS
    if (tm_out // l1) % (2 * SUBLANES) == 0:
        fft_l1 = l1
    else:
        fft_l1, y_a = 0, y_a.reshape(b * l, D_FNET)
    x, h = _out_proj_residual([y_a, y_b], [w_out[:D_FNET], w_out[D_FNET:]], x, wts["ev_norm_post"],
                              gate=gate_a, g_next=wts["od_norm_pre"], fft_l1=fft_l1, tm=tm_out)

    lambda_init = 0.8 - 0.6 * math.exp(-0.3 * 1)
    w_in = wts["od_w_in"]
    qk, qk_norms = _matmul_cols(h, w_in, 0, 2 * D_ATTN, out_scale=ATTN_QK_SCALE, row_norms=True,
                                tm=NORM_ROWS)
    v_t = _matmul_cols(h, w_in, 2 * D_ATTN, D_ATTN, transpose_out=True)
    gate = _matmul_cols(h, w_in, 3 * D_ATTN, D_ATTN)
    o = _diff_attention_gated(qk, qk_norms, v_t, gate, wts["lambda_q1"], wts["lambda_k1"], wts["lambda_q2"],
                              wts["lambda_k2"], wts["subln"], lambda_init, b, l)
    x = _out_proj_residual([o], [wts["od_w_out"]], x, wts["od_norm_post"])
    return x.reshape(b, l, d)


def kernel(x_prompt, x_sample, ev_w_in, ev_w_out, ev_norm_pre, ev_norm_post, hgrn_lb_logits,
           hgrn_norm, od_w_in, od_w_out, od_norm_pre, od_norm_post,
           lambda_q1, lambda_k1, lambda_q2, lambda_k2, subln):
    wts = {
        "ev_w_in": ev_w_in[0], "ev_w_out": ev_w_out[0].astype(BF16),
        "ev_norm_pre": ev_norm_pre[0], "ev_norm_post": ev_norm_post[0],
        "hgrn_lb_logits": hgrn_lb_logits, "hgrn_norm": hgrn_norm[0],
        "od_w_in": od_w_in[0], "od_w_out": od_w_out[0].astype(BF16),
        "od_norm_pre": od_norm_pre[0], "od_norm_post": od_norm_post[0],
        "lambda_q1": lambda_q1[0], "lambda_k1": lambda_k1[0],
        "lambda_q2": lambda_q2[0], "lambda_k2": lambda_k2[0], "subln": subln[0],
    }
    return (_trunk(x_prompt, wts), _trunk(x_sample, wts))
```

```python
import functools
import math

import jax
import jax.numpy as jnp
from jax import lax
from jax.experimental import pallas as pl
from jax.experimental.pallas import tpu as pltpu

F32 = jnp.float32
BF16 = jnp.bfloat16

D_MODEL = 2048
D_FNET = 1024
FNET_GROUP_DIM = 256
FNET_GROUPS = D_FNET // FNET_GROUP_DIM
D_HGRN = 1024
HGRN_HEAD_DIM = 128
HGRN_HEADS = D_HGRN // HGRN_HEAD_DIM
DIFF_HEADS = 8
DIFF_HEAD_DIM = 128
DIFF_V_DIM = 256
D_ATTN = DIFF_HEADS * DIFF_V_DIM
RMS_EPS = 1e-6
LOG2E = 1.4426950408889634

SUBLANES = 8
LANES = 128
NORM_ROWS = 1024
FFT_L2 = 128
FFT_KB = 8
HGRN_BLOCK = 128
HGRN_GROUP = 16
OUT_ROW_SPLIT = 2
IN_TILE = 1024
OUT_TILE_ROWS = 512
MIB = 1024 * 1024
VMEM_LIMIT_MIB = 52


def _params(semantics):
    return pltpu.CompilerParams(dimension_semantics=semantics,
                                vmem_limit_bytes=VMEM_LIMIT_MIB * MIB)


def _silu(x):
    return x * jax.nn.sigmoid(x)


def _dot_nt(a, b):
    return lax.dot_general(a, b, (((1,), (1,)), ((), ())), preferred_element_type=F32)


def _mm_kernel(h_ref, w_ref, o_ref, *, out_scale):
    r = jnp.dot(h_ref[...], w_ref[...].astype(BF16), preferred_element_type=F32)
    if out_scale != 1.0:
        r = r * out_scale
    o_ref[...] = r.astype(o_ref.dtype)


def _prenorm_product(x_ref, g_ref, w_ref):
    x = x_ref[...]
    inv = lax.rsqrt(jnp.mean(x * x, axis=-1, keepdims=True) + RMS_EPS)
    r = jnp.dot((x * g_ref[...]).astype(BF16), w_ref[...].astype(BF16), preferred_element_type=F32)
    return r * inv


def _mm_prenorm_kernel(x_ref, g_ref, w_ref, o_ref, *, silu):
    y = _prenorm_product(x_ref, g_ref, w_ref)
    o_ref[...] = (_silu(y) if silu else y).astype(o_ref.dtype)


def _mm_prenorm_gate_kernel(x_ref, g_ref, w_ref, lbl_ref, k_ref, hi_ref, lo_ref):
    y = _prenorm_product(x_ref, g_ref, w_ref)
    logits = lbl_ref[...]
    ex = jnp.exp(logits - jnp.max(logits, axis=0, keepdims=True))
    lb = ex[0:1, :] / jnp.sum(ex, axis=0, keepdims=True)
    f = lb + (1.0 - lb) * jax.nn.sigmoid(y)
    g = jnp.log2(f)
    hi = g.astype(BF16)
    k_ref[...] = (1.0 - f).astype(BF16)
    hi_ref[...] = hi
    lo_ref[...] = (g - hi.astype(F32)).astype(BF16)


def _mm_norm_kernel(h_ref, w_ref, o_ref, n_ref, *, out_scale):
    r = jnp.dot(h_ref[...], w_ref[...].astype(BF16), preferred_element_type=F32) * out_scale
    o = r.astype(o_ref.dtype)
    o_ref[...] = o
    of = o.astype(F32)
    sq = of * of
    for g in range(sq.shape[1] // LANES):
        ss = jnp.sum(sq[:, g * LANES:(g + 1) * LANES], axis=1, keepdims=True)
        n_ref[0, 0, g:g + 1, :] = jnp.broadcast_to(jnp.max(ss, axis=0, keepdims=True), (1, LANES))


def _mm_t_kernel(h_ref, w_ref, o_ref, r_ref):
    r_ref[...] = jnp.dot(h_ref[...], w_ref[...].astype(BF16), preferred_element_type=F32)
    o_ref[...] = r_ref[...].T.astype(o_ref.dtype)


def _matmul_cols(h, w, col_off, n_cols, transpose_out=False, out_scale=1.0, row_norms=False,
                 prenorm_gain=None, silu=False, gate_logits=None, tm=IN_TILE, tn=IN_TILE):
    t, k = h.shape
    tm = min(tm, t)
    off = col_off // tn
    if gate_logits is not None:
        assert prenorm_gain is not None and n_cols == tn
        out = pl.BlockSpec((tm, tn), lambda n, m: (m, n))
        return pl.pallas_call(
            _mm_prenorm_gate_kernel,
            grid=(1, t // tm),
            in_specs=[pl.BlockSpec((tm, k), lambda n, m: (m, 0)),
                      pl.BlockSpec((1, k), lambda n, m: (0, 0)),
                      pl.BlockSpec((k, tn), lambda n, m: (0, off)),
                      pl.BlockSpec(gate_logits.shape, lambda n, m: (0, 0))],
            out_specs=[out, out, out],
            out_shape=[jax.ShapeDtypeStruct((t, n_cols), BF16)] * 3,
            compiler_params=_params(("parallel", "parallel")),
            name="in_proj_gate",
        )(h, prenorm_gain.reshape(1, k).astype(F32), w, gate_logits.astype(F32))
    if prenorm_gain is not None:
        assert not transpose_out and not row_norms and out_scale == 1.0
        return pl.pallas_call(
            functools.partial(_mm_prenorm_kernel, silu=silu),
            grid=(n_cols // tn, t // tm),
            in_specs=[pl.BlockSpec((tm, k), lambda n, m: (m, 0)),
                      pl.BlockSpec((1, k), lambda n, m: (0, 0)),
                      pl.BlockSpec((k, tn), lambda n, m: (0, n + off))],
            out_specs=pl.BlockSpec((tm, tn), lambda n, m: (m, n)),
            out_shape=jax.ShapeDtypeStruct((t, n_cols), BF16),
            compiler_params=_params(("parallel", "parallel")),
            name="in_proj_prenorm",
        )(h, prenorm_gain.reshape(1, k).astype(F32), w)
    if row_norms:
        assert not transpose_out and tn // LANES == SUBLANES
        out, nrm = pl.pallas_call(
            functools.partial(_mm_norm_kernel, out_scale=out_scale),
            grid=(n_cols // tn, t // tm),
            in_specs=[pl.BlockSpec((tm, k), lambda n, m: (m, 0)),
                      pl.BlockSpec((k, tn), lambda n, m: (0, n + off))],
            out_specs=[pl.BlockSpec((tm, tn), lambda n, m: (m, n)),
                       pl.BlockSpec((1, 1, SUBLANES, LANES), lambda n, m: (n, m, 0, 0))],
            out_shape=[jax.ShapeDtypeStruct((t, n_cols), BF16),
                       jax.ShapeDtypeStruct((n_cols // tn, t // tm, SUBLANES, LANES), F32)],
            compiler_params=_params(("parallel", "parallel")),
            name="in_proj_norms",
        )(h, w)
        nrm = jnp.sqrt(nrm[:, :, :, 0]).transpose(1, 0, 2).reshape(t // tm, n_cols // LANES)
        return out, nrm
    if transpose_out:
        assert out_scale == 1.0
        body, out_shape, scratch = _mm_t_kernel, (n_cols, t), [pltpu.VMEM((tm, tn), F32)]
        out_spec = pl.BlockSpec((tn, tm), lambda n, m: (n, m))
    else:
        body = functools.partial(_mm_kernel, out_scale=out_scale)
        out_shape, scratch = (t, n_cols), []
        out_spec = pl.BlockSpec((tm, tn), lambda n, m: (m, n))
    return pl.pallas_call(
        body,
        grid=(n_cols // tn, t // tm),
        in_specs=[pl.BlockSpec((tm, k), lambda n, m: (m, 0)),
                  pl.BlockSpec((k, tn), lambda n, m: (0, n + off))],
        out_specs=out_spec,
        out_shape=jax.ShapeDtypeStruct(out_shape, BF16),
        scratch_shapes=scratch,
        compiler_params=_params(("parallel", "parallel")),
        name="in_proj_t" if transpose_out else "in_proj",
    )(h, w)


def _fft_tables(l):
    l1 = l // FFT_L2
    two_pi = 2.0 * math.pi
    k1 = jnp.arange(l1, dtype=jnp.int32)
    a1 = ((k1[:, None] * k1[None, :]) % l1).astype(F32) * (two_pi / l1)
    f1 = jnp.concatenate([jnp.cos(a1), -jnp.sin(a1)], axis=0).astype(BF16)
    k2 = jnp.arange(FFT_L2, dtype=jnp.int32)
    kk = k1[:, None, None] + l1 * k2[None, :, None]
    a2 = ((kk * k2[None, None, :]) % l).astype(F32) * (two_pi / l)
    c2, s2 = jnp.cos(a2), jnp.sin(a2)
    g2 = jnp.concatenate([jnp.concatenate([c2, s2], axis=2),
                          jnp.concatenate([-s2, c2], axis=2)], axis=1).astype(BF16)
    c = jnp.arange(FNET_GROUP_DIM, dtype=jnp.int32)
    a3 = ((c[:, None] * c[None, :]) % FNET_GROUP_DIM).astype(F32) * (two_pi / FNET_GROUP_DIM)
    scale = 1.0 / math.sqrt(l * FNET_GROUP_DIM)
    cs = (jnp.concatenate([jnp.cos(a3), jnp.sin(a3)], axis=0) * scale).astype(BF16)
    return f1, g2, cs


def _fft1_kernel(f_ref, u_ref, t_ref):
    l1 = u_ref.shape[1]
    r = jnp.dot(f_ref[...], u_ref[0], preferred_element_type=F32)
    t_ref[0, 0] = r[:l1].astype(t_ref.dtype)
    t_ref[1, 0] = r[l1:].astype(t_ref.dtype)


def _fft2_kernel(t_ref, g_ref, cs_ref, o_ref, p_scr):
    kb = g_ref.shape[0]
    l2 = FFT_L2
    gd = FNET_GROUP_DIM
    for j in range(kb):
        gm = g_ref[j]
        for g in range(FNET_GROUPS):
            cols = slice(g * gd, (g + 1) * gd)
            rhs = jnp.concatenate([t_ref[0, 0, j, :, cols], t_ref[1, 0, j, :, cols]], axis=0)
            p = jnp.dot(gm, rhs, preferred_element_type=F32)
            r0 = (j * FNET_GROUPS + g) * l2
            p_scr[r0:r0 + l2, 0:gd] = p[:l2].astype(p_scr.dtype)
            p_scr[r0:r0 + l2, gd:2 * gd] = p[l2:].astype(p_scr.dtype)
    y = jnp.dot(p_scr[...], cs_ref[...], preferred_element_type=F32)
    for j in range(kb):
        for g in range(FNET_GROUPS):
            r0 = (j * FNET_GROUPS + g) * l2
            cols = slice(j * D_FNET + g * gd, j * D_FNET + (g + 1) * gd)
            o_ref[0, :, cols] = y[r0:r0 + l2].astype(o_ref.dtype)


def _fnet_mix(u, b, l):
    l1, l2 = l // FFT_L2, FFT_L2
    f1, g2, cs = _fft_tables(l)
    wcols = l2 * D_FNET
    w = min(wcols, (2 * MIB) // (2 * l1))
    t = pl.pallas_call(
        _fft1_kernel,
        grid=(b, wcols // w),
        in_specs=[pl.BlockSpec((2 * l1, l1), lambda i, j: (0, 0)),
                  pl.BlockSpec((1, l1, w), lambda i, j: (i, 0, j))],
        out_specs=pl.BlockSpec((2, 1, l1, w), lambda i, j: (0, i, 0, j)),
        out_shape=jax.ShapeDtypeStruct((2, b, l1, wcols), BF16),
        compiler_params=_params(("parallel", "parallel")),
        name="fnet_stage1",
    )(f1, u.reshape(b, l1, wcols))
    kb = min(FFT_KB, l1)
    y = pl.pallas_call(
        _fft2_kernel,
        grid=(b, l1 // kb),
        in_specs=[pl.BlockSpec((2, 1, kb, l2, D_FNET), lambda i, j: (0, i, j, 0, 0)),
                  pl.BlockSpec((kb, 2 * l2, 2 * l2), lambda i, j: (j, 0, 0)),
                  pl.BlockSpec((2 * FNET_GROUP_DIM, FNET_GROUP_DIM), lambda i, j: (0, 0))],
        out_specs=pl.BlockSpec((1, l2, kb * D_FNET), lambda i, j: (i, 0, j)),
        out_shape=jax.ShapeDtypeStruct((b, l2, l1 * D_FNET), BF16),
        scratch_shapes=[pltpu.VMEM((kb * FNET_GROUPS * l2, 2 * FNET_GROUP_DIM), BF16)],
        compiler_params=_params(("parallel", "parallel")),
        name="fnet_stage2",
    )(t.reshape(2, b, l1, l2, D_FNET), g2, cs)
    return y


def _hgrn_kernel(q_ref, i_ref, kf_ref, hif_ref, lof_ref, kb_ref, hib_ref, lob_ref, gate_ref, gn_ref,
                 o_ref, acc_ref):
    hb = HGRN_BLOCK
    half = hb // 2
    n_blocks = q_ref.shape[0] // hb
    group = min(HGRN_GROUP, n_blocks)
    dk = HGRN_HEAD_DIM
    row = lax.broadcasted_iota(jnp.int32, (hb, hb), 0)
    col = lax.broadcasted_iota(jnp.int32, (hb, hb), 1)
    row2 = lax.broadcasted_iota(jnp.int32, (hb, 2 * hb), 0)
    col2 = lax.broadcasted_iota(jnp.int32, (hb, 2 * hb), 1) & (hb - 1)

    gn = gn_ref[...]

    def direction(k_ref, hi_ref, lo_ref, forward):
        keep = (col <= row) if forward else (col >= row)
        tri2 = jnp.where((col2 <= row2) if forward else (col2 >= row2), 1.0, 0.0).astype(BF16)

        def body(step, state_t):
            gi = step if forward else n_blocks // group - 1 - step
            base = gi * (group * hb)
            blocks = [pl.ds(pl.multiple_of(base + u * hb, hb), hb) for u in range(group)]
            stacked = jnp.concatenate([jnp.concatenate([hi_ref[r, :] for r in blocks], axis=1),
                                       jnp.concatenate([lo_ref[r, :] for r in blocks], axis=1)], axis=0)
            a_all = jnp.dot(tri2, stacked, preferred_element_type=F32)
            units = []
            for u in range(group):
                rows = blocks[u]
                q = q_ref[rows, :].astype(F32)
                v = i_ref[rows, :]
                k = k_ref[rows, :].astype(F32)
                a = a_all[:, u * dk:(u + 1) * dk]
                ref = a[half - 1:half, :] if forward else a[half:half + 1, :]
                end = a[hb - 1:hb, :] if forward else a[0:1, :]
                qt = q * jnp.exp2(a - ref)
                kt = k * jnp.exp2(ref - a)
                scores = jnp.where(keep, _dot_nt(qt.astype(BF16), kt.astype(BF16)), 0.0)
                o_intra = jnp.dot(scores.astype(BF16), v, preferred_element_type=F32)
                q_in = (qt * jnp.exp2(ref)).astype(BF16)
                k_end = (kt * jnp.exp2(end - ref)).astype(BF16)
                kv_t = lax.dot_general(v, k_end, (((0,), (0,)), ((), ())),
                                       preferred_element_type=F32)
                units.append((rows, o_intra, q_in, kv_t, jnp.exp2(end)))
            for rows, o_intra, q_in, kv_t, decay in (units if forward else units[::-1]):
                o = o_intra + _dot_nt(q_in, state_t.astype(BF16))
                state_t = state_t * decay + kv_t
                if forward:
                    acc_ref[rows, :] = o
                else:
                    tot = acc_ref[rows, :] + o
                    inv = lax.rsqrt(jnp.mean(tot * tot, axis=-1, keepdims=True) + RMS_EPS)
                    o_ref[rows, :] = (tot * inv * gn * gate_ref[rows, :].astype(F32)).astype(o_ref.dtype)
            return state_t

        lax.fori_loop(0, n_blocks // group, body, jnp.zeros((dk, dk), F32))

    direction(kf_ref, hif_ref, lof_ref, True)
    direction(kb_ref, hib_ref, lob_ref, False)


def _hgrn_mix_gated(arrays, g_norm, b, l):
    nh, dk = HGRN_HEADS, HGRN_HEAD_DIM
    col = pl.BlockSpec((l, dk), lambda i, h: (i, h))
    return pl.pallas_call(
        _hgrn_kernel,
        grid=(b, nh),
        in_specs=[col] * len(arrays) + [pl.BlockSpec((1, dk), lambda i, h: (0, 0))],
        out_specs=pl.BlockSpec((l, dk), lambda i, h: (i, h)),
        out_shape=jax.ShapeDtypeStruct((b * l, D_HGRN), BF16),
        scratch_shapes=[pltpu.VMEM((l, dk), F32)],
        compiler_params=_params(("parallel", "parallel")),
        name="hgrn2",
    )(*arrays, g_norm.reshape(1, dk).astype(F32))


ATTN_TQ = 256
ATTN_TK = 512
ATTN_Q_TILES = 8
ATTN_QK_SCALE = math.sqrt(DIFF_HEAD_DIM ** -0.5 * LOG2E)
ATTN_UNROLL_PAIRS = (6, 2, 1)
ATTN_SKIP_MARGIN = 128.0
ATTN_BOUND_SLACK = 1.0 + 2.0 ** -6
ATTN_FILLER_CONST = -3e38


def _attn_kernel(slopes_ref, nrm_ref, q_ref, k_ref, vt_ref, gate_ref, lq1_ref, lk1_ref, lq2_ref,
                 lk2_ref, subln_ref, o_ref, qbd_ref, d0_ref, ta_ref, tb_ref, m_ref, l_ref, acc_ref,
                 steps_ref, *, lambda_init):
    tq, tk, nqt = ATTN_TQ, ATTN_TK, ATTN_Q_TILES
    n_blocks = k_ref.shape[0] // tk
    n_steps = n_blocks * nqt
    dh = DIFF_HEAD_DIM
    h = pl.program_id(1)
    sup = pl.program_id(2)
    slope2 = slopes_ref[h] * LOG2E
    kk = lax.broadcasted_iota(jnp.int32, (tk, tq), 0)
    qq = lax.broadcasted_iota(jnp.int32, (tk, tq), 1)
    d0 = (kk - qq).astype(F32) * slope2
    d0_ref[0] = d0
    d0_ref[1] = -d0
    for r in range(tk // tq):
        d0_ref[2 + r] = -jnp.abs(d0 - (r * tq) * slope2)

    zeros = jnp.zeros((tq, dh), BF16)
    for t in range(nqt):
        rows = slice(t * tq, (t + 1) * tq)
        qbd_ref[t, 0:tq, 0:dh] = q_ref[rows, 0:dh]
        qbd_ref[t, 0:tq, dh:2 * dh] = zeros
        qbd_ref[t, tq:2 * tq, 0:dh] = zeros
        qbd_ref[t, tq:2 * tq, dh:2 * dh] = q_ref[rows, dh:2 * dh]

    seq_tiles = k_ref.shape[0] // NORM_ROWS
    groups = 2 * D_ATTN // LANES
    tile0 = pl.program_id(0) * seq_tiles
    qk_bound = None
    for c in range(2):
        qn = None
        for r in range(nqt * tq // NORM_ROWS):
            v = nrm_ref[(tile0 + sup * (nqt * tq // NORM_ROWS) + r) * groups + 2 * h + c]
            qn = v if qn is None else jnp.maximum(qn, v)
        kn = None
        for r in range(seq_tiles):
            v = nrm_ref[(tile0 + r) * groups + groups // 2 + 2 * h + c]
            kn = v if kn is None else jnp.maximum(kn, v)
        qk_bound = qn * kn if qk_bound is None else jnp.maximum(qk_bound, qn * kn)
    skip_thr = ATTN_SKIP_MARGIN + 2.0 * ATTN_BOUND_SLACK * qk_bound

    def offsets(s):
        t = s & (nqt - 1)
        k0 = lax.shift_right_logical(s, nqt.bit_length() - 1) * tk
        rel = (sup * nqt + t) * tq - k0
        return t, k0, rel, rel >= tk, rel <= -tq

    def list_step(s, n):
        t, _, rel, before, after = offsets(s)
        gap = jnp.where(before, rel - tk + 1, jnp.where(after, -rel - tq + 1, 0))
        steps_ref[n] = s
        return n + ((gap > 0) & (gap.astype(F32) * slope2 < skip_thr)).astype(jnp.int32)

    n_listed = lax.fori_loop(0, n_steps, list_step, 0, unroll=8)
    n_pairs = (n_listed + 1) // 2
    for extra in range(2):
        steps_ref[n_listed + extra] = n_steps

    def split(i):
        code = steps_ref[i]
        filler = code >= n_steps
        s = jnp.where(filler, 0, code)
        t, k0, rel, before, after = offsets(s)
        diag = lax.shift_right_logical(rel, tq.bit_length() - 1)
        idx = jnp.where(before, 0, jnp.where(after, 1, 2 + diag))
        lin = rel.astype(F32) * slope2
        const = jnp.where(before, -lin, jnp.where(after, lin, 0.0))
        return pl.multiple_of(k0, tk), t, idx, jnp.where(filler, ATTN_FILLER_CONST, const)

    def scores(s, t_ref):
        k0, t, idx, _ = split(s)
        bias = d0_ref[idx]
        sc = _dot_nt(k_ref[pl.ds(k0, tk), :], qbd_ref[t])
        t_ref[:, 0:tq] = sc[:, 0:tq] + bias
        t_ref[:, tq:2 * tq] = sc[:, tq:2 * tq] + bias

    def accumulate(s, t_ref):
        k0, t, _, const = split(s)
        sc = t_ref[...]
        m_old = m_ref[t]
        m_new = jnp.maximum(m_old, jnp.max(sc, axis=0, keepdims=True) + const)
        alpha = jnp.exp2(m_old - m_new)
        p = jnp.exp2(sc - (m_new - const))
        l_ref[t] = alpha * l_ref[t] + jnp.sum(p, axis=0, keepdims=True)
        acc_ref[t] = alpha * acc_ref[t] + jnp.dot(vt_ref[:, pl.ds(k0, tk)], p.astype(BF16),
                                                  preferred_element_type=F32)
        m_ref[t] = m_new

    def own_block(t):
        q0 = (sup * nqt + t) * tq
        k0 = lax.shift_left(lax.shift_right_logical(q0, tk.bit_length() - 1), tk.bit_length() - 1)
        return pl.multiple_of(k0, tk)

    def own_scores(t, t_ref):
        bias = d0_ref[2 + t % (tk // tq)]
        sc = _dot_nt(k_ref[pl.ds(own_block(t), tk), :], qbd_ref[t])
        t_ref[:, 0:tq] = sc[:, 0:tq] + bias
        t_ref[:, tq:2 * tq] = sc[:, tq:2 * tq] + bias

    def own_assign(t, t_ref):
        sc = t_ref[...]
        m_new = jnp.max(sc, axis=0, keepdims=True)
        p = jnp.exp2(sc - m_new)
        l_ref[t] = jnp.sum(p, axis=0, keepdims=True)
        acc_ref[t] = jnp.dot(vt_ref[:, pl.ds(own_block(t), tk)], p.astype(BF16),
                             preferred_element_type=F32)
        m_ref[t] = m_new

    buffers = (ta_ref, tb_ref)
    own_scores(0, ta_ref)
    for t in range(nqt):
        if t + 1 < nqt:
            own_scores(t + 1, buffers[(t + 1) % 2])
        else:
            scores(0, buffers[(t + 1) % 2])
        own_assign(t, buffers[t % 2])

    def pair(i):
        scores(2 * i + 1, tb_ref)
        accumulate(2 * i, ta_ref)
        scores(2 * i + 2, ta_ref)
        accumulate(2 * i + 1, tb_ref)

    done = 0
    for width in ATTN_UNROLL_PAIRS:
        def body(c, carry, width=width, done=done):
            for u in range(width):
                pair(done + c * width + u)
            return carry

        trips = (n_pairs - done) // width
        lax.fori_loop(0, trips, body, 0)
        done = done + trips * width

    lam = (jnp.exp(jnp.sum(lq1_ref[...] * lk1_ref[...], keepdims=True))
           - jnp.exp(jnp.sum(lq2_ref[...] * lk2_ref[...], keepdims=True)) + lambda_init)
    gain = subln_ref[...] * (1.0 - lambda_init)
    for t in range(nqt):
        rows = slice(t * tq, (t + 1) * tq)
        r = 1.0 / l_ref[t]
        o_t = (acc_ref[t, :, 0:tq] * r[:, 0:tq]
               - acc_ref[t, :, tq:2 * tq] * (lam * r[:, tq:2 * tq]))
        inv = lax.rsqrt(jnp.mean(o_t * o_t, axis=0, keepdims=True) + RMS_EPS)
        o_ref[rows, :] = ((o_t * inv).T * (gain * _silu(gate_ref[rows, :].astype(F32)))
                          ).astype(o_ref.dtype)


def _diff_attention_gated(qk, qk_norms, v_t, gate, lq1, lk1, lq2, lk2, subln, lambda_init, b, l):
    nh, dv = DIFF_HEADS, DIFF_V_DIM
    tq, tk, nqt = ATTN_TQ, ATTN_TK, ATTN_Q_TILES
    qs = tq * nqt
    ns = l // qs
    assert l % qs == 0 and qs % NORM_ROWS == 0 and l % NORM_ROWS == 0
    assert nqt & (nqt - 1) == 0 and tq & (tq - 1) == 0
    assert nqt % 2 == 0 and tk & (tk - 1) == 0 and tk % tq == 0
    slopes = jnp.exp2(-8.0 * (jnp.arange(nh, dtype=F32) + 1.0) / nh)
    vec = lambda a: a.reshape(1, -1).astype(F32)
    small = lambda n: pl.BlockSpec((1, n), lambda i, h, s, *_: (0, 0))
    grid_spec = pltpu.PrefetchScalarGridSpec(
        num_scalar_prefetch=2,
        grid=(b, nh, ns),
        in_specs=[pl.BlockSpec((qs, dv), lambda i, h, s, *_: (i * ns + s, h)),
                  pl.BlockSpec((l, dv), lambda i, h, s, *_: (i, nh + h)),
                  pl.BlockSpec((dv, l), lambda i, h, s, *_: (h, i)),
                  pl.BlockSpec((qs, dv), lambda i, h, s, *_: (i * ns + s, h)),
                  small(DIFF_HEAD_DIM), small(DIFF_HEAD_DIM), small(DIFF_HEAD_DIM),
                  small(DIFF_HEAD_DIM), small(dv)],
        out_specs=pl.BlockSpec((qs, dv), lambda i, h, s, *_: (i * ns + s, h)),
        scratch_shapes=[pltpu.VMEM((nqt, 2 * tq, 2 * DIFF_HEAD_DIM), BF16),
                        pltpu.VMEM((2 + tk // tq, tk, tq), F32),
                        pltpu.VMEM((tk, 2 * tq), F32),
                        pltpu.VMEM((tk, 2 * tq), F32),
                        pltpu.VMEM((nqt, 1, 2 * tq), F32), pltpu.VMEM((nqt, 1, 2 * tq), F32),
                        pltpu.VMEM((nqt, dv, 2 * tq), F32),
                        pltpu.SMEM((l // tk * nqt + 8,), jnp.int32)],
    )
    return pl.pallas_call(
        functools.partial(_attn_kernel, lambda_init=lambda_init),
        grid_spec=grid_spec,
        out_shape=jax.ShapeDtypeStruct((b * l, D_ATTN), BF16),
        compiler_params=_params(("parallel", "parallel", "parallel")),
        name="diff_attention",
    )(slopes, qk_norms.reshape(-1), qk, qk, v_t, gate, vec(lq1), vec(lk1), vec(lq2), vec(lk2), vec(subln))


def _out_kernel(*refs, n_act, gated, with_next, fft_l1):
    refs = list(refs)
    act_refs = [refs.pop(0) for _ in range(n_act)]
    gate_ref = refs.pop(0) if gated else None
    w_refs = [refs.pop(0) for _ in range(n_act)]
    x_ref, g_ref = refs.pop(0), refs.pop(0)
    gn_ref = refs.pop(0) if with_next else None
    o_ref = refs.pop(0)
    h_ref = refs.pop(0) if with_next else None
    tm = x_ref.shape[0]
    if fft_l1:
        rows_ref = refs.pop(0)
        width = act_refs[0].shape[2] // fft_l1
        for k1 in range(fft_l1):
            blk = act_refs[0][0, :, k1 * width:(k1 + 1) * width].astype(F32)
            for t in range(width // LANES):
                rows_ref[t, pl.ds(k1, tm // fft_l1, stride=fft_l1), :] = blk[:, t * LANES:(t + 1) * LANES]
    sub = tm // OUT_ROW_SPLIT
    for part in range(OUT_ROW_SPLIT):
        rows = slice(part * sub, (part + 1) * sub)
        acts = [a_ref[rows, :] for a_ref in act_refs[1 if fft_l1 else 0:]]
        if fft_l1:
            acts.insert(0, jnp.concatenate([rows_ref[t, rows, :] for t in range(rows_ref.shape[0])],
                                           axis=1))
        if gated:
            acts[0] = (acts[0].astype(F32) * _silu(gate_ref[rows, :].astype(F32))).astype(BF16)
        y = jnp.dot(acts[0], w_refs[0][...], preferred_element_type=F32)
        for a, w_ref in zip(acts[1:], w_refs[1:]):
            y = y + jnp.dot(a, w_ref[...], preferred_element_type=F32)
        inv = lax.rsqrt(jnp.mean(y * y, axis=-1, keepdims=True) + RMS_EPS)
        out = x_ref[rows, :] + y * inv * g_ref[...]
        o_ref[rows, :] = out
        if with_next:
            inv_n = lax.rsqrt(jnp.mean(out * out, axis=-1, keepdims=True) + RMS_EPS)
            h_ref[rows, :] = (out * inv_n * gn_ref[...]).astype(BF16)


def _out_proj_residual(acts, ws, x, g, gate=None, g_next=None, fft_l1=0, tm=OUT_TILE_ROWS):
    t, d = x.shape
    n_act = len(acts)
    row = lambda n: pl.BlockSpec((tm, n), lambda i: (i, 0))
    vec = lambda: pl.BlockSpec((1, d), lambda i: (0, 0))
    args = list(acts)
    in_specs = [row(a.shape[-1]) for a in acts]
    scratch = []
    if fft_l1:
        per_seq = acts[0].shape[1] * fft_l1 // tm
        in_specs[0] = pl.BlockSpec((1, tm // fft_l1, acts[0].shape[2]),
                                   lambda i: (i // per_seq, i % per_seq, 0))
        width = acts[0].shape[2] // fft_l1
        scratch = [pltpu.VMEM((width // LANES, tm, LANES), F32)]
    if gate is not None:
        args.append(gate)
        in_specs.append(row(gate.shape[1]))
    args += list(ws) + [x, g.reshape(1, d).astype(F32)]
    in_specs += [pl.BlockSpec(w.shape, lambda i: (0, 0)) for w in ws] + [row(d), vec()]
    out_specs, out_shape = row(d), jax.ShapeDtypeStruct((t, d), F32)
    if g_next is not None:
        args.append(g_next.reshape(1, d).astype(F32))
        in_specs.append(vec())
        out_specs = [out_specs, row(d)]
        out_shape = [out_shape, jax.ShapeDtypeStruct((t, d), BF16)]
    return pl.pallas_call(
        functools.partial(_out_kernel, n_act=n_act, gated=gate is not None,
                          with_next=g_next is not None, fft_l1=fft_l1),
        grid=(t // tm,),
        in_specs=in_specs,
        out_specs=out_specs,
        out_shape=out_shape,
        scratch_shapes=scratch,
        compiler_params=_params(("parallel",)),
        name="out_proj_residual",
    )(*args)


def _trunk(x3, wts):
    b, l, d = x3.shape
    x = x3.reshape(b * l, d)

    w_in, g_pre = wts["ev_w_in"], wts["ev_norm_pre"]
    u = _matmul_cols(x, w_in, 0, D_FNET, prenorm_gain=g_pre)
    gate_a = _matmul_cols(x, w_in, D_FNET, D_FNET, prenorm_gain=g_pre)
    base = 2 * D_FNET
    proj = functools.partial(_matmul_cols, x, w_in, n_cols=D_HGRN, prenorm_gain=g_pre)
    logits = wts["hgrn_lb_logits"]
    hgrn_in = [proj(col_off=base, silu=True), proj(col_off=base + D_HGRN),
               *proj(col_off=base + 2 * D_HGRN, gate_logits=logits[0]),
               *proj(col_off=base + 3 * D_HGRN, gate_logits=logits[1]),
               proj(col_off=base + 4 * D_HGRN, silu=True)]
    y_a = _fnet_mix(u, b, l)
    y_b = _hgrn_mix_gated(hgrn_in, wts["hgrn_norm"], b, l)
    w_out = wts["ev_w_out"]
    l1 = l // FFT_L2
    tm_out = OUT_TILE_ROWS
    if (tm_out // l1) % (2 * SUBLANES) == 0:
        fft_l1 = l1
    else:
        fft_l1, y_a = 0, y_a.reshape(b * l, D_FNET)
    x, h = _out_proj_residual([y_a, y_b], [w_out[:D_FNET], w_out[D_FNET:]], x, wts["ev_norm_post"],
                              gate=gate_a, g_next=wts["od_norm_pre"], fft_l1=fft_l1, tm=tm_out)

    lambda_init = 0.8 - 0.6 * math.exp(-0.3 * 1)
    w_in = wts["od_w_in"]
    qk, qk_norms = _matmul_cols(h, w_in, 0, 2 * D_ATTN, out_scale=ATTN_QK_SCALE, row_norms=True,
                                tm=NORM_ROWS)
    v_t = _matmul_cols(h, w_in, 2 * D_ATTN, D_ATTN, transpose_out=True)
    gate = _matmul_cols(h, w_in, 3 * D_ATTN, D_ATTN)
    o = _diff_attention_gated(qk, qk_norms, v_t, gate, wts["lambda_q1"], wts["lambda_k1"], wts["lambda_q2"],
                              wts["lambda_k2"], wts["subln"], lambda_init, b, l)
    x = _out_proj_residual([o], [wts["od_w_out"]], x, wts["od_norm_post"])
    return x.reshape(b, l, d)


def kernel(x_prompt, x_sample, ev_w_in, ev_w_out, ev_norm_pre, ev_norm_post, hgrn_lb_logits,
           hgrn_norm, od_w_in, od_w_out, od_norm_pre, od_norm_post,
           lambda_q1, lambda_k1, lambda_q2, lambda_k2, subln):
    wts = {
        "ev_w_in": ev_w_in[0], "ev_w_out": ev_w_out[0].astype(BF16),
        "ev_norm_pre": ev_norm_pre[0], "ev_norm_post": ev_norm_post[0],
        "hgrn_lb_logits": hgrn_lb_logits, "hgrn_norm": hgrn_norm[0],
        "od_w_in": od_w_in[0], "od_w_out": od_w_out[0].astype(BF16),
        "od_norm_pre": od_norm_pre[0], "od_norm_post": od_norm_post[0],
        "lambda_q1": lambda_q1[0], "lambda_k1": lambda_k1[0],
        "lambda_q2": lambda_q2[0], "lambda_k2": lambda_k2[0], "subln": subln[0],
    }
    return (_trunk(x_prompt, wts), _trunk(x_sample, wts))
```
